```python
import math
import jax, jax.numpy as jnp
from jax import lax
import numpy as np

D_MODEL = 1024
BATCH = 16
SEQ = 2048
DEPTH = 2
DEC_BATCH = 32
DEC_SEQ = 32
PAST_LEN = 1024

CHUNK = 64
N_EVEN = (DEPTH + 1) // 2
N_ODD = DEPTH // 2
D_MIX = D_MODEL
D_FF = 2816
EPS = 1e-6

A_HEADS = 4
A_KEY = 128
A_VAL = 128
A_WIDTH = A_HEADS * A_VAL
HGRN_BLOCK = 64
B_GROUP_DIM = 16
B_WIDTH = D_MIX - A_WIDTH
B_GROUPS = B_WIDTH // B_GROUP_DIM
B_STATE = 64
DT_MIN = 1e-3
DT_MAX = 1e-1
LAMBDA_RE_CEIL = -1e-4
C_CHUNK = 128
C_WIDTH = 512
C_GROUPS = 4
C_GROUP_DIM = C_WIDTH // C_GROUPS
D_HEADS = 8
Q_LORA = 256
KV_LORA = 128
NOPE_DIM = 64
ROPE_DIM = 32
V_DIM = 64
QK_DIM = NOPE_DIM + ROPE_DIM
D_WIDTH = D_HEADS * V_DIM
ROPE_BASE = 10000.0
Q_BLOCK = 128

A_QK = A_HEADS * A_KEY
AB_IN = 2 * A_QK + 2 * A_WIDTH + B_WIDTH
AB_SPLITS = [A_QK, 2 * A_QK, 2 * A_QK + A_WIDTH, 2 * A_QK + 2 * A_WIDTH]
CD_IN = 2 * C_WIDTH + Q_LORA + KV_LORA + ROPE_DIM
CD_SPLITS = [C_WIDTH, 2 * C_WIDTH, 2 * C_WIDTH + Q_LORA, 2 * C_WIDTH + Q_LORA + KV_LORA]

kernel_name = "hybrid_streaming_encoder_step"


def rmsnorm(x, g):
    xf = x.astype(jnp.float32)
    y = xf * lax.rsqrt(jnp.mean(xf * xf, axis=-1, keepdims=True) + EPS)
    return y.astype(x.dtype) * g


def swiglu(h, w_gate, w_up, w_down):
    return (jax.nn.silu(h @ w_gate) * (h @ w_up)) @ w_down


def _hgrn2_block(S, blk):
    q, k, v, logf = blk
    L = q.shape[2]
    b = jnp.cumsum(logf, axis=2)
    causal = jnp.tril(jnp.ones((L, L), dtype=bool))[:, :, None]
    diff = b[:, :, :, None, :] - b[:, :, None, :, :]
    decay = jnp.exp(jnp.where(causal, diff, -jnp.inf))
    att = jnp.einsum('bhtsn,bhsn->bhts', q[:, :, :, None, :] * decay, k)
    o = jnp.einsum('bhts,bhsv->bhtv', att, v) + jnp.einsum('bhtn,bhnv->bhtv', q * jnp.exp(b), S)
    b_end = b[:, :, -1:, :]
    S_new = jnp.exp(b_end[:, :, 0, :, None]) * S + jnp.einsum('bhsn,bhsv->bhnv', k * jnp.exp(b_end - b), v)
    return S_new, o


def hgrn2_mix(q, fz, iv, gz, S0, lb, out_g):
    Bsz, T, _ = q.shape
    L = min(HGRN_BLOCK, T)
    nc = T // L
    f = lb + (1.0 - lb) * jax.nn.sigmoid(fz.astype(jnp.float32))

    def to_blocks(t, d):
        return t.astype(jnp.float32).reshape(Bsz, nc, L, A_HEADS, d).transpose(1, 0, 3, 2, 4)

    S_T, o = lax.scan(_hgrn2_block, S0.astype(jnp.float32),
                      (to_blocks(q, A_KEY), to_blocks(1.0 - f, A_KEY),
                       to_blocks(iv, A_VAL), to_blocks(jnp.log(f), A_KEY)))
    o = o.transpose(1, 0, 3, 2, 4).reshape(Bsz, T, A_HEADS, A_VAL)
    gate = jax.nn.sigmoid(gz.astype(jnp.float32)).reshape(Bsz, T, A_HEADS, A_VAL)
    y = rmsnorm(o, out_g) * gate
    return y.reshape(Bsz, T, A_WIDTH).astype(q.dtype), S_T


def _ssm_combine(e1, e2):
    a1, b1 = e1
    a2, b2 = e2
    return a2 * a1, a2 * b1 + b2


def s5_mix(u, x0_re, x0_im, lam_re, lam_im, log_dt, b_re, b_im, c_re, c_im, d, w_glu, b_glu):
    Bsz, T, _ = u.shape
    f32 = jnp.float32
    ug = u.astype(f32).reshape(Bsz, T, B_GROUPS, B_GROUP_DIM)
    lam = lax.complex(jnp.minimum(lam_re.astype(f32), LAMBDA_RE_CEIL), lam_im.astype(f32))
    dt = jnp.exp(log_dt.astype(f32))[:, None]
    lam_bar = jnp.exp(lam * dt)
    b_bar = ((lam_bar - 1.0) / lam)[:, :, None] * lax.complex(b_re.astype(f32), b_im.astype(f32))
    bu = jnp.einsum('gph,btgh->tbgp', b_bar, ug.astype(jnp.complex64))
    x0 = lax.complex(x0_re.astype(f32), x0_im.astype(f32))
    bu = bu.at[0].add(lam_bar * x0)
    a = jnp.broadcast_to(lam_bar, (T, 1, B_GROUPS, B_STATE))
    _, xs = lax.associative_scan(_ssm_combine, (a, bu), axis=0)
    y = (jnp.einsum('ghp,tbgp->btgh', c_re.astype(f32), xs.real)
         - jnp.einsum('ghp,tbgp->btgh', c_im.astype(f32), xs.imag)
         + d.astype(f32) * ug)
    z = jax.nn.gelu(y.reshape(Bsz, T, B_WIDTH))
    out = z * jax.nn.sigmoid(z @ w_glu.astype(f32) + b_glu.astype(f32))
    x_T = xs[-1]
    return out.astype(u.dtype), x_T.real, x_T.imag


def gmlp_mix(u, v, v_norm, w_s, b_s):
    Bsz, T, _ = v.shape
    L = min(T, C_CHUNK)
    vn = rmsnorm(v, v_norm)
    vc = vn.reshape(Bsz, T // L, L, C_GROUPS, C_GROUP_DIM)
    ws = jnp.tril(w_s[:, :L, :L])
    s = jnp.einsum('gts,bcsgd->bctgd', ws, vc) + b_s[:, :L].T[:, :, None]
    return u * s.reshape(Bsz, T, C_WIDTH), vn[:, T - L:]


def rope_cos_sin(pos):
    half = ROPE_DIM // 2
    inv = ROPE_BASE ** (-jnp.arange(half, dtype=jnp.float32) / half)
    ang = pos.astype(jnp.float32)[:, None] * inv[None, :]
    return jnp.cos(ang), jnp.sin(ang)


def apply_rope(x, cos, sin):
    half = ROPE_DIM // 2
    x1, x2 = x[..., :half], x[..., half:]
    cos = cos.astype(x.dtype)
    sin = sin.astype(x.dtype)
    return jnp.concatenate([x1 * cos - x2 * sin, x1 * sin + x2 * cos], axis=-1)


def attend(q, k, v, qpos, kpos):
    s = jnp.einsum('bqhd,bkhd->bhqk', q, k).astype(jnp.float32) * (QK_DIM ** -0.5)
    visible = (kpos[None, :] // CHUNK) <= (qpos[:, None] // CHUNK)
    p = jax.nn.softmax(jnp.where(visible, s, -jnp.inf), axis=-1).astype(v.dtype)
    return jnp.einsum('bhqk,bkhd->bqhd', p, v)


def attend_query_blocks(q, k, v, qpos, kpos):
    Bsz, T, H, Dq = q.shape
    nb = T // Q_BLOCK
    qb = q.reshape(Bsz, nb, Q_BLOCK, H, Dq).transpose(1, 0, 2, 3, 4)
    pb = qpos.reshape(nb, Q_BLOCK)
    o = lax.map(lambda blk: attend(blk[0], k, v, blk[1], kpos), (qb, pb))
    return o.transpose(1, 0, 2, 3, 4).reshape(Bsz, T, H, V_DIM)


def mla_mix(cq_lin, ckv_lin, kpe_lin, pos, past_ckv, past_kpe,
            q_norm, w_uq, kv_norm, w_ukv, q_gain, k_gain):
    Bsz, T, _ = cq_lin.shape
    cos, sin = rope_cos_sin(pos)
    q = (rmsnorm(cq_lin, q_norm) @ w_uq).reshape(Bsz, T, D_HEADS, QK_DIM)
    q = jnp.concatenate([q[..., :NOPE_DIM],
                         apply_rope(q[..., NOPE_DIM:], cos[:, None], sin[:, None])], axis=-1)
    q = rmsnorm(q, q_gain)
    ckv = rmsnorm(ckv_lin, kv_norm)
    kpe = apply_rope(kpe_lin, cos, sin)
    if past_ckv is None:
        ckv_all, kpe_all, kpos = ckv, kpe, pos
    else:
        n_past = past_ckv.shape[1]
        ckv_all = jnp.concatenate([past_ckv.astype(ckv.dtype), ckv], axis=1)
        kpe_all = jnp.concatenate([past_kpe.astype(kpe.dtype), kpe], axis=1)
        kpos = jnp.concatenate([jnp.arange(n_past, dtype=pos.dtype), pos])
    S = ckv_all.shape[1]
    kv = (ckv_all @ w_ukv).reshape(Bsz, S, D_HEADS, NOPE_DIM + V_DIM)
    k = jnp.concatenate([kv[..., :NOPE_DIM],
                         jnp.broadcast_to(kpe_all[:, :, None, :], (Bsz, S, D_HEADS, ROPE_DIM))], axis=-1)
    k = rmsnorm(k, k_gain)
    v = kv[..., NOPE_DIM:]
    if past_ckv is None:
        o = attend_query_blocks(q, k, v, pos, kpos)
    else:
        o = attend(q, k, v, pos, kpos)
    return o.reshape(Bsz, T, D_WIDTH), ckv, kpe


def trunk(x, pos, w, past):
    Bsz = x.shape[0]
    lb_all = jnp.cumsum(jax.nn.softmax(w['hgrn_lb_logits'].astype(jnp.float32), axis=0), axis=0)
    hgrn_new, s5re_new, s5im_new, gv_new, ckv_new, kpe_new = [], [], [], [], [], []
    for l in range(DEPTH):
        j = l // 2
        x = x + 0.5 * swiglu(rmsnorm(x, w['ffn1_norm'][l]), w['ffn1_w_gate'][l], w['ffn1_w_up'][l], w['ffn1_w_down'][l])
        h = rmsnorm(x, w['mix_norm'][l])
        if l % 2 == 0:
            q, fz, iv, gz, u = jnp.split(h @ w['ab_w_in'][j], AB_SPLITS, axis=-1)
            if past is None:
                S0 = jnp.zeros((Bsz, A_HEADS, A_KEY, A_VAL), jnp.float32)
                x0_re = jnp.zeros((Bsz, B_GROUPS, B_STATE), jnp.float32)
                x0_im = jnp.zeros((Bsz, B_GROUPS, B_STATE), jnp.float32)
            else:
                S0, x0_re, x0_im = past[0][j], past[1][j], past[2][j]
            a_out, S_T = hgrn2_mix(q, fz, iv, gz, S0, lb_all[l], w['hgrn_out_norm'][j])
            b_out, xr, xi = s5_mix(u, x0_re, x0_im, w['s5_lambda_re'][j], w['s5_lambda_im'][j], w['s5_log_dt'][j],
                                   w['s5_b_re'][j], w['s5_b_im'][j], w['s5_c_re'][j], w['s5_c_im'][j],
                                   w['s5_d'][j], w['s5_w_glu'][j], w['s5_b_glu'][j])
            mixed = jnp.concatenate([a_out, b_out], axis=-1) @ w['ab_w_out'][j]
            hgrn_new.append(S_T)
            s5re_new.append(xr)
            s5im_new.append(xi)
        else:
            uc, vc, cq, ckv_lin, kpe_lin = jnp.split(h @ w['cd_w_in'][j], CD_SPLITS, axis=-1)
            c_out, v_rows = gmlp_mix(jax.nn.gelu(uc), jax.nn.gelu(vc), w['gmlp_v_norm'][j],
                                     w['gmlp_w_s'][j], w['gmlp_b_s'][j])
            if past is None:
                past_ckv, past_kpe = None, None
            else:
                past_ckv, past_kpe = past[3][j], past[4][j]
            d_out, ckv, kpe = mla_mix(cq, ckv_lin, kpe_lin, pos, past_ckv, past_kpe,
                                      w['mla_q_norm'][j], w['mla_w_uq'][j], w['mla_kv_norm'][j],
                                      w['mla_w_ukv'][j], w['mla_q_gain'][j], w['mla_k_gain'][j])
            mixed = jnp.concatenate([c_out, d_out], axis=-1) @ w['cd_w_out'][j]
            gv_new.append(v_rows)
            ckv_new.append(ckv)
            kpe_new.append(kpe)
        x = x + mixed
        x = x + 0.5 * swiglu(rmsnorm(x, w['ffn2_norm'][l]), w['ffn2_w_gate'][l], w['ffn2_w_up'][l], w['ffn2_w_down'][l])
    return x, (jnp.stack(hgrn_new), jnp.stack(s5re_new), jnp.stack(s5im_new),
               jnp.stack(gv_new), jnp.stack(ckv_new), jnp.stack(kpe_new))


def setup_inputs(seed: int = 0) -> dict:
    key = jax.random.key(seed)
    keys = jax.random.split(key, 64)
    counter = [0]

    def nk():
        counter[0] += 1
        return keys[counter[0] - 1]

    def nrm(shape, scale=1.0):
        return scale * jax.random.normal(nk(), shape, jnp.float32)

    def gain(shape):
        return 1.0 + 0.01 * jax.random.normal(nk(), shape, jnp.float32)

    inp = {}
    inp['x_prompt'] = nrm((BATCH, SEQ, D_MODEL))
    inp['x_sample'] = nrm((DEC_BATCH, DEC_SEQ, D_MODEL))
    inp['state_hgrn'] = nrm((N_EVEN, DEC_BATCH, A_HEADS, A_KEY, A_VAL), 0.5)
    inp['state_s5_re'] = nrm((N_EVEN, DEC_BATCH, B_GROUPS, B_STATE), 0.1)
    inp['state_s5_im'] = nrm((N_EVEN, DEC_BATCH, B_GROUPS, B_STATE), 0.1)
    inp['cache_mla_ckv'] = nrm((N_ODD, DEC_BATCH, PAST_LEN, KV_LORA))
    inp['cache_mla_kpe'] = nrm((N_ODD, DEC_BATCH, PAST_LEN, ROPE_DIM))
    inp['ffn1_norm'] = gain((DEPTH, D_MODEL))
    inp['ffn1_w_gate'] = nrm((DEPTH, D_MODEL, D_FF), D_MODEL ** -0.5)
    inp['ffn1_w_up'] = nrm((DEPTH, D_MODEL, D_FF), D_MODEL ** -0.5)
    inp['ffn1_w_down'] = nrm((DEPTH, D_FF, D_MODEL), D_FF ** -0.5)
    inp['mix_norm'] = gain((DEPTH, D_MODEL))
    inp['ffn2_norm'] = gain((DEPTH, D_MODEL))
    inp['ffn2_w_gate'] = nrm((DEPTH, D_MODEL, D_FF), D_MODEL ** -0.5)
    inp['ffn2_w_up'] = nrm((DEPTH, D_MODEL, D_FF), D_MODEL ** -0.5)
    inp['ffn2_w_down'] = nrm((DEPTH, D_FF, D_MODEL), D_FF ** -0.5)
    inp['ab_w_in'] = nrm((N_EVEN, D_MODEL, AB_IN), D_MODEL ** -0.5)
    inp['hgrn_lb_logits'] = nrm((DEPTH + 1, A_QK), 0.5)
    inp['hgrn_out_norm'] = gain((N_EVEN, A_VAL))
    inp['s5_lambda_re'] = -0.5 + nrm((N_EVEN, B_GROUPS, B_STATE), 0.01)
    inp['s5_lambda_im'] = (jnp.broadcast_to(jnp.pi * jnp.arange(B_STATE, dtype=jnp.float32), (N_EVEN, B_GROUPS, B_STATE))
                           + nrm((N_EVEN, B_GROUPS, B_STATE), 0.01))
    inp['s5_log_dt'] = jax.random.uniform(nk(), (N_EVEN, B_GROUPS), jnp.float32,
                                          minval=math.log(DT_MIN), maxval=math.log(DT_MAX))
    inp['s5_b_re'] = nrm((N_EVEN, B_GROUPS, B_STATE, B_GROUP_DIM), (2 * B_GROUP_DIM) ** -0.5)
    inp['s5_b_im'] = nrm((N_EVEN, B_GROUPS, B_STATE, B_GROUP_DIM), (2 * B_GROUP_DIM) ** -0.5)
    inp['s5_c_re'] = nrm((N_EVEN, B_GROUPS, B_GROUP_DIM, B_STATE), B_STATE ** -0.5)
    inp['s5_c_im'] = nrm((N_EVEN, B_GROUPS, B_GROUP_DIM, B_STATE), B_STATE ** -0.5)
    inp['s5_d'] = nrm((N_EVEN, B_GROUPS, B_GROUP_DIM))
    inp['s5_w_glu'] = nrm((N_EVEN, B_WIDTH, B_WIDTH), B_WIDTH ** -0.5)
    inp['s5_b_glu'] = nrm((N_EVEN, B_WIDTH), 0.01)
    inp['ab_w_out'] = nrm((N_EVEN, D_MIX, D_MODEL), D_MIX ** -0.5)
    inp['cd_w_in'] = nrm((N_ODD, D_MODEL, CD_IN), D_MODEL ** -0.5)
    inp['gmlp_v_norm'] = gain((N_ODD, C_WIDTH))
    inp['gmlp_w_s'] = nrm((N_ODD, C_GROUPS, C_CHUNK, C_CHUNK), C_CHUNK ** -0.5)
    inp['gmlp_b_s'] = gain((N_ODD, C_GROUPS, C_CHUNK))
    inp['mla_q_norm'] = gain((N_ODD, Q_LORA))
    inp['mla_w_uq'] = nrm((N_ODD, Q_LORA, D_HEADS * QK_DIM), Q_LORA ** -0.5)
    inp['mla_kv_norm'] = gain((N_ODD, KV_LORA))
    inp['mla_w_ukv'] = nrm((N_ODD, KV_LORA, D_HEADS * (NOPE_DIM + V_DIM)), KV_LORA ** -0.5)
    inp['mla_q_gain'] = gain((N_ODD, QK_DIM))
    inp['mla_k_gain'] = gain((N_ODD, QK_DIM))
    inp['cd_w_out'] = nrm((N_ODD, D_MIX, D_MODEL), D_MIX ** -0.5)
    return inp


def reference(x_prompt, x_sample, state_hgrn, state_s5_re, state_s5_im, cache_mla_ckv, cache_mla_kpe,
              ffn1_norm, ffn1_w_gate, ffn1_w_up, ffn1_w_down, mix_norm,
              ffn2_norm, ffn2_w_gate, ffn2_w_up, ffn2_w_down,
              ab_w_in, hgrn_lb_logits, hgrn_out_norm, s5_lambda_re, s5_lambda_im, s5_log_dt,
              s5_b_re, s5_b_im, s5_c_re, s5_c_im, s5_d, s5_w_glu, s5_b_glu, ab_w_out,
              cd_w_in, gmlp_v_norm, gmlp_w_s, gmlp_b_s, mla_q_norm, mla_w_uq, mla_kv_norm,
              mla_w_ukv, mla_q_gain, mla_k_gain, cd_w_out):
    w = dict(ffn1_norm=ffn1_norm, ffn1_w_gate=ffn1_w_gate, ffn1_w_up=ffn1_w_up, ffn1_w_down=ffn1_w_down,
             mix_norm=mix_norm, ffn2_norm=ffn2_norm, ffn2_w_gate=ffn2_w_gate, ffn2_w_up=ffn2_w_up,
             ffn2_w_down=ffn2_w_down, ab_w_in=ab_w_in, hgrn_lb_logits=hgrn_lb_logits,
             hgrn_out_norm=hgrn_out_norm, s5_lambda_re=s5_lambda_re, s5_lambda_im=s5_lambda_im,
             s5_log_dt=s5_log_dt, s5_b_re=s5_b_re, s5_b_im=s5_b_im, s5_c_re=s5_c_re, s5_c_im=s5_c_im,
             s5_d=s5_d, s5_w_glu=s5_w_glu, s5_b_glu=s5_b_glu, ab_w_out=ab_w_out, cd_w_in=cd_w_in,
             gmlp_v_norm=gmlp_v_norm, gmlp_w_s=gmlp_w_s, gmlp_b_s=gmlp_b_s, mla_q_norm=mla_q_norm,
             mla_w_uq=mla_w_uq, mla_kv_norm=mla_kv_norm, mla_w_ukv=mla_w_ukv, mla_q_gain=mla_q_gain,
             mla_k_gain=mla_k_gain, cd_w_out=cd_w_out)
    past_len = cache_mla_ckv.shape[2]
    pos_prompt = jnp.arange(x_prompt.shape[1], dtype=jnp.int32)
    pos_sample = past_len + jnp.arange(x_sample.shape[1], dtype=jnp.int32)
    y_prompt, st_p = trunk(x_prompt, pos_prompt, w, None)
    y_sample, st_s = trunk(x_sample, pos_sample, w,
                           (state_hgrn, state_s5_re, state_s5_im, cache_mla_ckv, cache_mla_kpe))
    hgrn_p, s5re_p, s5im_p, gv_p, ckv_p, kpe_p = st_p
    hgrn_s, s5re_s, s5im_s, gv_s, ckv_s, kpe_s = st_s
    return (y_prompt, y_sample, hgrn_p, hgrn_s, s5re_p, s5im_p, s5re_s, s5im_s,
            gv_p, gv_s, ckv_p, kpe_p, ckv_s, kpe_s)
```

```python
import functools
import math

import jax
import jax.numpy as jnp
from jax import lax
from jax.experimental import pallas as pl
from jax.experimental.pallas import tpu as pltpu

D_MODEL = 1024
D_FF = 2816
EPS = 1e-6
A_HEADS = 4
A_KEY = 128
A_VAL = 128
A_WIDTH = A_HEADS * A_VAL
HGRN_BLOCK = 64
B_GROUP_DIM = 16
B_WIDTH = 512
B_GROUPS = 32
B_STATE = 64
LAMBDA_RE_CEIL = -1e-4
C_CHUNK = 128
C_WIDTH = 512
C_GROUPS = 4
C_GROUP_DIM = 128
D_HEADS = 8
Q_LORA = 256
KV_LORA = 128
NOPE_DIM = 64
ROPE_DIM = 32
V_DIM = 64
QK_DIM = NOPE_DIM + ROPE_DIM
ROPE_BASE = 10000.0
CHUNK = 64
A_QK = A_HEADS * A_KEY
AB_IN = 2 * A_QK + 2 * A_WIDTH + B_WIDTH
CD_IN = 2 * C_WIDTH + Q_LORA + KV_LORA + ROPE_DIM
CD_IN_PAD = 1536

LANES = 128
SUBLANES = 8
VMEM_LIMIT = 56 * 1024 * 1024
HGRN_SUB = 16
HEAD_PAD = LANES
NEG_BIG = -1e30

BF16 = jnp.bfloat16
F32 = jnp.float32


def _cparams(sem):
    return pltpu.CompilerParams(dimension_semantics=sem, vmem_limit_bytes=VMEM_LIMIT)


def _rms(xf, width):
    return xf * lax.rsqrt(jnp.sum(xf * xf, axis=-1, keepdims=True) * (1.0 / width) + EPS)


def _dot(a, b):
    return jnp.dot(a, b, preferred_element_type=F32)


def _dot_nt(a, b):
    return lax.dot_general(a, b, (((1,), (1,)), ((), ())), preferred_element_type=F32)


def _dot_tn(a, b):
    return lax.dot_general(a, b, (((0,), (0,)), ((), ())), preferred_element_type=F32)


def _div_pow2(x, d):
    assert d & (d - 1) == 0
    return lax.shift_right_logical(x, jnp.int32(d.bit_length() - 1))


def _mod_pow2(x, d):
    assert d & (d - 1) == 0
    return x & jnp.int32(d - 1)


def _row_tile(n, cap):
    t = min(n, cap)
    assert n % t == 0, (n, t)
    return t


def _ffn_kernel(x_ref, g_ref, wg_ref, wu_ref, wd_ref, o_ref, h_ref, acc_ref):
    j = pl.program_id(1)

    @pl.when(j == 0)
    def _():
        xf = x_ref[...]
        h_ref[...] = (_rms(xf, D_MODEL) * g_ref[...]).astype(BF16)
        acc_ref[...] = jnp.zeros_like(acc_ref)

    h = h_ref[...]
    gate = _dot(h, wg_ref[...])
    up = _dot(h, wu_ref[...])
    act = (gate * jax.nn.sigmoid(gate) * up).astype(BF16)
    acc_ref[...] += _dot(act, wd_ref[...])

    @pl.when(j == pl.num_programs(1) - 1)
    def _():
        o_ref[...] = x_ref[...] + 0.5 * acc_ref[...]


def ffn(x, g, wg, wu, wd, tm_cap=1024, tf=256):
    n = x.shape[0]
    tm = _row_tile(n, tm_cap)
    return pl.pallas_call(
        _ffn_kernel,
        grid=(n // tm, D_FF // tf),
        in_specs=[
            pl.BlockSpec((tm, D_MODEL), lambda i, j: (i, 0)),
            pl.BlockSpec((1, D_MODEL), lambda i, j: (0, 0)),
            pl.BlockSpec((D_MODEL, tf), lambda i, j: (0, j)),
            pl.BlockSpec((D_MODEL, tf), lambda i, j: (0, j)),
            pl.BlockSpec((tf, D_MODEL), lambda i, j: (j, 0)),
        ],
        out_specs=pl.BlockSpec((tm, D_MODEL), lambda i, j: (i, 0)),
        out_shape=jax.ShapeDtypeStruct((n, D_MODEL), F32),
        scratch_shapes=[pltpu.VMEM((tm, D_MODEL), BF16), pltpu.VMEM((tm, D_MODEL), F32)],
        compiler_params=_cparams(("parallel", "arbitrary")),
        name="ffn",
    )(x, g.reshape(1, D_MODEL), wg, wu, wd)


def _norm_matmul_kernel(x_ref, g_ref, w_ref, o_ref, h_ref):
    @pl.when(pl.program_id(1) == 0)
    def _():
        h_ref[...] = (_rms(x_ref[...], D_MODEL) * g_ref[...]).astype(BF16)

    o_ref[...] = _dot(h_ref[...], w_ref[...])


def norm_matmul(x, g, w, tm_cap=1024, tn=512):
    n = x.shape[0]
    m = w.shape[1]
    tm = _row_tile(n, tm_cap)
    return pl.pallas_call(
        _norm_matmul_kernel,
        grid=(n // tm, m // tn),
        in_specs=[
            pl.BlockSpec((tm, D_MODEL), lambda i, j: (i, 0)),
            pl.BlockSpec((1, D_MODEL), lambda i, j: (0, 0)),
            pl.BlockSpec((D_MODEL, tn), lambda i, j: (0, j)),
        ],
        out_specs=pl.BlockSpec((tm, tn), lambda i, j: (i, j)),
        out_shape=jax.ShapeDtypeStruct((n, m), F32),
        scratch_shapes=[pltpu.VMEM((tm, D_MODEL), BF16)],
        compiler_params=_cparams(("parallel", "arbitrary")),
        name="norm_matmul",
    )(x, g.reshape(1, D_MODEL), w)


def _proj_res_kernel(x_ref, a1_ref, a2_ref, w1_ref, w2_ref, o_ref):
    o_ref[...] = x_ref[...] + _dot(a1_ref[...], w1_ref[...]) + _dot(a2_ref[...], w2_ref[...])


def proj_res(x, a1, a2, w1, w2, tm_cap=1024):
    n = x.shape[0]
    tm = _row_tile(n, tm_cap)
    k1, k2 = a1.shape[1], a2.shape[1]
    return pl.pallas_call(
        _proj_res_kernel,
        grid=(n // tm,),
        in_specs=[
            pl.BlockSpec((tm, D_MODEL), lambda i: (i, 0)),
            pl.BlockSpec((tm, k1), lambda i: (i, 0)),
            pl.BlockSpec((tm, k2), lambda i: (i, 0)),
            pl.BlockSpec((k1, D_MODEL), lambda i: (0, 0)),
            pl.BlockSpec((k2, D_MODEL), lambda i: (0, 0)),
        ],
        out_specs=pl.BlockSpec((tm, D_MODEL), lambda i: (i, 0)),
        out_shape=jax.ShapeDtypeStruct((n, D_MODEL), F32),
        compiler_params=_cparams(("parallel",)),
        name="proj_res",
    )(x, a1, a2, w1, w2)


def _hgrn_kernel(q_ref, f_ref, i_ref, g_ref, lb_ref, og_ref, s0_ref, y_ref, st_ref, state_ref,
                 *, tt, blk, has_past):
    t_idx = pl.program_id(1)
    n_chunks = tt // blk
    n_sub = blk // HGRN_SUB

    @pl.when(t_idx == 0)
    def _():
        if has_past:
            for h in range(A_HEADS):
                state_ref[h] = s0_ref[h].T
        else:
            state_ref[...] = jnp.zeros_like(state_ref)

    row = lax.broadcasted_iota(jnp.int32, (tt, tt), 0)
    col = lax.broadcasted_iota(jnp.int32, (tt, tt), 1)
    same_chunk = _div_pow2(row, blk) == _div_pow2(col, blk)
    causal = same_chunk & (col <= row)
    tril = jnp.where(causal, 1.0, 0.0).astype(BF16)
    r1 = lax.broadcasted_iota(jnp.int32, (tt, LANES), 0)
    sub_of_row = _div_pow2(_mod_pow2(r1, blk), HGRN_SUB)

    for h in range(A_HEADS):
        sl = slice(h * LANES, (h + 1) * LANES)
        q = q_ref[:, sl]
        fz = f_ref[:, sl]
        v = i_ref[:, sl]
        gz = g_ref[:, sl]
        lb = lb_ref[:, sl]
        f = lb + (1.0 - lb) * jax.nn.sigmoid(fz)
        k = 1.0 - f
        logf = jnp.log(f)
        hi = logf.astype(BF16)
        mid = (logf - hi.astype(F32)).astype(BF16)
        b = _dot(tril, hi) + _dot(tril, mid)

        refs = []
        for i in range(n_sub):
            parts = []
            for c in range(n_chunks):
                r = c * blk + i * HGRN_SUB + HGRN_SUB // 2 - 1
                parts.append(jnp.broadcast_to(b[r:r + 1, :], (blk, LANES)))
            refs.append(parts[0] if n_chunks == 1 else jnp.concatenate(parts, axis=0))
        ends = []
        for c in range(n_chunks):
            r = c * blk + blk - 1
            ends.append(b[r:r + 1, :])
        b_end_rows = (jnp.broadcast_to(ends[0], (blk, LANES)) if n_chunks == 1 else
                      jnp.concatenate([jnp.broadcast_to(e, (blk, LANES)) for e in ends], axis=0))

        own_ref = refs[n_sub - 1]
        for i in range(n_sub - 2, -1, -1):
            own_ref = jnp.where(sub_of_row == i, refs[i], own_ref)
        q_t = q * jnp.exp(b - own_ref)
        q_hat = jnp.concatenate(
            [jnp.where(sub_of_row == i, q_t, 0.0).astype(BF16) for i in range(n_sub)], axis=1)
        k_hat = jnp.concatenate(
            [(k * jnp.exp(jnp.where(sub_of_row <= i, refs[i] - b, NEG_BIG))).astype(BF16)
             for i in range(n_sub)], axis=1)
        att = _dot_nt(q_hat, k_hat)
        att = jnp.where(causal, att, 0.0).astype(BF16)
        v_bf = v.astype(BF16)
        o = _dot(att, v_bf)

        q_e = (q * jnp.exp(b)).astype(BF16)
        k_d = (k * jnp.exp(b_end_rows - b)).astype(BF16)
        s_t = state_ref[h]
        o_parts = []
        for c in range(n_chunks):
            rs = slice(c * blk, (c + 1) * blk)
            o_parts.append(o[rs] + _dot_nt(q_e[rs], s_t.astype(BF16)))
            s_t = s_t * jnp.exp(ends[c]) + _dot_tn(v_bf[rs], k_d[rs])
        state_ref[h] = s_t
        o = o_parts[0] if n_chunks == 1 else jnp.concatenate(o_parts, axis=0)

        y = _rms(o, A_VAL) * og_ref[...] * jax.nn.sigmoid(gz)
        y_ref[:, sl] = y.astype(y_ref.dtype)

    @pl.when(t_idx == pl.num_programs(1) - 1)
    def _():
        for h in range(A_HEADS):
            st_ref[h] = state_ref[h].T


def hgrn2(proj, lb, out_g, s0, tt_cap=256):
    bsz, t, _ = proj.shape
    blk = min(HGRN_BLOCK, t)
    tt = _row_tile(t, tt_cap)
    has_past = s0 is not None
    if s0 is None:
        s0 = jnp.zeros((bsz, A_HEADS, A_KEY, A_VAL), F32)
    seg = lambda s: pl.BlockSpec((None, tt, A_WIDTH), lambda b, i, s=s: (b, i, s))
    kern = functools.partial(_hgrn_kernel, tt=tt, blk=blk, has_past=has_past)
    return pl.pallas_call(
        kern,
        grid=(bsz, t // tt),
        in_specs=[
            seg(0), seg(1), seg(2), seg(3),
            pl.BlockSpec((1, A_QK), lambda b, i: (0, 0)),
            pl.BlockSpec((1, A_VAL), lambda b, i: (0, 0)),
            pl.BlockSpec((None, A_HEADS, A_KEY, A_VAL), lambda b, i: (b, 0, 0, 0)),
        ],
        out_specs=[
            pl.BlockSpec((None, tt, A_WIDTH), lambda b, i: (b, i, 0)),
            pl.BlockSpec((None, A_HEADS, A_KEY, A_VAL), lambda b, i: (b, 0, 0, 0)),
        ],
        out_shape=[
            jax.ShapeDtypeStruct((bsz, t, A_WIDTH), BF16),
            jax.ShapeDtypeStruct((bsz, A_HEADS, A_KEY, A_VAL), F32),
        ],
        scratch_shapes=[pltpu.VMEM((A_HEADS, A_VAL, A_KEY), F32)],
        compiler_params=_cparams(("parallel", "arbitrary")),
        name="hgrn2",
    )(proj, proj, proj, proj, lb.reshape(1, A_QK), out_g.reshape(1, A_VAL), s0)


S5_SUPER = 4
S5_SUPER_U = B_WIDTH // S5_SUPER
S5_SUPER_X = 2 * (B_GROUPS // S5_SUPER) * B_STATE
S5_HALF = S5_SUPER_X // 2
S5_STATE_W = S5_SUPER * S5_SUPER_X


def _s5_kernel(u_ref, wb_ref, wc_ref, lam_ref, d_ref, wglu_ref, bglu_ref, x0_ref,
               o_ref, xt_ref, xbuf_ref, state_ref, *, tt):
    t_idx = pl.program_id(1)
    rows = tt * SUBLANES

    @pl.when(t_idx == 0)
    def _():
        state_ref[...] = x0_ref[...]

    u = u_ref[...].reshape(rows, B_WIDTH)
    u_bf = u.astype(BF16)
    for j in range(S5_SUPER):
        xbuf_ref[:, j * S5_SUPER_X:(j + 1) * S5_SUPER_X] = _dot(
            u_bf[:, j * S5_SUPER_U:(j + 1) * S5_SUPER_U], wb_ref[j])

    lam_re = jnp.broadcast_to(lam_ref[0:1, :], (SUBLANES, S5_STATE_W))
    lam_im = jnp.broadcast_to(lam_ref[1:2, :], (SUBLANES, S5_STATE_W))

    def step(t, x):
        r0 = pl.multiple_of(t * SUBLANES, SUBLANES)
        bu = xbuf_ref[pl.ds(r0, SUBLANES), :]
        pieces = []
        for j in range(S5_SUPER):
            c0 = j * S5_SUPER_X
            xr = x[:, c0:c0 + S5_HALF]
            xi = x[:, c0 + S5_HALF:c0 + S5_SUPER_X]
            lr = lam_re[:, c0:c0 + S5_HALF]
            li = lam_im[:, c0:c0 + S5_HALF]
            pieces.append(lr * xr - li * xi + bu[:, c0:c0 + S5_HALF])
            pieces.append(lr * xi + li * xr + bu[:, c0 + S5_HALF:c0 + S5_SUPER_X])
        x_new = jnp.concatenate(pieces, axis=1)
        xbuf_ref[pl.ds(r0, SUBLANES), :] = x_new
        return x_new

    x_last = lax.fori_loop(0, tt, step, state_ref[...])
    state_ref[...] = x_last

    ys = []
    for j in range(S5_SUPER):
        xs = xbuf_ref[:, j * S5_SUPER_X:(j + 1) * S5_SUPER_X].astype(BF16)
        ys.append(_dot(xs, wc_ref[j]))
    y = jnp.concatenate(ys, axis=1) + d_ref[...] * u
    z = jax.nn.gelu(y)
    gate = jax.nn.sigmoid(_dot(z.astype(BF16), wglu_ref[...]) + bglu_ref[...])
    o_ref[...] = (z * gate).astype(o_ref.dtype).reshape(tt, SUBLANES, B_WIDTH)

    @pl.when(t_idx == pl.num_programs(1) - 1)
    def _():
        xt_ref[...] = x_last


def s5_mix(u_tb, wb, wc, lam, d, w_glu, b_glu, x0, tt_cap=64):
    t, bsz, _ = u_tb.shape
    tt = _row_tile(t, tt_cap)
    kern = functools.partial(_s5_kernel, tt=tt)
    return pl.pallas_call(
        kern,
        grid=(bsz // SUBLANES, t // tt),
        in_specs=[
            pl.BlockSpec((tt, SUBLANES, B_WIDTH), lambda b, i: (i, b, 0)),
            pl.BlockSpec((S5_SUPER, S5_SUPER_U, S5_SUPER_X), lambda b, i: (0, 0, 0)),
            pl.BlockSpec((S5_SUPER, S5_SUPER_X, S5_SUPER_U), lambda b, i: (0, 0, 0)),
            pl.BlockSpec((2, S5_STATE_W), lambda b, i: (0, 0)),
            pl.BlockSpec((1, B_WIDTH), lambda b, i: (0, 0)),
            pl.BlockSpec((B_WIDTH, B_WIDTH), lambda b, i: (0, 0)),
            pl.BlockSpec((1, B_WIDTH), lambda b, i: (0, 0)),
            pl.BlockSpec((SUBLANES, S5_STATE_W), lambda b, i: (b, 0)),
        ],
        out_specs=[
            pl.BlockSpec((tt, SUBLANES, B_WIDTH), lambda b, i: (i, b, 0)),
            pl.BlockSpec((SUBLANES, S5_STATE_W), lambda b, i: (b, 0)),
        ],
        out_shape=[
            jax.ShapeDtypeStruct((t, bsz, B_WIDTH), BF16),
            jax.ShapeDtypeStruct((bsz, S5_STATE_W), F32),
        ],
        scratch_shapes=[pltpu.VMEM((tt * SUBLANES, S5_STATE_W), F32),
                        pltpu.VMEM((SUBLANES, S5_STATE_W), F32)],
        compiler_params=_cparams(("parallel", "arbitrary")),
        name="s5_mix",
    )(u_tb, wb, wc, lam, d.reshape(1, B_WIDTH), w_glu, b_glu.reshape(1, B_WIDTH), x0)


def _s5_pack_cols(a):
    return a.reshape(a.shape[:-2] + (S5_SUPER, S5_HALF))


def s5_prepare(lam_re, lam_im, log_dt, b_re, b_im, c_re, c_im):
    lam = lax.complex(jnp.minimum(lam_re, LAMBDA_RE_CEIL), lam_im)
    dt = jnp.exp(log_dt)[:, None]
    lam_bar = jnp.exp(lam * dt)
    b_bar = ((lam_bar - 1.0) / lam)[:, :, None] * lax.complex(b_re, b_im)
    gps = B_GROUPS // S5_SUPER
    eye = jnp.eye(gps, dtype=F32)

    def in_block(bm):
        bm = bm.reshape(S5_SUPER, gps, B_STATE, B_GROUP_DIM)
        blk = jnp.einsum('jgph,gk->jghkp', bm, eye)
        return blk.reshape(S5_SUPER, gps * B_GROUP_DIM, gps * B_STATE)

    def out_block(cm):
        cm = cm.reshape(S5_SUPER, gps, B_GROUP_DIM, B_STATE)
        blk = jnp.einsum('jghp,gk->jkpgh', cm, eye)
        return blk.reshape(S5_SUPER, gps * B_STATE, gps * B_GROUP_DIM)

    wb = jnp.concatenate([in_block(b_bar.real), in_block(b_bar.imag)], axis=2).astype(BF16)
    wc = jnp.concatenate([out_block(c_re), out_block(-c_im)], axis=1).astype(BF16)
    lam_rows = jnp.stack([
        jnp.concatenate([_s5_pack_cols(lam_bar.real)] * 2, axis=-1).reshape(S5_STATE_W),
        jnp.concatenate([_s5_pack_cols(lam_bar.imag)] * 2, axis=-1).reshape(S5_STATE_W)])
    return wb, wc, lam_rows


def s5_pack_state(x_re, x_im):
    bsz = x_re.shape[0]
    return jnp.concatenate([_s5_pack_cols(x_re), _s5_pack_cols(x_im)], axis=-1).reshape(bsz, S5_STATE_W)


def s5_unpack_state(x):
    bsz = x.shape[0]
    x = x.reshape(bsz, S5_SUPER, 2, S5_HALF)
    return (x[:, :, 0].reshape(bsz, B_GROUPS, B_STATE), x[:, :, 1].reshape(bsz, B_GROUPS, B_STATE))


def _gmlp_kernel(u_ref, v_ref, gn_ref, ws_ref, bs_ref, o_ref, vr_ref, *, tt, chunk):
    n_chunks = tt // chunk
    u = jax.nn.gelu(u_ref[...])
    v = jax.nn.gelu(v_ref[...])
    vn = _rms(v, C_WIDTH) * gn_ref[...]
    vn_bf = vn.astype(BF16)
    row = lax.broadcasted_iota(jnp.int32, (chunk, chunk), 0)
    col = lax.broadcasted_iota(jnp.int32, (chunk, chunk), 1)
    keep = col <= row
    for g in range(C_GROUPS):
        w = jnp.where(keep, ws_ref[g], 0.0).astype(BF16)
        bias = bs_ref[:, g:g + 1]
        cs = slice(g * C_GROUP_DIM, (g + 1) * C_GROUP_DIM)
        for c in range(n_chunks):
            rs = slice(c * chunk, (c + 1) * chunk)
            s = _dot(w, vn_bf[rs, cs]) + bias
            o_ref[rs, cs] = (u[rs, cs] * s).astype(o_ref.dtype)

    @pl.when(pl.program_id(1) == pl.num_programs(1) - 1)
    def _():
        vr_ref[...] = vn[tt - chunk:, :]


def gmlp(proj, v_norm, w_s, b_s, tt_cap=512):
    bsz, t, _ = proj.shape
    chunk = min(t, C_CHUNK)
    tt = _row_tile(t, tt_cap)
    ws = w_s[:, :chunk, :chunk]
    bs_t = b_s[:, :chunk].T
    kern = functools.partial(_gmlp_kernel, tt=tt, chunk=chunk)
    return pl.pallas_call(
        kern,
        grid=(bsz, t // tt),
        in_specs=[
            pl.BlockSpec((None, tt, C_WIDTH), lambda b, i: (b, i, 0)),
            pl.BlockSpec((None, tt, C_WIDTH), lambda b, i: (b, i, 1)),
            pl.BlockSpec((1, C_WIDTH), lambda b, i: (0, 0)),
            pl.BlockSpec((C_GROUPS, chunk, chunk), lambda b, i: (0, 0, 0)),
            pl.BlockSpec((chunk, C_GROUPS), lambda b, i: (0, 0)),
        ],
        out_specs=[
            pl.BlockSpec((None, tt, C_WIDTH), lambda b, i: (b, i, 0)),
            pl.BlockSpec((None, chunk, C_WIDTH), lambda b, i: (b, 0, 0)),
        ],
        out_shape=[
            jax.ShapeDtypeStruct((bsz, t, C_WIDTH), BF16),
            jax.ShapeDtypeStruct((bsz, chunk, C_WIDTH), F32),
        ],
        compiler_params=_cparams(("parallel", "arbitrary")),
        name="gmlp",
    )(proj, proj, v_norm.reshape(1, C_WIDTH), ws, bs_t)


def _rope_lanes(x, cos_t, sin_lo, sin_hi):
    half = ROPE_DIM // 2
    return (x * cos_t + pltpu.roll(x, LANES - half, axis=1) * sin_lo
            + pltpu.roll(x, half, axis=1) * sin_hi)


def _mla_q_kernel(cq_ref, ckv_ref, kpe_ref, qn_ref, kvn_ref, wuq_ref, qg_ref,
                  qc_ref, qs1_ref, qs2_ref, kc_ref, ks1_ref, ks2_ref,
                  q_ref, ckv_o_ref, kpe_o_ref):
    cqn = (_rms(cq_ref[...], Q_LORA) * qn_ref[...]).astype(BF16)
    q_all = _dot(cqn, wuq_ref[...])
    qc, qs1, qs2 = qc_ref[...], qs1_ref[...], qs2_ref[...]
    scale = QK_DIM ** -0.5
    for h in range(D_HEADS):
        sl = slice(h * HEAD_PAD, (h + 1) * HEAD_PAD)
        qh = _rope_lanes(q_all[:, sl], qc, qs1, qs2)
        qh = _rms(qh, QK_DIM) * (qg_ref[...] * scale)
        q_ref[:, sl] = qh.astype(q_ref.dtype)
    ckv_o_ref[...] = _rms(ckv_ref[...], KV_LORA) * kvn_ref[...]
    kpe = _rope_lanes(kpe_ref[...], kc_ref[...], ks1_ref[...], ks2_ref[...])
    kpe_o_ref[...] = kpe[:, :ROPE_DIM]


def mla_q(proj, tables, q_norm, kv_norm, wuq_p, q_gain_p, tm_cap=512):
    bsz, t, _ = proj.shape
    tm = _row_tile(t, tm_cap)
    tab = pl.BlockSpec((tm, LANES), lambda b, i: (i, 0))
    row = lambda w: pl.BlockSpec((1, w), lambda b, i: (0, 0))
    return pl.pallas_call(
        _mla_q_kernel,
        grid=(bsz, t // tm),
        in_specs=[
            pl.BlockSpec((None, tm, Q_LORA), lambda b, i: (b, i, 2 * C_WIDTH // Q_LORA)),
            pl.BlockSpec((None, tm, KV_LORA), lambda b, i: (b, i, (2 * C_WIDTH + Q_LORA) // KV_LORA)),
            pl.BlockSpec((None, tm, LANES), lambda b, i: (b, i, (2 * C_WIDTH + Q_LORA + KV_LORA) // LANES)),
            row(Q_LORA), row(KV_LORA),
            pl.BlockSpec((Q_LORA, D_HEADS * HEAD_PAD), lambda b, i: (0, 0)),
            row(HEAD_PAD),
            tab, tab, tab, tab, tab, tab,
        ],
        out_specs=[
            pl.BlockSpec((None, tm, D_HEADS * HEAD_PAD), lambda b, i: (b, i, 0)),
            pl.BlockSpec((None, tm, KV_LORA), lambda b, i: (b, i, 0)),
            pl.BlockSpec((None, tm, ROPE_DIM), lambda b, i: (b, i, 0)),
        ],
        out_shape=[
            jax.ShapeDtypeStruct((bsz, t, D_HEADS * HEAD_PAD), BF16),
            jax.ShapeDtypeStruct((bsz, t, KV_LORA), F32),
            jax.ShapeDtypeStruct((bsz, t, ROPE_DIM), F32),
        ],
        compiler_params=_cparams(("parallel", "parallel")),
        name="mla_q",
    )(proj, proj, proj, q_norm.reshape(1, Q_LORA), kv_norm.reshape(1, KV_LORA), wuq_p,
      q_gain_p, *tables)


def _mla_kv_kernel(ckv_ref, kpe_ref, wk_ref, wkpe_ref, wv_ref, kg_ref, k_ref, v_ref):
    ckv = ckv_ref[...].astype(BF16)
    k_all = _dot(ckv, wk_ref[...]) + _dot(kpe_ref[...].astype(BF16), wkpe_ref[...])
    for h in range(D_HEADS):
        sl = slice(h * HEAD_PAD, (h + 1) * HEAD_PAD)
        k_ref[:, sl] = (_rms(k_all[:, sl], QK_DIM) * kg_ref[...]).astype(k_ref.dtype)
    v_ref[...] = _dot(ckv, wv_ref[...]).astype(v_ref.dtype)


def mla_kv(ckv_all, kpe_all, wk_p, wkpe_p, wv_p, k_gain_p, tm_cap=512):
    bsz, s, _ = ckv_all.shape
    tm = s if s % tm_cap else tm_cap
    width = D_HEADS * HEAD_PAD
    full = lambda r, c: pl.BlockSpec((r, c), lambda b, i: (0, 0))
    return pl.pallas_call(
        _mla_kv_kernel,
        grid=(bsz, s // tm),
        in_specs=[
            pl.BlockSpec((None, tm, KV_LORA), lambda b, i: (b, i, 0)),
            pl.BlockSpec((None, tm, ROPE_DIM), lambda b, i: (b, i, 0)),
            full(KV_LORA, width), full(ROPE_DIM, width), full(KV_LORA, width), full(1, HEAD_PAD),
        ],
        out_specs=[
            pl.BlockSpec((None, tm, width), lambda b, i: (b, i, 0)),
            pl.BlockSpec((None, tm, width), lambda b, i: (b, i, 0)),
        ],
        out_shape=[
            jax.ShapeDtypeStruct((bsz, s, width), BF16),
            jax.ShapeDtypeStruct((bsz, s, width), BF16),
        ],
        compiler_params=_cparams(("parallel", "parallel")),
        name="mla_kv",
    )(ckv_all, kpe_all, wk_p, wkpe_p, wv_p, k_gain_p)


def _attn_kernel(q_ref, k_ref, v_ref, o_ref, m_ref, l_ref, acc_ref, *, tq, tk, s_len, q_pos0):
    qi = pl.program_id(1)
    q_first = q_pos0 + qi * tq
    q_last = q_first + tq - 1
    last_visible = jnp.minimum(s_len - 1, (q_last // CHUNK) * CHUNK + CHUNK - 1)
    n_kv = last_visible // tk + 1

    m_ref[...] = jnp.full_like(m_ref, NEG_BIG)
    l_ref[...] = jnp.zeros_like(l_ref)
    acc_ref[...] = jnp.zeros_like(acc_ref)

    q_chunk = _div_pow2(q_first + lax.broadcasted_iota(jnp.int32, (tq, tk), 0), CHUNK)
    k_iota = lax.broadcasted_iota(jnp.int32, (tq, tk), 1)

    def kv_step(j, carry):
        k0 = pl.multiple_of(j * tk, tk)
        visible = _div_pow2(k0 + k_iota, CHUNK) <= q_chunk
        for h in range(D_HEADS):
            sl = slice(h * HEAD_PAD, (h + 1) * HEAD_PAD)
            s = _dot_nt(q_ref[:, sl], k_ref[pl.ds(k0, tk), sl])
            s = jnp.where(visible, s, NEG_BIG)
            m_old = m_ref[h]
            m_new = jnp.maximum(m_old, jnp.max(s, axis=-1, keepdims=True))
            alpha = jnp.exp(m_old - m_new)
            p = jnp.exp(s - m_new)
            l_ref[h] = alpha * l_ref[h] + jnp.sum(p, axis=-1, keepdims=True)
            acc_ref[h] = alpha * acc_ref[h] + _dot(p.astype(BF16), v_ref[pl.ds(k0, tk), sl])
            m_ref[h] = m_new
        return carry

    lax.fori_loop(0, n_kv, kv_step, 0)
    for h in range(D_HEADS):
        sl = slice(h * HEAD_PAD, (h + 1) * HEAD_PAD)
        o_ref[:, sl] = (acc_ref[h] / l_ref[h]).astype(o_ref.dtype)


def attention(q, k, v, q_pos0, tq_cap=256, tk_cap=256):
    bsz, tq_len, width = q.shape
    s_len = k.shape[1]
    tq = _row_tile(tq_len, tq_cap)
    tk = tk_cap
    while s_len % tk:
        tk -= 2 * SUBLANES
    kern = functools.partial(_attn_kernel, tq=tq, tk=tk, s_len=s_len, q_pos0=q_pos0)
    return pl.pallas_call(
        kern,
        grid=(bsz, tq_len // tq),
        in_specs=[
            pl.BlockSpec((None, tq, width), lambda b, i: (b, i, 0)),
            pl.BlockSpec((None, s_len, width), lambda b, i: (b, 0, 0)),
            pl.BlockSpec((None, s_len, width), lambda b, i: (b, 0, 0)),
        ],
        out_specs=pl.BlockSpec((None, tq, width), lambda b, i: (b, i, 0)),
        out_shape=jax.ShapeDtypeStruct((bsz, tq_len, width), BF16),
        scratch_shapes=[pltpu.VMEM((D_HEADS, tq, 1), F32), pltpu.VMEM((D_HEADS, tq, 1), F32),
                        pltpu.VMEM((D_HEADS, tq, HEAD_PAD), F32)],
        compiler_params=_cparams(("parallel", "arbitrary")),
        name="attention",
    )(q, k, v)


def rope_tables(pos):
    half = ROPE_DIM // 2
    inv = ROPE_BASE ** (-jnp.arange(half, dtype=F32) / half)
    ang = pos.astype(F32)[:, None] * inv[None, :]
    cos, sin = jnp.cos(ang), jnp.sin(ang)
    n = pos.shape[0]
    z = lambda w: jnp.zeros((n, w), F32)
    o = lambda w: jnp.ones((n, w), F32)
    tail = LANES - NOPE_DIM - ROPE_DIM
    q_cos = jnp.concatenate([o(NOPE_DIM), cos, cos, z(tail)], axis=1)
    q_s1 = jnp.concatenate([z(NOPE_DIM), -sin, z(half), z(tail)], axis=1)
    q_s2 = jnp.concatenate([z(NOPE_DIM), z(half), sin, z(tail)], axis=1)
    k_cos = jnp.concatenate([cos, cos, z(LANES - ROPE_DIM)], axis=1)
    k_s1 = jnp.concatenate([-sin, z(LANES - half)], axis=1)
    k_s2 = jnp.concatenate([z(half), sin, z(LANES - ROPE_DIM)], axis=1)
    return q_cos, q_s1, q_s2, k_cos, k_s1, k_s2


def pack_mla_weights(w_uq, w_ukv, q_gain, k_gain, cd_w_out):
    pad_q = HEAD_PAD - QK_DIM
    wuq_p = jnp.pad(w_uq.reshape(Q_LORA, D_HEADS, QK_DIM), ((0, 0), (0, 0), (0, pad_q)))
    wuq_p = wuq_p.reshape(Q_LORA, D_HEADS * HEAD_PAD).astype(BF16)
    wkv = w_ukv.reshape(KV_LORA, D_HEADS, NOPE_DIM + V_DIM)
    wk_p = jnp.pad(wkv[:, :, :NOPE_DIM], ((0, 0), (0, 0), (0, HEAD_PAD - NOPE_DIM)))
    wk_p = wk_p.reshape(KV_LORA, D_HEADS * HEAD_PAD).astype(BF16)
    wv_p = jnp.pad(wkv[:, :, NOPE_DIM:], ((0, 0), (0, 0), (0, HEAD_PAD - V_DIM)))
    wv_p = wv_p.reshape(KV_LORA, D_HEADS * HEAD_PAD).astype(BF16)
    place = jnp.pad(jnp.eye(ROPE_DIM, dtype=F32), ((0, 0), (NOPE_DIM, HEAD_PAD - QK_DIM)))
    wkpe_p = jnp.tile(place, (1, D_HEADS)).astype(BF16)
    q_gain_p = jnp.pad(q_gain, (0, pad_q)).reshape(1, HEAD_PAD)
    k_gain_p = jnp.pad(k_gain, (0, pad_q)).reshape(1, HEAD_PAD)
    w_c = cd_w_out[:C_WIDTH].astype(BF16)
    w_d = cd_w_out[C_WIDTH:].reshape(D_HEADS, V_DIM, D_MODEL)
    w_d_p = jnp.pad(w_d, ((0, 0), (0, HEAD_PAD - V_DIM), (0, 0))).reshape(D_HEADS * HEAD_PAD, D_MODEL)
    return wuq_p, wk_p, wkpe_p, wv_p, q_gain_p, k_gain_p, w_c, w_d_p.astype(BF16)


def _trunk(x, pos0, w, past):
    bsz, t, _ = x.shape
    n = bsz * t
    x2 = x.reshape(n, D_MODEL)
    pos = pos0 + jnp.arange(t, dtype=jnp.int32)
    lb_all = jnp.cumsum(jax.nn.softmax(w['hgrn_lb_logits'], axis=0), axis=0)
    outs = {}
    for l in range(2):
        x2 = ffn(x2, w['ffn1_norm'][l], w['ffn1_w_gate'][l], w['ffn1_w_up'][l], w['ffn1_w_down'][l])
        if l == 0:
            proj = norm_matmul(x2, w['mix_norm'][l], w['ab_w_in']).reshape(bsz, t, AB_IN)
            s0 = None if past is None else past[0][0]
            a_out, s_t = hgrn2(proj, lb_all[0], w['hgrn_out_norm'][0], s0)
            if past is None:
                x0 = jnp.zeros((bsz, S5_STATE_W), F32)
            else:
                x0 = s5_pack_state(past[1][0], past[2][0])
            u_tb = jnp.transpose(proj[:, :, AB_IN - B_WIDTH:], (1, 0, 2))
            b_tb, x_t = s5_mix(u_tb, w['s5_wb'], w['s5_wc'], w['s5_lam'], w['s5_d'],
                               w['s5_w_glu'], w['s5_b_glu'], x0)
            b_out = jnp.transpose(b_tb, (1, 0, 2))
            x2 = proj_res(x2, a_out.reshape(n, A_WIDTH), b_out.reshape(n, B_WIDTH),
                          w['ab_w_out_a'], w['ab_w_out_b'])
            outs['hgrn'] = s_t
            outs['s5re'], outs['s5im'] = s5_unpack_state(x_t)
        else:
            proj = norm_matmul(x2, w['mix_norm'][l], w['cd_w_in_p']).reshape(bsz, t, CD_IN_PAD)
            c_out, v_rows = gmlp(proj, w['gmlp_v_norm'], w['gmlp_w_s'], w['gmlp_b_s'])
            q, ckv, kpe = mla_q(proj, rope_tables(pos), w['mla_q_norm'], w['mla_kv_norm'],
                                w['mla_wuq_p'], w['mla_q_gain_p'])
            if past is None:
                ckv_all, kpe_all = ckv, kpe
            else:
                ckv_all = jnp.concatenate([past[3][0], ckv], axis=1)
                kpe_all = jnp.concatenate([past[4][0], kpe], axis=1)
            k, v = mla_kv(ckv_all, kpe_all, w['mla_wk_p'], w['mla_wkpe_p'], w['mla_wv_p'],
                          w['mla_k_gain_p'])
            d_out = attention(q, k, v, pos0)
            x2 = proj_res(x2, c_out.reshape(n, C_WIDTH), d_out.reshape(n, D_HEADS * HEAD_PAD),
                          w['cd_w_out_c'], w['cd_w_out_d'])
            outs['gv'], outs['ckv'], outs['kpe'] = v_rows, ckv, kpe
        x2 = ffn(x2, w['ffn2_norm'][l], w['ffn2_w_gate'][l], w['ffn2_w_up'][l], w['ffn2_w_down'][l])
    return x2.reshape(bsz, t, D_MODEL), outs


def kernel(x_prompt, x_sample, state_hgrn, state_s5_re, state_s5_im, cache_mla_ckv, cache_mla_kpe, ffn1_norm, ffn1_w_gate, ffn1_w_up, ffn1_w_down, mix_norm, ffn2_norm, ffn2_w_gate, ffn2_w_up, ffn2_w_down, ab_w_in, hgrn_lb_logits, hgrn_out_norm, s5_lambda_re, s5_lambda_im, s5_log_dt, s5_b_re, s5_b_im, s5_c_re, s5_c_im, s5_d, s5_w_glu, s5_b_glu, ab_w_out, cd_w_in, gmlp_v_norm, gmlp_w_s, gmlp_b_s, mla_q_norm, mla_w_uq, mla_kv_norm, mla_w_ukv, mla_q_gain, mla_k_gain, cd_w_out):
    bf = lambda a: a.astype(BF16)
    wb, wc, lam_rows = s5_prepare(s5_lambda_re[0], s5_lambda_im[0], s5_log_dt[0],
                                  s5_b_re[0], s5_b_im[0], s5_c_re[0], s5_c_im[0])
    wuq_p, wk_p, wkpe_p, wv_p, q_gain_p, k_gain_p, w_c, w_d_p = pack_mla_weights(
        mla_w_uq[0], mla_w_ukv[0], mla_q_gain[0], mla_k_gain[0], cd_w_out[0])
    w = dict(
        ffn1_norm=ffn1_norm, ffn1_w_gate=bf(ffn1_w_gate), ffn1_w_up=bf(ffn1_w_up), ffn1_w_down=bf(ffn1_w_down),
        ffn2_norm=ffn2_norm, ffn2_w_gate=bf(ffn2_w_gate), ffn2_w_up=bf(ffn2_w_up), ffn2_w_down=bf(ffn2_w_down),
        mix_norm=mix_norm, ab_w_in=bf(ab_w_in[0]), hgrn_lb_logits=hgrn_lb_logits, hgrn_out_norm=hgrn_out_norm,
        s5_wb=wb, s5_wc=wc, s5_lam=lam_rows, s5_d=s5_d[0].reshape(B_WIDTH), s5_w_glu=bf(s5_w_glu[0]),
        s5_b_glu=s5_b_glu[0], ab_w_out_a=bf(ab_w_out[0, :A_WIDTH]), ab_w_out_b=bf(ab_w_out[0, A_WIDTH:]),
        cd_w_in_p=bf(jnp.pad(cd_w_in[0], ((0, 0), (0, CD_IN_PAD - CD_IN)))),
        gmlp_v_norm=gmlp_v_norm[0], gmlp_w_s=gmlp_w_s[0], gmlp_b_s=gmlp_b_s[0],
        mla_q_norm=mla_q_norm[0], mla_kv_norm=mla_kv_norm[0], mla_wuq_p=wuq_p, mla_q_gain_p=q_gain_p,
        mla_wk_p=wk_p, mla_wkpe_p=wkpe_p, mla_wv_p=wv_p, mla_k_gain_p=k_gain_p,
        cd_w_out_c=w_c, cd_w_out_d=w_d_p,
    )
    past_len = cache_mla_ckv.shape[2]
    y_p, o_p = _trunk(x_prompt, 0, w, None)
    y_s, o_s = _trunk(x_sample, past_len, w,
                      (state_hgrn, state_s5_re, state_s5_im, cache_mla_ckv, cache_mla_kpe))
    e = lambda a: a[None]
    return (y_p, y_s, e(o_p['hgrn']), e(o_s['hgrn']), e(o_p['s5re']), e(o_p['s5im']),
            e(o_s['s5re']), e(o_s['s5im']), e(o_p['gv']), e(o_s['gv']),
            e(o_p['ckv']), e(o_p['kpe']), e(o_s['ckv']), e(o_s['kpe']))
```

```python
import functools
import math

import jax
import jax.numpy as jnp
from jax import lax
from jax.experimental import pallas as pl
from jax.experimental.pallas import tpu as pltpu

D_MODEL = 1024
D_FF = 2816
EPS = 1e-6
A_HEADS = 4
A_KEY = 128
A_VAL = 128
A_WIDTH = A_HEADS * A_VAL
HGRN_BLOCK = 64
B_GROUP_DIM = 16
B_WIDTH = 512
B_GROUPS = 32
B_STATE = 64
LAMBDA_RE_CEIL = -1e-4
C_CHUNK = 128
C_WIDTH = 512
C_GROUPS = 4
C_GROUP_DIM = 128
D_HEADS = 8
Q_LORA = 256
KV_LORA = 128
NOPE_DIM = 64
ROPE_DIM = 32
V_DIM = 64
QK_DIM = NOPE_DIM + ROPE_DIM
ROPE_BASE = 10000.0
CHUNK = 64
A_QK = A_HEADS * A_KEY
AB_IN = 2 * A_QK + 2 * A_WIDTH + B_WIDTH
CD_IN = 2 * C_WIDTH + Q_LORA + KV_LORA + ROPE_DIM
CD_IN_PAD = 1536

LANES = 128
SUBLANES = 8
VMEM_LIMIT = 56 * 1024 * 1024
HGRN_SUB = 16
HEAD_PAD = LANES
NEG_BIG = -1e30

BF16 = jnp.bfloat16
F32 = jnp.float32


def _cparams(sem):
    return pltpu.CompilerParams(dimension_semantics=sem, vmem_limit_bytes=VMEM_LIMIT)


def _rms(xf, width):
    return xf * lax.rsqrt(jnp.sum(xf * xf, axis=-1, keepdims=True) * (1.0 / width) + EPS)


def _dot(a, b):
    return jnp.dot(a, b, preferred_element_type=F32)


def _dot_nt(a, b):
    return lax.dot_general(a, b, (((1,), (1,)), ((), ())), preferred_element_type=F32)


def _dot_tn(a, b):
    return lax.dot_general(a, b, (((0,), (0,)), ((), ())), preferred_element_type=F32)


def _div_pow2(x, d):
    assert d & (d - 1) == 0
    return lax.shift_right_logical(x, jnp.int32(d.bit_length() - 1))


def _mod_pow2(x, d):
    assert d & (d - 1) == 0
    return x & jnp.int32(d - 1)


def _row_tile(n, cap):
    t = min(n, cap)
    assert n % t == 0, (n, t)
    return t


def _ffn_kernel(x_ref, g_ref, wg_ref, wu_ref, wd_ref, o_ref, h_ref, acc_ref):
    j = pl.program_id(1)

    @pl.when(j == 0)
    def _():
        xf = x_ref[...]
        h_ref[...] = (_rms(xf, D_MODEL) * g_ref[...]).astype(BF16)
        acc_ref[...] = jnp.zeros_like(acc_ref)

    h = h_ref[...]
    gate = _dot(h, wg_ref[...])
    up = _dot(h, wu_ref[...])
    act = (gate * jax.nn.sigmoid(gate) * up).astype(BF16)
    acc_ref[...] += _dot(act, wd_ref[...])

    @pl.when(j == pl.num_programs(1) - 1)
    def _():
        o_ref[...] = x_ref[...] + 0.5 * acc_ref[...]


def ffn(x, g, wg, wu, wd, tm_cap=1024, tf=256):
    n = x.shape[0]
    tm = _row_tile(n, tm_cap)
    return pl.pallas_call(
        _ffn_kernel,
        grid=(n // tm, D_FF // tf),
        in_specs=[
            pl.BlockSpec((tm, D_MODEL), lambda i, j: (i, 0)),
            pl.BlockSpec((1, D_MODEL), lambda i, j: (0, 0)),
            pl.BlockSpec((D_MODEL, tf), lambda i, j: (0, j)),
            pl.BlockSpec((D_MODEL, tf), lambda i, j: (0, j)),
            pl.BlockSpec((tf, D_MODEL), lambda i, j: (j, 0)),
        ],
        out_specs=pl.BlockSpec((tm, D_MODEL), lambda i, j: (i, 0)),
        out_shape=jax.ShapeDtypeStruct((n, D_MODEL), F32),
        scratch_shapes=[pltpu.VMEM((tm, D_MODEL), BF16), pltpu.VMEM((tm, D_MODEL), F32)],
        compiler_params=_cparams(("parallel", "arbitrary")),
        name="ffn",
    )(x, g.reshape(1, D_MODEL), wg, wu, wd)


def _norm_matmul_kernel(x_ref, g_ref, w_ref, o_ref, h_ref):
    @pl.when(pl.program_id(1) == 0)
    def _():
        h_ref[...] = (_rms(x_ref[...], D_MODEL) * g_ref[...]).astype(BF16)

    o_ref[...] = _dot(h_ref[...], w_ref[...])


def norm_matmul(x, g, w, tm_cap=1024, tn=512):
    n = x.shape[0]
    m = w.shape[1]
    tm = _row_tile(n, tm_cap)
    return pl.pallas_call(
        _norm_matmul_kernel,
        grid=(n // tm, m // tn),
        in_specs=[
            pl.BlockSpec((tm, D_MODEL), lambda i, j: (i, 0)),
            pl.BlockSpec((1, D_MODEL), lambda i, j: (0, 0)),
            pl.BlockSpec((D_MODEL, tn), lambda i, j: (0, j)),
        ],
        out_specs=pl.BlockSpec((tm, tn), lambda i, j: (i, j)),
        out_shape=jax.ShapeDtypeStruct((n, m), F32),
        scratch_shapes=[pltpu.VMEM((tm, D_MODEL), BF16)],
        compiler_params=_cparams(("parallel", "arbitrary")),
        name="norm_matmul",
    )(x, g.reshape(1, D_MODEL), w)


def _proj_res_kernel(x_ref, a1_ref, a2_ref, w1_ref, w2_ref, o_ref):
    o_ref[...] = x_ref[...] + _dot(a1_ref[...], w1_ref[...]) + _dot(a2_ref[...], w2_ref[...])


def proj_res(x, a1, a2, w1, w2, tm_cap=1024):
    n = x.shape[0]
    tm = _row_tile(n, tm_cap)
    k1, k2 = a1.shape[1], a2.shape[1]
    return pl.pallas_call(
        _proj_res_kernel,
        grid=(n // tm,),
        in_specs=[
            pl.BlockSpec((tm, D_MODEL), lambda i: (i, 0)),
            pl.BlockSpec((tm, k1), lambda i: (i, 0)),
            pl.BlockSpec((tm, k2), lambda i: (i, 0)),
            pl.BlockSpec((k1, D_MODEL), lambda i: (0, 0)),
            pl.BlockSpec((k2, D_MODEL), lambda i: (0, 0)),
        ],
        out_specs=pl.BlockSpec((tm, D_MODEL), lambda i: (i, 0)),
        out_shape=jax.ShapeDtypeStruct((n, D_MODEL), F32),
        compiler_params=_cparams(("parallel",)),
        name="proj_res",
    )(x, a1, a2, w1, w2)


def _hgrn_kernel(q_ref, f_ref, i_ref, g_ref, lb_ref, og_ref, s0_ref, y_ref, st_ref, state_ref,
                 *, tt, blk, has_past):
    t_idx = pl.program_id(1)
    n_chunks = tt // blk
    n_sub = blk // HGRN_SUB

    @pl.when(t_idx == 0)
    def _():
        if has_past:
            for h in range(A_HEADS):
                state_ref[h] = s0_ref[h].T
        else:
            state_ref[...] = jnp.zeros_like(state_ref)

    row = lax.broadcasted_iota(jnp.int32, (tt, tt), 0)
    col = lax.broadcasted_iota(jnp.int32, (tt, tt), 1)
    same_chunk = _div_pow2(row, blk) == _div_pow2(col, blk)
    causal = same_chunk & (col <= row)
    tril = jnp.where(causal, 1.0, 0.0).astype(BF16)
    r1 = lax.broadcasted_iota(jnp.int32, (tt, LANES), 0)
    sub_of_row = _div_pow2(_mod_pow2(r1, blk), HGRN_SUB)

    for h in range(A_HEADS):
        sl = slice(h * LANES, (h + 1) * LANES)
        q = q_ref[:, sl]
        fz = f_ref[:, sl]
        v = i_ref[:, sl]
        gz = g_ref[:, sl]
        lb = lb_ref[:, sl]
        f = lb + (1.0 - lb) * jax.nn.sigmoid(fz)
        k = 1.0 - f
        logf = jnp.log(f)
        hi = logf.astype(BF16)
        mid = (logf - hi.astype(F32)).astype(BF16)
        b = _dot(tril, hi) + _dot(tril, mid)

        refs = []
        for i in range(n_sub):
            parts = []
            for c in range(n_chunks):
                r = c * blk + i * HGRN_SUB + HGRN_SUB // 2 - 1
                parts.append(jnp.broadcast_to(b[r:r + 1, :], (blk, LANES)))
            refs.append(parts[0] if n_chunks == 1 else jnp.concatenate(parts, axis=0))
        ends = []
        for c in range(n_chunks):
            r = c * blk + blk - 1
            ends.append(b[r:r + 1, :])
        b_end_rows = (jnp.broadcast_to(ends[0], (blk, LANES)) if n_chunks == 1 else
                      jnp.concatenate([jnp.broadcast_to(e, (blk, LANES)) for e in ends], axis=0))

        own_ref = refs[n_sub - 1]
        for i in range(n_sub - 2, -1, -1):
            own_ref = jnp.where(sub_of_row == i, refs[i], own_ref)
        q_t = q * jnp.exp(b - own_ref)
        q_hat = jnp.concatenate(
            [jnp.where(sub_of_row == i, q_t, 0.0).astype(BF16) for i in range(n_sub)], axis=1)
        k_hat = jnp.concatenate(
            [(k * jnp.exp(jnp.where(sub_of_row <= i, refs[i] - b, NEG_BIG))).astype(BF16)
             for i in range(n_sub)], axis=1)
        att = _dot_nt(q_hat, k_hat)
        att = jnp.where(causal, att, 0.0).astype(BF16)
        v_bf = v.astype(BF16)
        o = _dot(att, v_bf)

        q_e = (q * jnp.exp(b)).astype(BF16)
        k_d = (k * jnp.exp(b_end_rows - b)).astype(BF16)
        s_t = state_ref[h]
        o_parts = []
        for c in range(n_chunks):
            rs = slice(c * blk, (c + 1) * blk)
            o_parts.append(o[rs] + _dot_nt(q_e[rs], s_t.astype(BF16)))
            s_t = s_t * jnp.exp(ends[c]) + _dot_tn(v_bf[rs], k_d[rs])
        state_ref[h] = s_t
        o = o_parts[0] if n_chunks == 1 else jnp.concatenate(o_parts, axis=0)

        y = _rms(o, A_VAL) * og_ref[...] * jax.nn.sigmoid(gz)
        y_ref[:, sl] = y.astype(y_ref.dtype)

    @pl.when(t_idx == pl.num_programs(1) - 1)
    def _():
        for h in range(A_HEADS):
            st_ref[h] = state_ref[h].T


def hgrn2(proj, lb, out_g, s0, tt_cap=256):
    bsz, t, _ = proj.shape
    blk = min(HGRN_BLOCK, t)
    tt = _row_tile(t, tt_cap)
    has_past = s0 is not None
    if s0 is None:
        s0 = jnp.zeros((bsz, A_HEADS, A_KEY, A_VAL), F32)
    seg = lambda s: pl.BlockSpec((None, tt, A_WIDTH), lambda b, i, s=s: (b, i, s))
    kern = functools.partial(_hgrn_kernel, tt=tt, blk=blk, has_past=has_past)
    return pl.pallas_call(
        kern,
        grid=(bsz, t // tt),
        in_specs=[
            seg(0), seg(1), seg(2), seg(3),
            pl.BlockSpec((1, A_QK), lambda b, i: (0, 0)),
            pl.BlockSpec((1, A_VAL), lambda b, i: (0, 0)),
            pl.BlockSpec((None, A_HEADS, A_KEY, A_VAL), lambda b, i: (b, 0, 0, 0)),
        ],
        out_specs=[
            pl.BlockSpec((None, tt, A_WIDTH), lambda b, i: (b, i, 0)),
            pl.BlockSpec((None, A_HEADS, A_KEY, A_VAL), lambda b, i: (b, 0, 0, 0)),
        ],
        out_shape=[
            jax.ShapeDtypeStruct((bsz, t, A_WIDTH), BF16),
            jax.ShapeDtypeStruct((bsz, A_HEADS, A_KEY, A_VAL), F32),
        ],
        scratch_shapes=[pltpu.VMEM((A_HEADS, A_VAL, A_KEY), F32)],
        compiler_params=_cparams(("parallel", "arbitrary")),
        name="hgrn2",
    )(proj, proj, proj, proj, lb.reshape(1, A_QK), out_g.reshape(1, A_VAL), s0)


S5_SUPER = 4
S5_SUPER_U = B_WIDTH // S5_SUPER
S5_SUPER_X = 2 * (B_GROUPS // S5_SUPER) * B_STATE
S5_HALF = S5_SUPER_X // 2
S5_STATE_W = S5_SUPER * S5_SUPER_X


def _s5_kernel(u_ref, wb_ref, wc_ref, lam_ref, d_ref, wglu_ref, bglu_ref, x0_ref,
               o_ref, xt_ref, xbuf_ref, state_ref, *, tt):
    t_idx = pl.program_id(1)
    rows = tt * SUBLANES

    @pl.when(t_idx == 0)
    def _():
        state_ref[...] = x0_ref[...]

    u = u_ref[...].reshape(rows, B_WIDTH)
    u_bf = u.astype(BF16)
    for j in range(S5_SUPER):
        xbuf_ref[:, j * S5_SUPER_X:(j + 1) * S5_SUPER_X] = _dot(
            u_bf[:, j * S5_SUPER_U:(j + 1) * S5_SUPER_U], wb_ref[j])

    lam_re = jnp.broadcast_to(lam_ref[0:1, :], (SUBLANES, S5_STATE_W))
    lam_im = jnp.broadcast_to(lam_ref[1:2, :], (SUBLANES, S5_STATE_W))

    def step(t, x):
        r0 = pl.multiple_of(t * SUBLANES, SUBLANES)
        bu = xbuf_ref[pl.ds(r0, SUBLANES), :]
        pieces = []
        for j in range(S5_SUPER):
            c0 = j * S5_SUPER_X
            xr = x[:, c0:c0 + S5_HALF]
            xi = x[:, c0 + S5_HALF:c0 + S5_SUPER_X]
            lr = lam_re[:, c0:c0 + S5_HALF]
            li = lam_im[:, c0:c0 + S5_HALF]
            pieces.append(lr * xr - li * xi + bu[:, c0:c0 + S5_HALF])
            pieces.append(lr * xi + li * xr + bu[:, c0 + S5_HALF:c0 + S5_SUPER_X])
        x_new = jnp.concatenate(pieces, axis=1)
        xbuf_ref[pl.ds(r0, SUBLANES), :] = x_new
        return x_new

    x_last = lax.fori_loop(0, tt, step, state_ref[...])
    state_ref[...] = x_last

    ys = []
    for j in range(S5_SUPER):
        xs = xbuf_ref[:, j * S5_SUPER_X:(j + 1) * S5_SUPER_X].astype(BF16)
        ys.append(_dot(xs, wc_ref[j]))
    y = jnp.concatenate(ys, axis=1) + d_ref[...] * u
    z = jax.nn.gelu(y)
    gate = jax.nn.sigmoid(_dot(z.astype(BF16), wglu_ref[...]) + bglu_ref[...])
    o_ref[...] = (z * gate).astype(o_ref.dtype).reshape(tt, SUBLANES, B_WIDTH)

    @pl.when(t_idx == pl.num_programs(1) - 1)
    def _():
        xt_ref[...] = x_last


def s5_mix(u_tb, wb, wc, lam, d, w_glu, b_glu, x0, tt_cap=64):
    t, bsz, _ = u_tb.shape
    tt = _row_tile(t, tt_cap)
    kern = functools.partial(_s5_kernel, tt=tt)
    return pl.pallas_call(
        kern,
        grid=(bsz // SUBLANES, t // tt),
        in_specs=[
            pl.BlockSpec((tt, SUBLANES, B_WIDTH), lambda b, i: (i, b, 0)),
            pl.BlockSpec((S5_SUPER, S5_SUPER_U, S5_SUPER_X), lambda b, i: (0, 0, 0)),
            pl.BlockSpec((S5_SUPER, S5_SUPER_X, S5_SUPER_U), lambda b, i: (0, 0, 0)),
            pl.BlockSpec((2, S5_STATE_W), lambda b, i: (0, 0)),
            pl.BlockSpec((1, B_WIDTH), lambda b, i: (0, 0)),
            pl.BlockSpec((B_WIDTH, B_WIDTH), lambda b, i: (0, 0)),
            pl.BlockSpec((1, B_WIDTH), lambda b, i: (0, 0)),
            pl.BlockSpec((SUBLANES, S5_STATE_W), lambda b, i: (b, 0)),
        ],
        out_specs=[
            pl.BlockSpec((tt, SUBLANES, B_WIDTH), lambda b, i: (i, b, 0)),
            pl.BlockSpec((SUBLANES, S5_STATE_W), lambda b, i: (b, 0)),
        ],
        out_shape=[
            jax.ShapeDtypeStruct((t, bsz, B_WIDTH), BF16),
            jax.ShapeDtypeStruct((bsz, S5_STATE_W), F32),
        ],
        scratch_shapes=[pltpu.VMEM((tt * SUBLANES, S5_STATE_W), F32),
                        pltpu.VMEM((SUBLANES, S5_STATE_W), F32)],
        compiler_params=_cparams(("parallel", "arbitrary")),
        name="s5_mix",
    )(u_tb, wb, wc, lam, d.reshape(1, B_WIDTH), w_glu, b_glu.reshape(1, B_WIDTH), x0)


def _s5_pack_cols(a):
    return a.reshape(a.shape[:-2] + (S5_SUPER, S5_HALF))


def s5_prepare(lam_re, lam_im, log_dt, b_re, b_im, c_re, c_im):
    lr = jnp.minimum(lam_re, LAMBDA_RE_CEIL)
    li = lam_im
    dt = jnp.exp(log_dt)[:, None]
    mag = jnp.exp(lr * dt)
    bar_re = mag * jnp.cos(li * dt)
    bar_im = mag * jnp.sin(li * dt)
    den = lr * lr + li * li
    coef_re = (((bar_re - 1.0) * lr + bar_im * li) / den)[:, :, None]
    coef_im = ((bar_im * lr - (bar_re - 1.0) * li) / den)[:, :, None]
    bb_re = coef_re * b_re - coef_im * b_im
    bb_im = coef_re * b_im + coef_im * b_re
    gps = B_GROUPS // S5_SUPER
    eye = jnp.eye(gps, dtype=F32)

    def in_block(bm):
        bm = bm.reshape(S5_SUPER, gps, B_STATE, B_GROUP_DIM)
        blk = jnp.einsum('jgph,gk->jghkp', bm, eye)
        return blk.reshape(S5_SUPER, gps * B_GROUP_DIM, gps * B_STATE)

    def out_block(cm):
        cm = cm.reshape(S5_SUPER, gps, B_GROUP_DIM, B_STATE)
        blk = jnp.einsum('jghp,gk->jkpgh', cm, eye)
        return blk.reshape(S5_SUPER, gps * B_STATE, gps * B_GROUP_DIM)

    wb = jnp.concatenate([in_block(bb_re), in_block(bb_im)], axis=2).astype(BF16)
    wc = jnp.concatenate([out_block(c_re), out_block(-c_im)], axis=1).astype(BF16)
    lam_rows = jnp.stack([
        jnp.concatenate([_s5_pack_cols(bar_re)] * 2, axis=-1).reshape(S5_STATE_W),
        jnp.concatenate([_s5_pack_cols(bar_im)] * 2, axis=-1).reshape(S5_STATE_W)])
    return wb, wc, lam_rows


def s5_pack_state(x_re, x_im):
    bsz = x_re.shape[0]
    return jnp.concatenate([_s5_pack_cols(x_re), _s5_pack_cols(x_im)], axis=-1).reshape(bsz, S5_STATE_W)


def s5_unpack_state(x):
    bsz = x.shape[0]
    x = x.reshape(bsz, S5_SUPER, 2, S5_HALF)
    return (x[:, :, 0].reshape(bsz, B_GROUPS, B_STATE), x[:, :, 1].reshape(bsz, B_GROUPS, B_STATE))


def _gmlp_kernel(u_ref, v_ref, gn_ref, ws_ref, bs_ref, o_ref, vr_ref, *, tt, chunk):
    n_chunks = tt // chunk
    u = jax.nn.gelu(u_ref[...])
    v = jax.nn.gelu(v_ref[...])
    vn = _rms(v, C_WIDTH) * gn_ref[...]
    vn_bf = vn.astype(BF16)
    row = lax.broadcasted_iota(jnp.int32, (chunk, chunk), 0)
    col = lax.broadcasted_iota(jnp.int32, (chunk, chunk), 1)
    keep = col <= row
    for g in range(C_GROUPS):
        w = jnp.where(keep, ws_ref[g], 0.0).astype(BF16)
        bias = bs_ref[:, g:g + 1]
        cs = slice(g * C_GROUP_DIM, (g + 1) * C_GROUP_DIM)
        for c in range(n_chunks):
            rs = slice(c * chunk, (c + 1) * chunk)
            s = _dot(w, vn_bf[rs, cs]) + bias
            o_ref[rs, cs] = (u[rs, cs] * s).astype(o_ref.dtype)

    @pl.when(pl.program_id(1) == pl.num_programs(1) - 1)
    def _():
        vr_ref[...] = vn[tt - chunk:, :]


def gmlp(proj, v_norm, w_s, b_s, tt_cap=512):
    bsz, t, _ = proj.shape
    chunk = min(t, C_CHUNK)
    tt = _row_tile(t, tt_cap)
    ws = w_s[:, :chunk, :chunk]
    bs_t = b_s[:, :chunk].T
    kern = functools.partial(_gmlp_kernel, tt=tt, chunk=chunk)
    return pl.pallas_call(
        kern,
        grid=(bsz, t // tt),
        in_specs=[
            pl.BlockSpec((None, tt, C_WIDTH), lambda b, i: (b, i, 0)),
            pl.BlockSpec((None, tt, C_WIDTH), lambda b, i: (b, i, 1)),
            pl.BlockSpec((1, C_WIDTH), lambda b, i: (0, 0)),
            pl.BlockSpec((C_GROUPS, chunk, chunk), lambda b, i: (0, 0, 0)),
            pl.BlockSpec((chunk, C_GROUPS), lambda b, i: (0, 0)),
        ],
        out_specs=[
            pl.BlockSpec((None, tt, C_WIDTH), lambda b, i: (b, i, 0)),
            pl.BlockSpec((None, chunk, C_WIDTH), lambda b, i: (b, 0, 0)),
        ],
        out_shape=[
            jax.ShapeDtypeStruct((bsz, t, C_WIDTH), BF16),
            jax.ShapeDtypeStruct((bsz, chunk, C_WIDTH), F32),
        ],
        compiler_params=_cparams(("parallel", "arbitrary")),
        name="gmlp",
    )(proj, proj, v_norm.reshape(1, C_WIDTH), ws, bs_t)


def _rope_lanes(x, cos_t, sin_lo, sin_hi):
    half = ROPE_DIM // 2
    return (x * cos_t + pltpu.roll(x, LANES - half, axis=1) * sin_lo
            + pltpu.roll(x, half, axis=1) * sin_hi)


def _mla_q_kernel(cq_ref, ckv_ref, kpe_ref, qn_ref, kvn_ref, wuq_ref, qg_ref,
                  qc_ref, qs1_ref, qs2_ref, kc_ref, ks1_ref, ks2_ref,
                  q_ref, ckv_o_ref, kpe_o_ref):
    cqn = (_rms(cq_ref[...], Q_LORA) * qn_ref[...]).astype(BF16)
    q_all = _dot(cqn, wuq_ref[...])
    qc, qs1, qs2 = qc_ref[...], qs1_ref[...], qs2_ref[...]
    scale = QK_DIM ** -0.5
    for h in range(D_HEADS):
        sl = slice(h * HEAD_PAD, (h + 1) * HEAD_PAD)
        qh = _rope_lanes(q_all[:, sl], qc, qs1, qs2)
        qh = _rms(qh, QK_DIM) * (qg_ref[...] * scale)
        q_ref[:, sl] = qh.astype(q_ref.dtype)
    ckv_o_ref[...] = _rms(ckv_ref[...], KV_LORA) * kvn_ref[...]
    kpe = _rope_lanes(kpe_ref[...], kc_ref[...], ks1_ref[...], ks2_ref[...])
    kpe_o_ref[...] = kpe[:, :ROPE_DIM]


def mla_q(proj, tables, q_norm, kv_norm, wuq_p, q_gain_p, tm_cap=512):
    bsz, t, _ = proj.shape
    tm = _row_tile(t, tm_cap)
    tab = pl.BlockSpec((tm, LANES), lambda b, i: (i, 0))
    row = lambda w: pl.BlockSpec((1, w), lambda b, i: (0, 0))
    return pl.pallas_call(
        _mla_q_kernel,
        grid=(bsz, t // tm),
        in_specs=[
            pl.BlockSpec((None, tm, Q_LORA), lambda b, i: (b, i, 2 * C_WIDTH // Q_LORA)),
            pl.BlockSpec((None, tm, KV_LORA), lambda b, i: (b, i, (2 * C_WIDTH + Q_LORA) // KV_LORA)),
            pl.BlockSpec((None, tm, LANES), lambda b, i: (b, i, (2 * C_WIDTH + Q_LORA + KV_LORA) // LANES)),
            row(Q_LORA), row(KV_LORA),
            pl.BlockSpec((Q_LORA, D_HEADS * HEAD_PAD), lambda b, i: (0, 0)),
            row(HEAD_PAD),
            tab, tab, tab, tab, tab, tab,
        ],
        out_specs=[
            pl.BlockSpec((None, tm, D_HEADS * HEAD_PAD), lambda b, i: (b, i, 0)),
            pl.BlockSpec((None, tm, KV_LORA), lambda b, i: (b, i, 0)),
            pl.BlockSpec((None, tm, ROPE_DIM), lambda b, i: (b, i, 0)),
        ],
        out_shape=[
            jax.ShapeDtypeStruct((bsz, t, D_HEADS * HEAD_PAD), BF16),
            jax.ShapeDtypeStruct((bsz, t, KV_LORA), F32),
            jax.ShapeDtypeStruct((bsz, t, ROPE_DIM), F32),
        ],
        compiler_params=_cparams(("parallel", "parallel")),
        name="mla_q",
    )(proj, proj, proj, q_norm.reshape(1, Q_LORA), kv_norm.reshape(1, KV_LORA), wuq_p,
      q_gain_p, *tables)


def _mla_kv_kernel(ckv_ref, kpe_ref, wk_ref, wkpe_ref, wv_ref, kg_ref, k_ref, v_ref):
    ckv = ckv_ref[...].astype(BF16)
    k_all = _dot(ckv, wk_ref[...]) + _dot(kpe_ref[...].astype(BF16), wkpe_ref[...])
    for h in range(D_HEADS):
        sl = slice(h * HEAD_PAD, (h + 1) * HEAD_PAD)
        k_ref[:, sl] = (_rms(k_all[:, sl], QK_DIM) * kg_ref[...]).astype(k_ref.dtype)
    v_ref[...] = _dot(ckv, wv_ref[...]).astype(v_ref.dtype)


def mla_kv(ckv_all, kpe_all, wk_p, wkpe_p, wv_p, k_gain_p, tm_cap=512):
    bsz, s, _ = ckv_all.shape
    tm = s if s % tm_cap else tm_cap
    width = D_HEADS * HEAD_PAD
    full = lambda r, c: pl.BlockSpec((r, c), lambda b, i: (0, 0))
    return pl.pallas_call(
        _mla_kv_kernel,
        grid=(bsz, s // tm),
        in_specs=[
            pl.BlockSpec((None, tm, KV_LORA), lambda b, i: (b, i, 0)),
            pl.BlockSpec((None, tm, ROPE_DIM), lambda b, i: (b, i, 0)),
            full(KV_LORA, width), full(ROPE_DIM, width), full(KV_LORA, width), full(1, HEAD_PAD),
        ],
        out_specs=[
            pl.BlockSpec((None, tm, width), lambda b, i: (b, i, 0)),
            pl.BlockSpec((None, tm, width), lambda b, i: (b, i, 0)),
        ],
        out_shape=[
            jax.ShapeDtypeStruct((bsz, s, width), BF16),
            jax.ShapeDtypeStruct((bsz, s, width), BF16),
        ],
        compiler_params=_cparams(("parallel", "parallel")),
        name="mla_kv",
    )(ckv_all, kpe_all, wk_p, wkpe_p, wv_p, k_gain_p)


def _attn_kernel(q_ref, k_ref, v_ref, o_ref, mx_ref, l_ref, acc_ref, *, tq, tk, s_len, q_pos0):
    qi = pl.program_id(1)
    q_first = q_pos0 + qi * tq
    q_last = q_first + tq - 1
    vis_first = jnp.minimum(s_len, (q_first // CHUNK + 1) * CHUNK)
    vis_last = jnp.minimum(s_len, (q_last // CHUNK + 1) * CHUNK)
    n_full = vis_first // tk
    n_kv = (vis_last + tk - 1) // tk

    q_chunk = _div_pow2(q_first + lax.broadcasted_iota(jnp.int32, (tq, tk), 0), CHUNK)
    k_iota = lax.broadcasted_iota(jnp.int32, (tq, tk), 1)
    heads = [slice(h * HEAD_PAD, (h + 1) * HEAD_PAD) for h in range(D_HEADS)]

    def scores(j, masked):
        k0 = pl.multiple_of(j * tk, tk)
        visible = (_div_pow2(k0 + k_iota, CHUNK) <= q_chunk) if masked else None
        for h, sl in enumerate(heads):
            s = _dot_nt(q_ref[:, sl], k_ref[pl.ds(k0, tk), sl])
            yield h, sl, k0, (jnp.where(visible, s, NEG_BIG) if masked else s)

    def max_pass(masked):
        def body(j, carry):
            for h, _, _, s in scores(j, masked):
                mx_ref[h] = jnp.maximum(mx_ref[h], s)
            return carry
        return body

    def sum_pass(masked):
        def body(j, carry):
            for h, sl, k0, s in scores(j, masked):
                p = jnp.exp(s - mx_ref[h])
                l_ref[h] += p
                acc_ref[h] += _dot(p.astype(BF16), v_ref[pl.ds(k0, tk), sl])
            return carry
        return body

    mx_ref[...] = jnp.full_like(mx_ref, NEG_BIG)
    lax.fori_loop(0, n_full, max_pass(False), 0)
    lax.fori_loop(n_full, n_kv, max_pass(True), 0)
    for h in range(D_HEADS):
        m = jnp.max(mx_ref[h], axis=-1, keepdims=True)
        mx_ref[h] = jnp.broadcast_to(m, (tq, tk))
    l_ref[...] = jnp.zeros_like(l_ref)
    acc_ref[...] = jnp.zeros_like(acc_ref)
    lax.fori_loop(0, n_full, sum_pass(False), 0)
    lax.fori_loop(n_full, n_kv, sum_pass(True), 0)
    for h, sl in enumerate(heads):
        l = jnp.sum(l_ref[h], axis=-1, keepdims=True)
        o_ref[:, sl] = (acc_ref[h] / l).astype(o_ref.dtype)


def attention(q, k, v, q_pos0, tq_cap=512, tk_cap=256):
    bsz, tq_len, width = q.shape
    s_len = k.shape[1]
    tq = _row_tile(tq_len, tq_cap)
    tk = tk_cap
    while s_len % tk:
        tk -= 2 * SUBLANES
    kern = functools.partial(_attn_kernel, tq=tq, tk=tk, s_len=s_len, q_pos0=q_pos0)
    return pl.pallas_call(
        kern,
        grid=(bsz, tq_len // tq),
        in_specs=[
            pl.BlockSpec((None, tq, width), lambda b, i: (b, i, 0)),
            pl.BlockSpec((None, s_len, width), lambda b, i: (b, 0, 0)),
            pl.BlockSpec((None, s_len, width), lambda b, i: (b, 0, 0)),
        ],
        out_specs=pl.BlockSpec((None, tq, width), lambda b, i: (b, i, 0)),
        out_shape=jax.ShapeDtypeStruct((bsz, tq_len, width), BF16),
        scratch_shapes=[pltpu.VMEM((D_HEADS, tq, tk), F32), pltpu.VMEM((D_HEADS, tq, tk), F32),
                        pltpu.VMEM((D_HEADS, tq, HEAD_PAD), F32)],
        compiler_params=_cparams(("parallel", "arbitrary")),
        name="attention",
    )(q, k, v)


def rope_tables(pos):
    half = ROPE_DIM // 2
    inv = ROPE_BASE ** (-jnp.arange(half, dtype=F32) / half)
    ang = pos.astype(F32)[:, None] * inv[None, :]
    cos, sin = jnp.cos(ang), jnp.sin(ang)
    n = pos.shape[0]
    z = lambda w: jnp.zeros((n, w), F32)
    o = lambda w: jnp.ones((n, w), F32)
    tail = LANES - NOPE_DIM - ROPE_DIM
    q_cos = jnp.concatenate([o(NOPE_DIM), cos, cos, z(tail)], axis=1)
    q_s1 = jnp.concatenate([z(NOPE_DIM), -sin, z(half), z(tail)], axis=1)
    q_s2 = jnp.concatenate([z(NOPE_DIM), z(half), sin, z(tail)], axis=1)
    k_cos = jnp.concatenate([cos, cos, z(LANES - ROPE_DIM)], axis=1)
    k_s1 = jnp.concatenate([-sin, z(LANES - half)], axis=1)
    k_s2 = jnp.concatenate([z(half), sin, z(LANES - ROPE_DIM)], axis=1)
    return q_cos, q_s1, q_s2, k_cos, k_s1, k_s2


def pack_mla_weights(w_uq, w_ukv, q_gain, k_gain, cd_w_out):
    pad_q = HEAD_PAD - QK_DIM
    wuq_p = jnp.pad(w_uq.reshape(Q_LORA, D_HEADS, QK_DIM), ((0, 0), (0, 0), (0, pad_q)))
    wuq_p = wuq_p.reshape(Q_LORA, D_HEADS * HEAD_PAD).astype(BF16)
    wkv = w_ukv.reshape(KV_LORA, D_HEADS, NOPE_DIM + V_DIM)
    wk_p = jnp.pad(wkv[:, :, :NOPE_DIM], ((0, 0), (0, 0), (0, HEAD_PAD - NOPE_DIM)))
    wk_p = wk_p.reshape(KV_LORA, D_HEADS * HEAD_PAD).astype(BF16)
    wv_p = jnp.pad(wkv[:, :, NOPE_DIM:], ((0, 0), (0, 0), (0, HEAD_PAD - V_DIM)))
    wv_p = wv_p.reshape(KV_LORA, D_HEADS * HEAD_PAD).astype(BF16)
    place = jnp.pad(jnp.eye(ROPE_DIM, dtype=F32), ((0, 0), (NOPE_DIM, HEAD_PAD - QK_DIM)))
    wkpe_p = jnp.tile(place, (1, D_HEADS)).astype(BF16)
    q_gain_p = jnp.pad(q_gain, (0, pad_q)).reshape(1, HEAD_PAD)
    k_gain_p = jnp.pad(k_gain, (0, pad_q)).reshape(1, HEAD_PAD)
    w_c = cd_w_out[:C_WIDTH].astype(BF16)
    w_d = cd_w_out[C_WIDTH:].reshape(D_HEADS, V_DIM, D_MODEL)
    w_d_p = jnp.pad(w_d, ((0, 0), (0, HEAD_PAD - V_DIM), (0, 0))).reshape(D_HEADS * HEAD_PAD, D_MODEL)
    return wuq_p, wk_p, wkpe_p, wv_p, q_gain_p, k_gain_p, w_c, w_d_p.astype(BF16)


def _trunk(x, pos0, w, past):
    bsz, t, _ = x.shape
    n = bsz * t
    x2 = x.reshape(n, D_MODEL)
    pos = pos0 + jnp.arange(t, dtype=jnp.int32)
    lb_all = jnp.cumsum(jax.nn.softmax(w['hgrn_lb_logits'], axis=0), axis=0)
    outs = {}
    for l in range(2):
        x2 = ffn(x2, w['ffn1_norm'][l], w['ffn1_w_gate'][l], w['ffn1_w_up'][l], w['ffn1_w_down'][l])
        if l == 0:
            proj = norm_matmul(x2, w['mix_norm'][l], w['ab_w_in']).reshape(bsz, t, AB_IN)
            s0 = None if past is None else past[0][0]
            a_out, s_t = hgrn2(proj, lb_all[0], w['hgrn_out_norm'][0], s0)
            if past is None:
                x0 = jnp.zeros((bsz, S5_STATE_W), F32)
            else:
                x0 = s5_pack_state(past[1][0], past[2][0])
            u_tb = jnp.transpose(proj[:, :, AB_IN - B_WIDTH:], (1, 0, 2))
            b_tb, x_t = s5_mix(u_tb, w['s5_wb'], w['s5_wc'], w['s5_lam'], w['s5_d'],
                               w['s5_w_glu'], w['s5_b_glu'], x0)
            b_out = jnp.transpose(b_tb, (1, 0, 2))
            x2 = proj_res(x2, a_out.reshape(n, A_WIDTH), b_out.reshape(n, B_WIDTH),
                          w['ab_w_out_a'], w['ab_w_out_b'])
            outs['hgrn'] = s_t
            outs['s5re'], outs['s5im'] = s5_unpack_state(x_t)
        else:
            proj = norm_matmul(x2, w['mix_norm'][l], w['cd_w_in_p']).reshape(bsz, t, CD_IN_PAD)
            c_out, v_rows = gmlp(proj, w['gmlp_v_norm'], w['gmlp_w_s'], w['gmlp_b_s'])
            q, ckv, kpe = mla_q(proj, rope_tables(pos), w['mla_q_norm'], w['mla_kv_norm'],
                                w['mla_wuq_p'], w['mla_q_gain_p'])
            if past is None:
                ckv_all, kpe_all = ckv, kpe
            else:
                ckv_all = jnp.concatenate([past[3][0], ckv], axis=1)
                kpe_all = jnp.concatenate([past[4][0], kpe], axis=1)
            k, v = mla_kv(ckv_all, kpe_all, w['mla_wk_p'], w['mla_wkpe_p'], w['mla_wv_p'],
                          w['mla_k_gain_p'])
            d_out = attention(q, k, v, pos0)
            x2 = proj_res(x2, c_out.reshape(n, C_WIDTH), d_out.reshape(n, D_HEADS * HEAD_PAD),
                          w['cd_w_out_c'], w['cd_w_out_d'])
            outs['gv'], outs['ckv'], outs['kpe'] = v_rows, ckv, kpe
        x2 = ffn(x2, w['ffn2_norm'][l], w['ffn2_w_gate'][l], w['ffn2_w_up'][l], w['ffn2_w_down'][l])
    return x2.reshape(bsz, t, D_MODEL), outs


def kernel(x_prompt, x_sample, state_hgrn, state_s5_re, state_s5_im, cache_mla_ckv, cache_mla_kpe, ffn1_norm, ffn1_w_gate, ffn1_w_up, ffn1_w_down, mix_norm, ffn2_norm, ffn2_w_gate, ffn2_w_up, ffn2_w_down, ab_w_in, hgrn_lb_logits, hgrn_out_norm, s5_lambda_re, s5_lambda_im, s5_log_dt, s5_b_re, s5_b_im, s5_c_re, s5_c_im, s5_d, s5_w_glu, s5_b_glu, ab_w_out, cd_w_in, gmlp_v_norm, gmlp_w_s, gmlp_b_s, mla_q_norm, mla_w_uq, mla_kv_norm, mla_w_ukv, mla_q_gain, mla_k_gain, cd_w_out):
    bf = lambda a: a.astype(BF16)
    wb, wc, lam_rows = s5_prepare(s5_lambda_re[0], s5_lambda_im[0], s5_log_dt[0],
                                  s5_b_re[0], s5_b_im[0], s5_c_re[0], s5_c_im[0])
    wuq_p, wk_p, wkpe_p, wv_p, q_gain_p, k_gain_p, w_c, w_d_p = pack_mla_weights(
        mla_w_uq[0], mla_w_ukv[0], mla_q_gain[0], mla_k_gain[0], cd_w_out[0])
    w = dict(
        ffn1_norm=ffn1_norm, ffn1_w_gate=bf(ffn1_w_gate), ffn1_w_up=bf(ffn1_w_up), ffn1_w_down=bf(ffn1_w_down),
        ffn2_norm=ffn2_norm, ffn2_w_gate=bf(ffn2_w_gate), ffn2_w_up=bf(ffn2_w_up), ffn2_w_down=bf(ffn2_w_down),
        mix_norm=mix_norm, ab_w_in=bf(ab_w_in[0]), hgrn_lb_logits=hgrn_lb_logits, hgrn_out_norm=hgrn_out_norm,
        s5_wb=wb, s5_wc=wc, s5_lam=lam_rows, s5_d=s5_d[0].reshape(B_WIDTH), s5_w_glu=bf(s5_w_glu[0]),
        s5_b_glu=s5_b_glu[0], ab_w_out_a=bf(ab_w_out[0, :A_WIDTH]), ab_w_out_b=bf(ab_w_out[0, A_WIDTH:]),
        cd_w_in_p=bf(jnp.pad(cd_w_in[0], ((0, 0), (0, CD_IN_PAD - CD_IN)))),
        gmlp_v_norm=gmlp_v_norm[0], gmlp_w_s=gmlp_w_s[0], gmlp_b_s=gmlp_b_s[0],
        mla_q_norm=mla_q_norm[0], mla_kv_norm=mla_kv_norm[0], mla_wuq_p=wuq_p, mla_q_gain_p=q_gain_p,
        mla_wk_p=wk_p, mla_wkpe_p=wkpe_p, mla_wv_p=wv_p, mla_k_gain_p=k_gain_p,
        cd_w_out_c=w_c, cd_w_out_d=w_d_p,
    )
    past_len = cache_mla_ckv.shape[2]
    y_p, o_p = _trunk(x_prompt, 0, w, None)
    y_s, o_s = _trunk(x_sample, past_len, w,
                      (state_hgrn, state_s5_re, state_s5_im, cache_mla_ckv, cache_mla_kpe))
    e = lambda a: a[None]
    return (y_p, y_s, e(o_p['hgrn']), e(o_s['hgrn']), e(o_p['s5re']), e(o_p['s5im']),
            e(o_s['s5re']), e(o_s['s5im']), e(o_p['gv']), e(o_s['gv']),
            e(o_p['ckv']), e(o_p['kpe']), e(o_s['ckv']), e(o_s['kpe']))
```

```python
import functools
import math

import jax
import jax.numpy as jnp
from jax import lax
from jax.experimental import pallas as pl
from jax.experimental.pallas import tpu as pltpu

D_MODEL = 1024
D_FF = 2816
EPS = 1e-6
A_HEADS = 4
A_KEY = 128
A_VAL = 128
A_WIDTH = A_HEADS * A_VAL
HGRN_BLOCK = 64
B_GROUP_DIM = 16
B_WIDTH = 512
B_GROUPS = 32
B_STATE = 64
LAMBDA_RE_CEIL = -1e-4
C_CHUNK = 128
C_WIDTH = 512
C_GROUPS = 4
C_GROUP_DIM = 128
D_HEADS = 8
Q_LORA = 256
KV_LORA = 128
NOPE_DIM = 64
ROPE_DIM = 32
V_DIM = 64
QK_DIM = NOPE_DIM + ROPE_DIM
ROPE_BASE = 10000.0
CHUNK = 64
A_QK = A_HEADS * A_KEY
AB_IN = 2 * A_QK + 2 * A_WIDTH + B_WIDTH
CD_IN = 2 * C_WIDTH + Q_LORA + KV_LORA + ROPE_DIM
CD_IN_PAD = 1536

LANES = 128
SUBLANES = 8
VMEM_LIMIT = 56 * 1024 * 1024
HGRN_SUB = 16
HEAD_PAD = LANES
NEG_BIG = -1e30

BF16 = jnp.bfloat16
F32 = jnp.float32


def _cparams(sem):
    return pltpu.CompilerParams(dimension_semantics=sem, vmem_limit_bytes=VMEM_LIMIT)


def _rms(xf, width):
    return xf * lax.rsqrt(jnp.sum(xf * xf, axis=-1, keepdims=True) * (1.0 / width) + EPS)


def _dot(a, b):
    return jnp.dot(a, b, preferred_element_type=F32)


def _dot_nt(a, b):
    return lax.dot_general(a, b, (((1,), (1,)), ((), ())), preferred_element_type=F32)


def _dot_tn(a, b):
    return lax.dot_general(a, b, (((0,), (0,)), ((), ())), preferred_element_type=F32)


def _div_pow2(x, d):
    assert d & (d - 1) == 0
    return lax.shift_right_logical(x, jnp.int32(d.bit_length() - 1))


def _mod_pow2(x, d):
    assert d & (d - 1) == 0
    return x & jnp.int32(d - 1)


def _row_tile(n, cap):
    t = min(n, cap)
    assert n % t == 0, (n, t)
    return t


FFN_CHUNK = 256
POST_CHUNK = 512


def _ffn_kernel(*refs, has_pre, has_post):
    refs = list(refs)
    x_ref = refs.pop(0)
    if has_pre:
        a1_ref, a2_ref, w1_ref, w2_ref = refs[:4]
        refs = refs[4:]
    g_ref, wg_ref, wu_ref, wd_ref = refs[:4]
    refs = refs[4:]
    if has_post:
        g2_ref, wp_ref = refs[:2]
        refs = refs[2:]
    o_ref = refs.pop(0)
    if has_post:
        p_ref = refs.pop(0)
    acc_ref = refs.pop(0)

    x = x_ref[...]
    if has_pre:
        x = x + _dot(a1_ref[...], w1_ref[...]) + _dot(a2_ref[...], w2_ref[...])
    h = (_rms(x, D_MODEL) * g_ref[...]).astype(BF16)
    for j in range(D_FF // FFN_CHUNK):
        cs = slice(j * FFN_CHUNK, (j + 1) * FFN_CHUNK)
        gate = _dot(h, wg_ref[:, cs])
        up = _dot(h, wu_ref[:, cs])
        act = (gate * jax.nn.sigmoid(gate) * up).astype(BF16)
        part = _dot(act, wd_ref[cs, :])
        if j == 0:
            acc_ref[...] = part
        else:
            acc_ref[...] += part
    out = x + 0.5 * acc_ref[...]
    o_ref[...] = out
    if has_post:
        h2 = (_rms(out, D_MODEL) * g2_ref[...]).astype(BF16)
        for c in range(wp_ref.shape[1] // POST_CHUNK):
            cs = slice(c * POST_CHUNK, (c + 1) * POST_CHUNK)
            p_ref[:, cs] = _dot(h2, wp_ref[:, cs]).astype(p_ref.dtype)


def ffn(x, g, wg, wu, wd, pre=None, post=None, tm_cap=512):
    n = x.shape[0]
    tm = _row_tile(n, tm_cap)
    rows = lambda w: pl.BlockSpec((tm, w), lambda i: (i, 0))
    fixed = lambda r, c: pl.BlockSpec((r, c), lambda i: (0, 0), pipeline_mode=pl.Buffered(1))
    args, specs = [x], [rows(D_MODEL)]
    if pre is not None:
        a1, a2, w1, w2 = pre
        args += [a1, a2, w1, w2]
        specs += [rows(a1.shape[1]), rows(a2.shape[1]), fixed(*w1.shape), fixed(*w2.shape)]
    args += [g.reshape(1, D_MODEL), wg, wu, wd]
    specs += [fixed(1, D_MODEL), fixed(D_MODEL, D_FF), fixed(D_MODEL, D_FF), fixed(D_FF, D_MODEL)]
    out_specs = [rows(D_MODEL)]
    out_shape = [jax.ShapeDtypeStruct((n, D_MODEL), F32)]
    if post is not None:
        g2, wp = post
        assert wp.shape[1] % POST_CHUNK == 0
        args += [g2.reshape(1, D_MODEL), wp]
        specs += [fixed(1, D_MODEL), fixed(*wp.shape)]
        out_specs.append(rows(wp.shape[1]))
        out_shape.append(jax.ShapeDtypeStruct((n, wp.shape[1]), BF16))
    kern = functools.partial(_ffn_kernel, has_pre=pre is not None, has_post=post is not None)
    res = pl.pallas_call(
        kern,
        grid=(n // tm,),
        in_specs=specs,
        out_specs=out_specs,
        out_shape=out_shape,
        scratch_shapes=[pltpu.VMEM((tm, D_MODEL), F32)],
        compiler_params=_cparams(("parallel",)),
        name="ffn",
    )(*args)
    return res if post is not None else res[0]


def _hgrn_kernel(q_ref, f_ref, i_ref, g_ref, lb_ref, og_ref, s0_ref, y_ref, st_ref, state_ref,
                 *, tt, blk, has_past):
    t_idx = pl.program_id(1)
    n_chunks = tt // blk
    n_sub = blk // HGRN_SUB

    @pl.when(t_idx == 0)
    def _():
        if has_past:
            for h in range(A_HEADS):
                state_ref[h] = s0_ref[h].T
        else:
            state_ref[...] = jnp.zeros_like(state_ref)

    row = lax.broadcasted_iota(jnp.int32, (tt, tt), 0)
    col = lax.broadcasted_iota(jnp.int32, (tt, tt), 1)
    same_chunk = _div_pow2(row, blk) == _div_pow2(col, blk)
    causal = same_chunk & (col <= row)
    tril = jnp.where(causal, 1.0, 0.0).astype(BF16)
    r1 = lax.broadcasted_iota(jnp.int32, (tt, LANES), 0)
    sub_of_row = _div_pow2(_mod_pow2(r1, blk), HGRN_SUB)

    for h in range(A_HEADS):
        sl = slice(h * LANES, (h + 1) * LANES)
        q = q_ref[:, sl].astype(F32)
        fz = f_ref[:, sl].astype(F32)
        v = i_ref[:, sl].astype(F32)
        gz = g_ref[:, sl].astype(F32)
        lb = lb_ref[:, sl]
        f = lb + (1.0 - lb) * jax.nn.sigmoid(fz)
        k = 1.0 - f
        logf = jnp.log(f)
        hi = logf.astype(BF16)
        mid = (logf - hi.astype(F32)).astype(BF16)
        b = _dot(tril, hi) + _dot(tril, mid)

        refs = []
        for i in range(n_sub):
            parts = []
            for c in range(n_chunks):
                r = c * blk + i * HGRN_SUB + HGRN_SUB // 2 - 1
                parts.append(jnp.broadcast_to(b[r:r + 1, :], (blk, LANES)))
            refs.append(parts[0] if n_chunks == 1 else jnp.concatenate(parts, axis=0))
        ends = []
        for c in range(n_chunks):
            r = c * blk + blk - 1
            ends.append(b[r:r + 1, :])
        b_end_rows = (jnp.broadcast_to(ends[0], (blk, LANES)) if n_chunks == 1 else
                      jnp.concatenate([jnp.broadcast_to(e, (blk, LANES)) for e in ends], axis=0))

        own_ref = refs[n_sub - 1]
        for i in range(n_sub - 2, -1, -1):
            own_ref = jnp.where(sub_of_row == i, refs[i], own_ref)
        q_t = q * jnp.exp(b - own_ref)
        q_hat = jnp.concatenate(
            [jnp.where(sub_of_row == i, q_t, 0.0).astype(BF16) for i in range(n_sub)], axis=1)
        k_hat = jnp.concatenate(
            [(k * jnp.exp(jnp.where(sub_of_row <= i, refs[i] - b, NEG_BIG))).astype(BF16)
             for i in range(n_sub)], axis=1)
        att = _dot_nt(q_hat, k_hat)
        att = jnp.where(causal, att, 0.0).astype(BF16)
        v_bf = v.astype(BF16)
        o = _dot(att, v_bf)

        q_e = (q * jnp.exp(b)).astype(BF16)
        k_d = (k * jnp.exp(b_end_rows - b)).astype(BF16)
        s_t = state_ref[h]
        o_parts = []
        for c in range(n_chunks):
            rs = slice(c * blk, (c + 1) * blk)
            o_parts.append(o[rs] + _dot_nt(q_e[rs], s_t.astype(BF16)))
            s_t = s_t * jnp.exp(ends[c]) + _dot_tn(v_bf[rs], k_d[rs])
        state_ref[h] = s_t
        o = o_parts[0] if n_chunks == 1 else jnp.concatenate(o_parts, axis=0)

        y = _rms(o, A_VAL) * og_ref[...] * jax.nn.sigmoid(gz)
        y_ref[:, sl] = y.astype(y_ref.dtype)

    @pl.when(t_idx == pl.num_programs(1) - 1)
    def _():
        for h in range(A_HEADS):
            st_ref[h] = state_ref[h].T


def hgrn2(proj, lb, out_g, s0, tt_cap=256):
    bsz, t, _ = proj.shape
    blk = min(HGRN_BLOCK, t)
    tt = _row_tile(t, tt_cap)
    has_past = s0 is not None
    if s0 is None:
        s0 = jnp.zeros((bsz, A_HEADS, A_KEY, A_VAL), F32)
    seg = lambda s: pl.BlockSpec((None, tt, A_WIDTH), lambda b, i, s=s: (b, i, s))
    kern = functools.partial(_hgrn_kernel, tt=tt, blk=blk, has_past=has_past)
    return pl.pallas_call(
        kern,
        grid=(bsz, t // tt),
        in_specs=[
            seg(0), seg(1), seg(2), seg(3),
            pl.BlockSpec((1, A_QK), lambda b, i: (0, 0)),
            pl.BlockSpec((1, A_VAL), lambda b, i: (0, 0)),
            pl.BlockSpec((None, A_HEADS, A_KEY, A_VAL), lambda b, i: (b, 0, 0, 0)),
        ],
        out_specs=[
            pl.BlockSpec((None, tt, A_WIDTH), lambda b, i: (b, i, 0)),
            pl.BlockSpec((None, A_HEADS, A_KEY, A_VAL), lambda b, i: (b, 0, 0, 0)),
        ],
        out_shape=[
            jax.ShapeDtypeStruct((bsz, t, A_WIDTH), BF16),
            jax.ShapeDtypeStruct((bsz, A_HEADS, A_KEY, A_VAL), F32),
        ],
        scratch_shapes=[pltpu.VMEM((A_HEADS, A_VAL, A_KEY), F32)],
        compiler_params=_cparams(("parallel", "arbitrary")),
        name="hgrn2",
    )(proj, proj, proj, proj, lb.reshape(1, A_QK), out_g.reshape(1, A_VAL), s0)


S5_SUPER = 4
S5_SUPER_U = B_WIDTH // S5_SUPER
S5_SUPER_X = 2 * (B_GROUPS // S5_SUPER) * B_STATE
S5_HALF = S5_SUPER_X // 2
S5_STATE_W = S5_SUPER * S5_SUPER_X


def _s5_kernel(u_ref, wb_ref, wc_ref, lam_ref, d_ref, wglu_ref, bglu_ref, x0_ref,
               o_ref, xt_ref, xbuf_ref, state_ref, *, tt):
    t_idx = pl.program_id(1)
    rows = tt * SUBLANES

    @pl.when(t_idx == 0)
    def _():
        state_ref[...] = x0_ref[...]

    u = u_ref[...].reshape(rows, B_WIDTH)
    u_bf = u.astype(BF16)
    for j in range(S5_SUPER):
        xbuf_ref[:, j * S5_SUPER_X:(j + 1) * S5_SUPER_X] = _dot(
            u_bf[:, j * S5_SUPER_U:(j + 1) * S5_SUPER_U], wb_ref[j])

    lam_re = jnp.broadcast_to(lam_ref[0:1, :], (SUBLANES, S5_STATE_W))
    lam_im = jnp.broadcast_to(lam_ref[1:2, :], (SUBLANES, S5_STATE_W))

    def step(t, x):
        r0 = pl.multiple_of(t * SUBLANES, SUBLANES)
        bu = xbuf_ref[pl.ds(r0, SUBLANES), :]
        pieces = []
        for j in range(S5_SUPER):
            c0 = j * S5_SUPER_X
            xr = x[:, c0:c0 + S5_HALF]
            xi = x[:, c0 + S5_HALF:c0 + S5_SUPER_X]
            lr = lam_re[:, c0:c0 + S5_HALF]
            li = lam_im[:, c0:c0 + S5_HALF]
            pieces.append(lr * xr - li * xi + bu[:, c0:c0 + S5_HALF])
            pieces.append(lr * xi + li * xr + bu[:, c0 + S5_HALF:c0 + S5_SUPER_X])
        x_new = jnp.concatenate(pieces, axis=1)
        xbuf_ref[pl.ds(r0, SUBLANES), :] = x_new
        return x_new

    x_last = lax.fori_loop(0, tt, step, state_ref[...])
    state_ref[...] = x_last

    ys = []
    for j in range(S5_SUPER):
        xs = xbuf_ref[:, j * S5_SUPER_X:(j + 1) * S5_SUPER_X].astype(BF16)
        ys.append(_dot(xs, wc_ref[j]))
    y = jnp.concatenate(ys, axis=1) + d_ref[...] * u
    z = jax.nn.gelu(y)
    gate = jax.nn.sigmoid(_dot(z.astype(BF16), wglu_ref[...]) + bglu_ref[...])
    o_ref[...] = (z * gate).astype(o_ref.dtype).reshape(tt, SUBLANES, B_WIDTH)

    @pl.when(t_idx == pl.num_programs(1) - 1)
    def _():
        xt_ref[...] = x_last


def s5_mix(u_tb, wb, wc, lam, d, w_glu, b_glu, x0, tt_cap=64):
    t, bsz, _ = u_tb.shape
    tt = _row_tile(t, tt_cap)
    kern = functools.partial(_s5_kernel, tt=tt)
    return pl.pallas_call(
        kern,
        grid=(bsz // SUBLANES, t // tt),
        in_specs=[
            pl.BlockSpec((tt, SUBLANES, B_WIDTH), lambda b, i: (i, b, 0)),
            pl.BlockSpec((S5_SUPER, S5_SUPER_U, S5_SUPER_X), lambda b, i: (0, 0, 0)),
            pl.BlockSpec((S5_SUPER, S5_SUPER_X, S5_SUPER_U), lambda b, i: (0, 0, 0)),
            pl.BlockSpec((2, S5_STATE_W), lambda b, i: (0, 0)),
            pl.BlockSpec((1, B_WIDTH), lambda b, i: (0, 0)),
            pl.BlockSpec((B_WIDTH, B_WIDTH), lambda b, i: (0, 0)),
            pl.BlockSpec((1, B_WIDTH), lambda b, i: (0, 0)),
            pl.BlockSpec((SUBLANES, S5_STATE_W), lambda b, i: (b, 0)),
        ],
        out_specs=[
            pl.BlockSpec((tt, SUBLANES, B_WIDTH), lambda b, i: (i, b, 0)),
            pl.BlockSpec((SUBLANES, S5_STATE_W), lambda b, i: (b, 0)),
        ],
        out_shape=[
            jax.ShapeDtypeStruct((t, bsz, B_WIDTH), BF16),
            jax.ShapeDtypeStruct((bsz, S5_STATE_W), F32),
        ],
        scratch_shapes=[pltpu.VMEM((tt * SUBLANES, S5_STATE_W), F32),
                        pltpu.VMEM((SUBLANES, S5_STATE_W), F32)],
        compiler_params=_cparams(("parallel", "arbitrary")),
        name="s5_mix",
    )(u_tb, wb, wc, lam, d.reshape(1, B_WIDTH), w_glu, b_glu.reshape(1, B_WIDTH), x0)


def _s5_pack_cols(a):
    return a.reshape(a.shape[:-2] + (S5_SUPER, S5_HALF))


def s5_prepare(lam_re, lam_im, log_dt, b_re, b_im, c_re, c_im):
    lr = jnp.minimum(lam_re, LAMBDA_RE_CEIL)
    li = lam_im
    dt = jnp.exp(log_dt)[:, None]
    mag = jnp.exp(lr * dt)
    bar_re = mag * jnp.cos(li * dt)
    bar_im = mag * jnp.sin(li * dt)
    den = lr * lr + li * li
    coef_re = (((bar_re - 1.0) * lr + bar_im * li) / den)[:, :, None]
    coef_im = ((bar_im * lr - (bar_re - 1.0) * li) / den)[:, :, None]
    bb_re = coef_re * b_re - coef_im * b_im
    bb_im = coef_re * b_im + coef_im * b_re
    gps = B_GROUPS // S5_SUPER
    eye = jnp.eye(gps, dtype=F32)

    def in_block(bm):
        bm = bm.reshape(S5_SUPER, gps, B_STATE, B_GROUP_DIM)
        blk = jnp.einsum('jgph,gk->jghkp', bm, eye)
        return blk.reshape(S5_SUPER, gps * B_GROUP_DIM, gps * B_STATE)

    def out_block(cm):
        cm = cm.reshape(S5_SUPER, gps, B_GROUP_DIM, B_STATE)
        blk = jnp.einsum('jghp,gk->jkpgh', cm, eye)
        return blk.reshape(S5_SUPER, gps * B_STATE, gps * B_GROUP_DIM)

    wb = jnp.concatenate([in_block(bb_re), in_block(bb_im)], axis=2).astype(BF16)
    wc = jnp.concatenate([out_block(c_re), out_block(-c_im)], axis=1).astype(BF16)
    lam_rows = jnp.stack([
        jnp.concatenate([_s5_pack_cols(bar_re)] * 2, axis=-1).reshape(S5_STATE_W),
        jnp.concatenate([_s5_pack_cols(bar_im)] * 2, axis=-1).reshape(S5_STATE_W)])
    return wb, wc, lam_rows


def s5_pack_state(x_re, x_im):
    bsz = x_re.shape[0]
    return jnp.concatenate([_s5_pack_cols(x_re), _s5_pack_cols(x_im)], axis=-1).reshape(bsz, S5_STATE_W)


def s5_unpack_state(x):
    bsz = x.shape[0]
    x = x.reshape(bsz, S5_SUPER, 2, S5_HALF)
    return (x[:, :, 0].reshape(bsz, B_GROUPS, B_STATE), x[:, :, 1].reshape(bsz, B_GROUPS, B_STATE))


def _gmlp_kernel(u_ref, v_ref, gn_ref, ws_ref, bs_ref, o_ref, vr_ref, *, tt, chunk):
    n_chunks = tt // chunk
    u = jax.nn.gelu(u_ref[...].astype(F32))
    v = jax.nn.gelu(v_ref[...].astype(F32))
    vn = _rms(v, C_WIDTH) * gn_ref[...]
    vn_bf = vn.astype(BF16)
    row = lax.broadcasted_iota(jnp.int32, (chunk, chunk), 0)
    col = lax.broadcasted_iota(jnp.int32, (chunk, chunk), 1)
    keep = col <= row
    for g in range(C_GROUPS):
        w = jnp.where(keep, ws_ref[g], 0.0).astype(BF16)
        bias = bs_ref[:, g:g + 1]
        cs = slice(g * C_GROUP_DIM, (g + 1) * C_GROUP_DIM)
        for c in range(n_chunks):
            rs = slice(c * chunk, (c + 1) * chunk)
            s = _dot(w, vn_bf[rs, cs]) + bias
            o_ref[rs, cs] = (u[rs, cs] * s).astype(o_ref.dtype)

    @pl.when(pl.program_id(1) == pl.num_programs(1) - 1)
    def _():
        vr_ref[...] = vn[tt - chunk:, :]


def gmlp(proj, v_norm, w_s, b_s, tt_cap=512):
    bsz, t, _ = proj.shape
    chunk = min(t, C_CHUNK)
    tt = _row_tile(t, tt_cap)
    ws = w_s[:, :chunk, :chunk]
    bs_t = b_s[:, :chunk].T
    kern = functools.partial(_gmlp_kernel, tt=tt, chunk=chunk)
    return pl.pallas_call(
        kern,
        grid=(bsz, t // tt),
        in_specs=[
            pl.BlockSpec((None, tt, C_WIDTH), lambda b, i: (b, i, 0)),
            pl.BlockSpec((None, tt, C_WIDTH), lambda b, i: (b, i, 1)),
            pl.BlockSpec((1, C_WIDTH), lambda b, i: (0, 0)),
            pl.BlockSpec((C_GROUPS, chunk, chunk), lambda b, i: (0, 0, 0)),
            pl.BlockSpec((chunk, C_GROUPS), lambda b, i: (0, 0)),
        ],
        out_specs=[
            pl.BlockSpec((None, tt, C_WIDTH), lambda b, i: (b, i, 0)),
            pl.BlockSpec((None, chunk, C_WIDTH), lambda b, i: (b, 0, 0)),
        ],
        out_shape=[
            jax.ShapeDtypeStruct((bsz, t, C_WIDTH), BF16),
            jax.ShapeDtypeStruct((bsz, chunk, C_WIDTH), F32),
        ],
        compiler_params=_cparams(("parallel", "arbitrary")),
        name="gmlp",
    )(proj, proj, v_norm.reshape(1, C_WIDTH), ws, bs_t)


def _rope_lanes(x, cos_t, sin_lo, sin_hi):
    half = ROPE_DIM // 2
    return (x * cos_t + pltpu.roll(x, LANES - half, axis=1) * sin_lo
            + pltpu.roll(x, half, axis=1) * sin_hi)


def _mla_q_kernel(cq_ref, ckv_ref, kpe_ref, qn_ref, kvn_ref, wuq_ref, qg_ref,
                  qc_ref, qs1_ref, qs2_ref, kc_ref, ks1_ref, ks2_ref,
                  q_ref, ckv_o_ref, kpe_o_ref):
    cqn = (_rms(cq_ref[...].astype(F32), Q_LORA) * qn_ref[...]).astype(BF16)
    q_all = _dot(cqn, wuq_ref[...])
    qc, qs1, qs2 = qc_ref[...], qs1_ref[...], qs2_ref[...]
    scale = QK_DIM ** -0.5
    for h in range(D_HEADS):
        sl = slice(h * HEAD_PAD, (h + 1) * HEAD_PAD)
        qh = _rope_lanes(q_all[:, sl], qc, qs1, qs2)
        qh = _rms(qh, QK_DIM) * (qg_ref[...] * scale)
        q_ref[:, sl] = qh.astype(q_ref.dtype)
    ckv_o_ref[...] = _rms(ckv_ref[...].astype(F32), KV_LORA) * kvn_ref[...]
    kpe = _rope_lanes(kpe_ref[...].astype(F32), kc_ref[...], ks1_ref[...], ks2_ref[...])
    kpe_o_ref[...] = kpe[:, :ROPE_DIM]


def mla_q(proj, tables, q_norm, kv_norm, wuq_p, q_gain_p, tm_cap=512):
    bsz, t, _ = proj.shape
    tm = _row_tile(t, tm_cap)
    tab = pl.BlockSpec((tm, LANES), lambda b, i: (i, 0))
    row = lambda w: pl.BlockSpec((1, w), lambda b, i: (0, 0))
    return pl.pallas_call(
        _mla_q_kernel,
        grid=(bsz, t // tm),
        in_specs=[
            pl.BlockSpec((None, tm, Q_LORA), lambda b, i: (b, i, 2 * C_WIDTH // Q_LORA)),
            pl.BlockSpec((None, tm, KV_LORA), lambda b, i: (b, i, (2 * C_WIDTH + Q_LORA) // KV_LORA)),
            pl.BlockSpec((None, tm, LANES), lambda b, i: (b, i, (2 * C_WIDTH + Q_LORA + KV_LORA) // LANES)),
            row(Q_LORA), row(KV_LORA),
            pl.BlockSpec((Q_LORA, D_HEADS * HEAD_PAD), lambda b, i: (0, 0)),
            row(HEAD_PAD),
            tab, tab, tab, tab, tab, tab,
        ],
        out_specs=[
            pl.BlockSpec((None, tm, D_HEADS * HEAD_PAD), lambda b, i: (b, i, 0)),
            pl.BlockSpec((None, tm, KV_LORA), lambda b, i: (b, i, 0)),
            pl.BlockSpec((None, tm, ROPE_DIM), lambda b, i: (b, i, 0)),
        ],
        out_shape=[
            jax.ShapeDtypeStruct((bsz, t, D_HEADS * HEAD_PAD), BF16),
            jax.ShapeDtypeStruct((bsz, t, KV_LORA), F32),
            jax.ShapeDtypeStruct((bsz, t, ROPE_DIM), F32),
        ],
        compiler_params=_cparams(("parallel", "parallel")),
        name="mla_q",
    )(proj, proj, proj, q_norm.reshape(1, Q_LORA), kv_norm.reshape(1, KV_LORA), wuq_p,
      q_gain_p, *tables)


def _mla_kv_kernel(ckv_ref, kpe_ref, wk_ref, wkpe_ref, wv_ref, kg_ref, k_ref, v_ref):
    ckv = ckv_ref[...].astype(BF16)
    k_all = _dot(ckv, wk_ref[...]) + _dot(kpe_ref[...].astype(BF16), wkpe_ref[...])
    for h in range(D_HEADS):
        sl = slice(h * HEAD_PAD, (h + 1) * HEAD_PAD)
        k_ref[:, sl] = (_rms(k_all[:, sl], QK_DIM) * kg_ref[...]).astype(k_ref.dtype)
    v_ref[...] = _dot(ckv, wv_ref[...]).astype(v_ref.dtype)


def mla_kv(ckv_all, kpe_all, wk_p, wkpe_p, wv_p, k_gain_p, tm_cap=512):
    bsz, s, _ = ckv_all.shape
    tm = s if s % tm_cap else tm_cap
    width = D_HEADS * HEAD_PAD
    full = lambda r, c: pl.BlockSpec((r, c), lambda b, i: (0, 0))
    return pl.pallas_call(
        _mla_kv_kernel,
        grid=(bsz, s // tm),
        in_specs=[
            pl.BlockSpec((None, tm, KV_LORA), lambda b, i: (b, i, 0)),
            pl.BlockSpec((None, tm, ROPE_DIM), lambda b, i: (b, i, 0)),
            full(KV_LORA, width), full(ROPE_DIM, width), full(KV_LORA, width), full(1, HEAD_PAD),
        ],
        out_specs=[
            pl.BlockSpec((None, tm, width), lambda b, i: (b, i, 0)),
            pl.BlockSpec((None, tm, width), lambda b, i: (b, i, 0)),
        ],
        out_shape=[
            jax.ShapeDtypeStruct((bsz, s, width), BF16),
            jax.ShapeDtypeStruct((bsz, s, width), BF16),
        ],
        compiler_params=_cparams(("parallel", "parallel")),
        name="mla_kv",
    )(ckv_all, kpe_all, wk_p, wkpe_p, wv_p, k_gain_p)


def _attn_kernel(q_ref, k_ref, v_ref, o_ref, mx_ref, l_ref, acc_ref, *, tq, tk, s_len, q_pos0):
    qi = pl.program_id(1)
    q_first = q_pos0 + qi * tq
    q_last = q_first + tq - 1
    vis_first = jnp.minimum(s_len, (q_first // CHUNK + 1) * CHUNK)
    vis_last = jnp.minimum(s_len, (q_last // CHUNK + 1) * CHUNK)
    n_full = vis_first // tk
    n_kv = (vis_last + tk - 1) // tk

    q_chunk = _div_pow2(q_first + lax.broadcasted_iota(jnp.int32, (tq, tk), 0), CHUNK)
    k_iota = lax.broadcasted_iota(jnp.int32, (tq, tk), 1)
    heads = [slice(h * HEAD_PAD, (h + 1) * HEAD_PAD) for h in range(D_HEADS)]

    def scores(j, masked):
        k0 = pl.multiple_of(j * tk, tk)
        visible = (_div_pow2(k0 + k_iota, CHUNK) <= q_chunk) if masked else None
        for h, sl in enumerate(heads):
            s = _dot_nt(q_ref[:, sl], k_ref[pl.ds(k0, tk), sl])
            yield h, sl, k0, (jnp.where(visible, s, NEG_BIG) if masked else s)

    def max_pass(masked):
        def body(j, carry):
            for h, _, _, s in scores(j, masked):
                mx_ref[h] = jnp.maximum(mx_ref[h], s)
            return carry
        return body

    def sum_pass(masked):
        def body(j, carry):
            for h, sl, k0, s in scores(j, masked):
                p = jnp.exp(s - mx_ref[h])
                l_ref[h] += p
                acc_ref[h] += _dot(p.astype(BF16), v_ref[pl.ds(k0, tk), sl])
            return carry
        return body

    mx_ref[...] = jnp.full_like(mx_ref, NEG_BIG)
    lax.fori_loop(0, n_full, max_pass(False), 0)
    lax.fori_loop(n_full, n_kv, max_pass(True), 0)
    for h in range(D_HEADS):
        m = jnp.max(mx_ref[h], axis=-1, keepdims=True)
        mx_ref[h] = jnp.broadcast_to(m, (tq, tk))
    l_ref[...] = jnp.zeros_like(l_ref)
    acc_ref[...] = jnp.zeros_like(acc_ref)
    lax.fori_loop(0, n_full, sum_pass(False), 0)
    lax.fori_loop(n_full, n_kv, sum_pass(True), 0)
    for h, sl in enumerate(heads):
        l = jnp.sum(l_ref[h], axis=-1, keepdims=True)
        o_ref[:, sl] = (acc_ref[h] / l).astype(o_ref.dtype)


def attention(q, k, v, q_pos0, tq_cap=512, tk_cap=256):
    bsz, tq_len, width = q.shape
    s_len = k.shape[1]
    tq = _row_tile(tq_len, tq_cap)
    tk = tk_cap
    while s_len % tk:
        tk -= 2 * SUBLANES
    kern = functools.partial(_attn_kernel, tq=tq, tk=tk, s_len=s_len, q_pos0=q_pos0)
    return pl.pallas_call(
        kern,
        grid=(bsz, tq_len // tq),
        in_specs=[
            pl.BlockSpec((None, tq, width), lambda b, i: (b, i, 0)),
            pl.BlockSpec((None, s_len, width), lambda b, i: (b, 0, 0)),
            pl.BlockSpec((None, s_len, width), lambda b, i: (b, 0, 0)),
        ],
        out_specs=pl.BlockSpec((None, tq, width), lambda b, i: (b, i, 0)),
        out_shape=jax.ShapeDtypeStruct((bsz, tq_len, width), BF16),
        scratch_shapes=[pltpu.VMEM((D_HEADS, tq, tk), F32), pltpu.VMEM((D_HEADS, tq, tk), F32),
                        pltpu.VMEM((D_HEADS, tq, HEAD_PAD), F32)],
        compiler_params=_cparams(("parallel", "arbitrary")),
        name="attention",
    )(q, k, v)


def _mla_decode_kernel(q_ref, ckv_ref, kpe_ref, wk_ref, wkt_ref, wv_ref, kg_ref, o_ref,
                       *, tq, s_len, q_pos0):
    s_pad = ckv_ref.shape[0]
    heads = [slice(h * HEAD_PAD, (h + 1) * HEAD_PAD) for h in range(D_HEADS)]
    ckv = ckv_ref[...].astype(BF16)
    kpe = kpe_ref[...].astype(BF16)
    eye = (lax.broadcasted_iota(jnp.int32, (LANES, LANES), 0)
           == lax.broadcasted_iota(jnp.int32, (LANES, LANES), 1))
    kpe_t = _dot_nt(jnp.where(eye, 1.0, 0.0).astype(BF16), kpe)
    pe_ss = jnp.sum(kpe_t * kpe_t, axis=0, keepdims=True)
    kn_t = _dot_nt(wkt_ref[...], ckv)
    inv_rms, q_lat, q_g = [], [], []
    for sl in heads:
        blk = kn_t[sl, :]
        ss = jnp.sum(blk * blk, axis=0, keepdims=True) + pe_ss
        inv_rms.append(jnp.broadcast_to(lax.rsqrt(ss * (1.0 / QK_DIM) + EPS), (tq, s_pad)))
        qg = (q_ref[:, sl].astype(F32) * kg_ref[...]).astype(BF16)
        q_g.append(qg)
        q_lat.append(_dot_nt(qg, wk_ref[:, sl]).astype(BF16))
    s = _dot_nt(jnp.concatenate(q_lat, axis=0), ckv) + _dot_nt(jnp.concatenate(q_g, axis=0), kpe)
    s = s * jnp.concatenate(inv_rms, axis=0)
    row = lax.broadcasted_iota(jnp.int32, s.shape, 0)
    col = lax.broadcasted_iota(jnp.int32, s.shape, 1)
    q_chunk = _div_pow2(q_pos0 + _mod_pow2(row, tq), CHUNK)
    visible = (_div_pow2(col, CHUNK) <= q_chunk) & (col < s_len)
    s = jnp.where(visible, s, NEG_BIG)
    p = jnp.exp(s - jnp.max(s, axis=-1, keepdims=True))
    lat = _dot(p.astype(BF16), ckv) / jnp.sum(p, axis=-1, keepdims=True)
    for h, sl in enumerate(heads):
        o_ref[:, sl] = _dot(lat[h * tq:(h + 1) * tq].astype(BF16), wv_ref[:, sl]).astype(o_ref.dtype)


def mla_decode(q, ckv_all, kpe_all, wk_p, wv_p, k_gain_p, q_pos0):
    bsz, tq, width = q.shape
    s_len = ckv_all.shape[1]
    s_pad = -(-s_len // LANES) * LANES
    assert tq & (tq - 1) == 0
    ckv_pad = jnp.pad(ckv_all, ((0, 0), (0, s_pad - s_len), (0, 0)))
    kpe_pad = jnp.pad(kpe_all, ((0, 0), (0, s_pad - s_len), (NOPE_DIM, HEAD_PAD - QK_DIM)))
    full = lambda r, c: pl.BlockSpec((r, c), lambda b: (0, 0))
    kern = functools.partial(_mla_decode_kernel, tq=tq, s_len=s_len, q_pos0=q_pos0)
    return pl.pallas_call(
        kern,
        grid=(bsz,),
        in_specs=[
            pl.BlockSpec((None, tq, width), lambda b: (b, 0, 0)),
            pl.BlockSpec((None, s_pad, KV_LORA), lambda b: (b, 0, 0)),
            pl.BlockSpec((None, s_pad, HEAD_PAD), lambda b: (b, 0, 0)),
            full(KV_LORA, width), full(width, KV_LORA), full(KV_LORA, width), full(1, HEAD_PAD),
        ],
        out_specs=pl.BlockSpec((None, tq, width), lambda b: (b, 0, 0)),
        out_shape=jax.ShapeDtypeStruct((bsz, tq, width), BF16),
        compiler_params=_cparams(("parallel",)),
        name="mla_decode",
    )(q, ckv_pad, kpe_pad, wk_p, wk_p.T, wv_p, k_gain_p)


def rope_tables(pos):
    half = ROPE_DIM // 2
    inv = ROPE_BASE ** (-jnp.arange(half, dtype=F32) / half)
    ang = pos.astype(F32)[:, None] * inv[None, :]
    cos, sin = jnp.cos(ang), jnp.sin(ang)
    n = pos.shape[0]
    z = lambda w: jnp.zeros((n, w), F32)
    o = lambda w: jnp.ones((n, w), F32)
    tail = LANES - NOPE_DIM - ROPE_DIM
    q_cos = jnp.concatenate([o(NOPE_DIM), cos, cos, z(tail)], axis=1)
    q_s1 = jnp.concatenate([z(NOPE_DIM), -sin, z(half), z(tail)], axis=1)
    q_s2 = jnp.concatenate([z(NOPE_DIM), z(half), sin, z(tail)], axis=1)
    k_cos = jnp.concatenate([cos, cos, z(LANES - ROPE_DIM)], axis=1)
    k_s1 = jnp.concatenate([-sin, z(LANES - half)], axis=1)
    k_s2 = jnp.concatenate([z(half), sin, z(LANES - ROPE_DIM)], axis=1)
    return q_cos, q_s1, q_s2, k_cos, k_s1, k_s2


def pack_mla_weights(w_uq, w_ukv, q_gain, k_gain, cd_w_out):
    pad_q = HEAD_PAD - QK_DIM
    wuq_p = jnp.pad(w_uq.reshape(Q_LORA, D_HEADS, QK_DIM), ((0, 0), (0, 0), (0, pad_q)))
    wuq_p = wuq_p.reshape(Q_LORA, D_HEADS * HEAD_PAD).astype(BF16)
    wkv = w_ukv.reshape(KV_LORA, D_HEADS, NOPE_DIM + V_DIM)
    wk_p = jnp.pad(wkv[:, :, :NOPE_DIM], ((0, 0), (0, 0), (0, HEAD_PAD - NOPE_DIM)))
    wk_p = wk_p.reshape(KV_LORA, D_HEADS * HEAD_PAD).astype(BF16)
    wv_p = jnp.pad(wkv[:, :, NOPE_DIM:], ((0, 0), (0, 0), (0, HEAD_PAD - V_DIM)))
    wv_p = wv_p.reshape(KV_LORA, D_HEADS * HEAD_PAD).astype(BF16)
    place = jnp.pad(jnp.eye(ROPE_DIM, dtype=F32), ((0, 0), (NOPE_DIM, HEAD_PAD - QK_DIM)))
    wkpe_p = jnp.tile(place, (1, D_HEADS)).astype(BF16)
    q_gain_p = jnp.pad(q_gain, (0, pad_q)).reshape(1, HEAD_PAD)
    k_gain_p = jnp.pad(k_gain, (0, pad_q)).reshape(1, HEAD_PAD)
    w_c = cd_w_out[:C_WIDTH].astype(BF16)
    w_d = cd_w_out[C_WIDTH:].reshape(D_HEADS, V_DIM, D_MODEL)
    w_d_p = jnp.pad(w_d, ((0, 0), (0, HEAD_PAD - V_DIM), (0, 0))).reshape(D_HEADS * HEAD_PAD, D_MODEL)
    return wuq_p, wk_p, wkpe_p, wv_p, q_gain_p, k_gain_p, w_c, w_d_p.astype(BF16)


def _trunk(x, pos0, w, past):
    bsz, t, _ = x.shape
    n = bsz * t
    x2 = x.reshape(n, D_MODEL)
    pos = pos0 + jnp.arange(t, dtype=jnp.int32)
    lb_all = jnp.cumsum(jax.nn.softmax(w['hgrn_lb_logits'], axis=0), axis=0)
    outs = {}
    ffn_w = lambda name, l: (w[name + '_norm'][l], w[name + '_w_gate'][l], w[name + '_w_up'][l],
                             w[name + '_w_down'][l])
    for l in range(2):
        w_in = w['ab_w_in'] if l == 0 else w['cd_w_in_p']
        x2, proj = ffn(x2, *ffn_w('ffn1', l), post=(w['mix_norm'][l], w_in))
        proj = proj.reshape(bsz, t, w_in.shape[1])
        if l == 0:
            s0 = None if past is None else past[0][0]
            a_out, s_t = hgrn2(proj, lb_all[0], w['hgrn_out_norm'][0], s0)
            if past is None:
                x0 = jnp.zeros((bsz, S5_STATE_W), F32)
            else:
                x0 = s5_pack_state(past[1][0], past[2][0])
            u_tb = jnp.transpose(proj[:, :, AB_IN - B_WIDTH:], (1, 0, 2)).astype(F32)
            b_tb, x_t = s5_mix(u_tb, w['s5_wb'], w['s5_wc'], w['s5_lam'], w['s5_d'],
                               w['s5_w_glu'], w['s5_b_glu'], x0)
            b_out = jnp.transpose(b_tb, (1, 0, 2))
            pre = (a_out.reshape(n, A_WIDTH), b_out.reshape(n, B_WIDTH),
                   w['ab_w_out_a'], w['ab_w_out_b'])
            outs['hgrn'] = s_t
            outs['s5re'], outs['s5im'] = s5_unpack_state(x_t)
        else:
            c_out, v_rows = gmlp(proj, w['gmlp_v_norm'], w['gmlp_w_s'], w['gmlp_b_s'])
            q, ckv, kpe = mla_q(proj, rope_tables(pos), w['mla_q_norm'], w['mla_kv_norm'],
                                w['mla_wuq_p'], w['mla_q_gain_p'])
            if past is None:
                k, v = mla_kv(ckv, kpe, w['mla_wk_p'], w['mla_wkpe_p'], w['mla_wv_p'],
                              w['mla_k_gain_p'])
                d_out = attention(q, k, v, pos0)
            else:
                ckv_all = jnp.concatenate([past[3][0], ckv], axis=1)
                kpe_all = jnp.concatenate([past[4][0], kpe], axis=1)
                d_out = mla_decode(q, ckv_all, kpe_all, w['mla_wk_p'], w['mla_wv_p'],
                                   w['mla_k_gain_p'], pos0)
            pre = (c_out.reshape(n, C_WIDTH), d_out.reshape(n, D_HEADS * HEAD_PAD),
                   w['cd_w_out_c'], w['cd_w_out_d'])
            outs['gv'], outs['ckv'], outs['kpe'] = v_rows, ckv, kpe
        x2 = ffn(x2, *ffn_w('ffn2', l), pre=pre)
    return x2.reshape(bsz, t, D_MODEL), outs


def kernel(x_prompt, x_sample, state_hgrn, state_s5_re, state_s5_im, cache_mla_ckv, cache_mla_kpe, ffn1_norm, ffn1_w_gate, ffn1_w_up, ffn1_w_down, mix_norm, ffn2_norm, ffn2_w_gate, ffn2_w_up, ffn2_w_down, ab_w_in, hgrn_lb_logits, hgrn_out_norm, s5_lambda_re, s5_lambda_im, s5_log_dt, s5_b_re, s5_b_im, s5_c_re, s5_c_im, s5_d, s5_w_glu, s5_b_glu, ab_w_out, cd_w_in, gmlp_v_norm, gmlp_w_s, gmlp_b_s, mla_q_norm, mla_w_uq, mla_kv_norm, mla_w_ukv, mla_q_gain, mla_k_gain, cd_w_out):
    bf = lambda a: a.astype(BF16)
    wb, wc, lam_rows = s5_prepare(s5_lambda_re[0], s5_lambda_im[0], s5_log_dt[0],
                                  s5_b_re[0], s5_b_im[0], s5_c_re[0], s5_c_im[0])
    wuq_p, wk_p, wkpe_p, wv_p, q_gain_p, k_gain_p, w_c, w_d_p = pack_mla_weights(
        mla_w_uq[0], mla_w_ukv[0], mla_q_gain[0], mla_k_gain[0], cd_w_out[0])
    w = dict(
        ffn1_norm=ffn1_norm, ffn1_w_gate=bf(ffn1_w_gate), ffn1_w_up=bf(ffn1_w_up), ffn1_w_down=bf(ffn1_w_down),
        ffn2_norm=ffn2_norm, ffn2_w_gate=bf(ffn2_w_gate), ffn2_w_up=bf(ffn2_w_up), ffn2_w_down=bf(ffn2_w_down),
        mix_norm=mix_norm, ab_w_in=bf(ab_w_in[0]), hgrn_lb_logits=hgrn_lb_logits, hgrn_out_norm=hgrn_out_norm,
        s5_wb=wb, s5_wc=wc, s5_lam=lam_rows, s5_d=s5_d[0].reshape(B_WIDTH), s5_w_glu=bf(s5_w_glu[0]),
        s5_b_glu=s5_b_glu[0], ab_w_out_a=bf(ab_w_out[0, :A_WIDTH]), ab_w_out_b=bf(ab_w_out[0, A_WIDTH:]),
        cd_w_in_p=bf(jnp.pad(cd_w_in[0], ((0, 0), (0, CD_IN_PAD - CD_IN)))),
        gmlp_v_norm=gmlp_v_norm[0], gmlp_w_s=gmlp_w_s[0], gmlp_b_s=gmlp_b_s[0],
        mla_q_norm=mla_q_norm[0], mla_kv_norm=mla_kv_norm[0], mla_wuq_p=wuq_p, mla_q_gain_p=q_gain_p,
        mla_wk_p=wk_p, mla_wkpe_p=wkpe_p, mla_wv_p=wv_p, mla_k_gain_p=k_gain_p,
        cd_w_out_c=w_c, cd_w_out_d=w_d_p,
    )
    past_len = cache_mla_ckv.shape[2]
    y_p, o_p = _trunk(x_prompt, 0, w, None)
    y_s, o_s = _trunk(x_sample, past_len, w,
                      (state_hgrn, state_s5_re, state_s5_im, cache_mla_ckv, cache_mla_kpe))
    e = lambda a: a[None]
    return (y_p, y_s, e(o_p['hgrn']), e(o_s['hgrn']), e(o_p['s5re']), e(o_p['s5im']),
            e(o_s['s5re']), e(o_s['s5im']), e(o_p['gv']), e(o_s['gv']),
            e(o_p['ckv']), e(o_p['kpe']), e(o_s['ckv']), e(o_s['kpe']))
```

```python
import functools
import math

import jax
import jax.numpy as jnp
from jax import lax
from jax.experimental import pallas as pl
from jax.experimental.pallas import tpu as pltpu

D_MODEL = 1024
D_FF = 2816
EPS = 1e-6
A_HEADS = 4
A_KEY = 128
A_VAL = 128
A_WIDTH = A_HEADS * A_VAL
HGRN_BLOCK = 64
B_GROUP_DIM = 16
B_WIDTH = 512
B_GROUPS = 32
B_STATE = 64
LAMBDA_RE_CEIL = -1e-4
C_CHUNK = 128
C_WIDTH = 512
C_GROUPS = 4
C_GROUP_DIM = 128
D_HEADS = 8
Q_LORA = 256
KV_LORA = 128
NOPE_DIM = 64
ROPE_DIM = 32
V_DIM = 64
QK_DIM = NOPE_DIM + ROPE_DIM
ROPE_BASE = 10000.0
CHUNK = 64
A_QK = A_HEADS * A_KEY
AB_IN = 2 * A_QK + 2 * A_WIDTH + B_WIDTH
CD_IN = 2 * C_WIDTH + Q_LORA + KV_LORA + ROPE_DIM
CD_IN_PAD = 1536

LANES = 128
SUBLANES = 8
VMEM_LIMIT = 56 * 1024 * 1024
HGRN_SUB = 16
HEAD_PAD = LANES
NEG_BIG = -1e30
V_ONE_LANE = V_DIM
ATTN_HEAD_GROUP = 4

BF16 = jnp.bfloat16
F32 = jnp.float32


def _cparams(sem):
    return pltpu.CompilerParams(dimension_semantics=sem, vmem_limit_bytes=VMEM_LIMIT)


def _rms(xf, width):
    return xf * lax.rsqrt(jnp.sum(xf * xf, axis=-1, keepdims=True) * (1.0 / width) + EPS)


def _dot(a, b):
    return jnp.dot(a, b, preferred_element_type=F32)


def _dot_nt(a, b):
    return lax.dot_general(a, b, (((1,), (1,)), ((), ())), preferred_element_type=F32)


def _dot_tn(a, b):
    return lax.dot_general(a, b, (((0,), (0,)), ((), ())), preferred_element_type=F32)


def _div_pow2(x, d):
    assert d & (d - 1) == 0
    return lax.shift_right_logical(x, jnp.int32(d.bit_length() - 1))


def _mod_pow2(x, d):
    assert d & (d - 1) == 0
    return x & jnp.int32(d - 1)


def _row_tile(n, cap):
    t = min(n, cap)
    assert n % t == 0, (n, t)
    return t


FFN_CHUNK = 256
POST_CHUNK = 512


def _ffn_kernel(*refs, has_pre, has_post):
    refs = list(refs)
    x_ref = refs.pop(0)
    if has_pre:
        a1_ref, a2_ref, w1_ref, w2_ref = refs[:4]
        refs = refs[4:]
    g_ref, wg_ref, wu_ref, wd_ref = refs[:4]
    refs = refs[4:]
    if has_post:
        g2_ref, wp_ref = refs[:2]
        refs = refs[2:]
    o_ref = refs.pop(0)
    if has_post:
        p_ref = refs.pop(0)
    acc_ref = refs.pop(0)

    x = x_ref[...]
    if has_pre:
        x = x + _dot(a1_ref[...], w1_ref[...]) + _dot(a2_ref[...], w2_ref[...])
    h = (_rms(x, D_MODEL) * g_ref[...]).astype(BF16)
    for j in range(D_FF // FFN_CHUNK):
        cs = slice(j * FFN_CHUNK, (j + 1) * FFN_CHUNK)
        gate = _dot(h, wg_ref[:, cs])
        up = _dot(h, wu_ref[:, cs])
        act = (gate * jax.nn.sigmoid(gate) * up).astype(BF16)
        part = _dot(act, wd_ref[cs, :])
        if j == 0:
            acc_ref[...] = part
        else:
            acc_ref[...] += part
    out = x + 0.5 * acc_ref[...]
    o_ref[...] = out
    if has_post:
        h2 = (_rms(out, D_MODEL) * g2_ref[...]).astype(BF16)
        for c in range(wp_ref.shape[1] // POST_CHUNK):
            cs = slice(c * POST_CHUNK, (c + 1) * POST_CHUNK)
            p_ref[:, cs] = _dot(h2, wp_ref[:, cs]).astype(p_ref.dtype)


def ffn(x, g, wg, wu, wd, pre=None, post=None, tm_cap=512):
    n = x.shape[0]
    tm = _row_tile(n, tm_cap)
    rows = lambda w: pl.BlockSpec((tm, w), lambda i: (i, 0))
    fixed = lambda r, c: pl.BlockSpec((r, c), lambda i: (0, 0), pipeline_mode=pl.Buffered(1))
    args, specs = [x], [rows(D_MODEL)]
    if pre is not None:
        a1, a2, w1, w2 = pre
        args += [a1, a2, w1, w2]
        specs += [rows(a1.shape[1]), rows(a2.shape[1]), fixed(*w1.shape), fixed(*w2.shape)]
    args += [g.reshape(1, D_MODEL), wg, wu, wd]
    specs += [fixed(1, D_MODEL), fixed(D_MODEL, D_FF), fixed(D_MODEL, D_FF), fixed(D_FF, D_MODEL)]
    out_specs = [rows(D_MODEL)]
    out_shape = [jax.ShapeDtypeStruct((n, D_MODEL), F32)]
    if post is not None:
        g2, wp = post
        assert wp.shape[1] % POST_CHUNK == 0
        args += [g2.reshape(1, D_MODEL), wp]
        specs += [fixed(1, D_MODEL), fixed(*wp.shape)]
        out_specs.append(rows(wp.shape[1]))
        out_shape.append(jax.ShapeDtypeStruct((n, wp.shape[1]), BF16))
    kern = functools.partial(_ffn_kernel, has_pre=pre is not None, has_post=post is not None)
    res = pl.pallas_call(
        kern,
        grid=(n // tm,),
        in_specs=specs,
        out_specs=out_specs,
        out_shape=out_shape,
        scratch_shapes=[pltpu.VMEM((tm, D_MODEL), F32)],
        compiler_params=_cparams(("parallel",)),
        name="ffn",
    )(*args)
    return res if post is not None else res[0]


def _hgrn_kernel(q_ref, f_ref, i_ref, g_ref, lb_ref, og_ref, s0_ref, y_ref, st_ref, state_ref,
                 *, tt, blk, has_past):
    t_idx = pl.program_id(1)
    n_chunks = tt // blk
    n_sub = blk // HGRN_SUB

    @pl.when(t_idx == 0)
    def _():
        if has_past:
            for h in range(A_HEADS):
                state_ref[h] = s0_ref[h].T
        else:
            state_ref[...] = jnp.zeros_like(state_ref)

    row = lax.broadcasted_iota(jnp.int32, (tt, tt), 0)
    col = lax.broadcasted_iota(jnp.int32, (tt, tt), 1)
    same_chunk = _div_pow2(row, blk) == _div_pow2(col, blk)
    causal = same_chunk & (col <= row)
    tril = jnp.where(causal, 1.0, 0.0).astype(BF16)
    heads = [slice(h * LANES, (h + 1) * LANES) for h in range(A_HEADS)]

    q = q_ref[...].astype(F32)
    v_bf = i_ref[...].astype(BF16)
    lb = lb_ref[...]
    f = lb + (1.0 - lb) * jax.nn.sigmoid(f_ref[...].astype(F32))
    k = 1.0 - f
    logf = jnp.log(f)
    hi = logf.astype(BF16)
    mid = (logf - hi.astype(F32)).astype(BF16)
    b = _dot(tril, hi) + _dot(tril, mid)

    q_slab, k_slab, ends = [], [], []
    for c in range(n_chunks):
        rows = [slice(c * blk + i * HGRN_SUB, c * blk + (i + 1) * HGRN_SUB) for i in range(n_sub)]
        mids = [b[r.start + HGRN_SUB // 2 - 1:r.start + HGRN_SUB // 2, :] for r in rows]
        q_slab.append([(q[r] * jnp.exp(b[r] - mids[i])).astype(BF16) for i, r in enumerate(rows)])
        k_slab.append([[(k[r] * jnp.exp(mids[i] - b[r])).astype(BF16) if j <= i else None
                        for i in range(n_sub)] for j, r in enumerate(rows)])
        ends.append(b[(c + 1) * blk - 1:(c + 1) * blk, :])
    b_end_rows = (jnp.broadcast_to(ends[0], (blk, A_WIDTH)) if n_chunks == 1 else
                  jnp.concatenate([jnp.broadcast_to(e, (blk, A_WIDTH)) for e in ends], axis=0))
    q_e = (q * jnp.exp(b)).astype(BF16)
    k_d = (k * jnp.exp(b_end_rows - b)).astype(BF16)
    zero_slab = jnp.zeros((HGRN_SUB, LANES), BF16)
    zero_blk = jnp.zeros((blk, LANES), BF16)
    chunk_rows = [slice(c * blk, (c + 1) * blk) for c in range(n_chunks)]

    outs = []
    for h, sl in enumerate(heads):
        q_hat = jnp.concatenate(
            [jnp.concatenate([q_slab[c][j][:, sl] if i == j else zero_slab for i in range(n_sub)], axis=1)
             for c in range(n_chunks) for j in range(n_sub)], axis=0)
        k_hat = jnp.concatenate(
            [jnp.concatenate([k_slab[c][j][i][:, sl] if j <= i else zero_slab for i in range(n_sub)], axis=1)
             for c in range(n_chunks) for j in range(n_sub)], axis=0)
        att = _dot_nt(q_hat, k_hat)
        att = jnp.where(causal, att, 0.0).astype(BF16)
        spread = lambda a: jnp.concatenate(
            [jnp.concatenate([a[rs, sl] if c == d else zero_blk for d in range(n_chunks)], axis=1)
             for c, rs in enumerate(chunk_rows)], axis=0)
        kv = _dot_tn(spread(v_bf), k_d[:, sl])
        s_t = state_ref[h]
        states = []
        for c in range(n_chunks):
            states.append(s_t.astype(BF16))
            s_t = s_t * jnp.exp(ends[c][:, sl]) + kv[c * A_VAL:(c + 1) * A_VAL]
        state_ref[h] = s_t
        o = _dot(att, v_bf[:, sl]) + _dot_nt(spread(q_e), jnp.concatenate(states, axis=1))
        outs.append(_rms(o, A_VAL) * og_ref[...])
    y = jnp.concatenate(outs, axis=1) * jax.nn.sigmoid(g_ref[...].astype(F32))
    y_ref[...] = y.astype(y_ref.dtype)

    @pl.when(t_idx == pl.num_programs(1) - 1)
    def _():
        for h in range(A_HEADS):
            st_ref[h] = state_ref[h].T


def hgrn2(proj, lb, out_g, s0, tt_cap=256):
    bsz, t, _ = proj.shape
    blk = min(HGRN_BLOCK, t)
    tt = _row_tile(t, tt_cap)
    has_past = s0 is not None
    if s0 is None:
        s0 = jnp.zeros((bsz, A_HEADS, A_KEY, A_VAL), F32)
    seg = lambda s: pl.BlockSpec((None, tt, A_WIDTH), lambda b, i, s=s: (b, i, s))
    kern = functools.partial(_hgrn_kernel, tt=tt, blk=blk, has_past=has_past)
    return pl.pallas_call(
        kern,
        grid=(bsz, t // tt),
        in_specs=[
            seg(0), seg(1), seg(2), seg(3),
            pl.BlockSpec((1, A_QK), lambda b, i: (0, 0)),
            pl.BlockSpec((1, A_VAL), lambda b, i: (0, 0)),
            pl.BlockSpec((None, A_HEADS, A_KEY, A_VAL), lambda b, i: (b, 0, 0, 0)),
        ],
        out_specs=[
            pl.BlockSpec((None, tt, A_WIDTH), lambda b, i: (b, i, 0)),
            pl.BlockSpec((None, A_HEADS, A_KEY, A_VAL), lambda b, i: (b, 0, 0, 0)),
        ],
        out_shape=[
            jax.ShapeDtypeStruct((bsz, t, A_WIDTH), BF16),
            jax.ShapeDtypeStruct((bsz, A_HEADS, A_KEY, A_VAL), F32),
        ],
        scratch_shapes=[pltpu.VMEM((A_HEADS, A_VAL, A_KEY), F32)],
        compiler_params=_cparams(("parallel", "arbitrary")),
        name="hgrn2",
    )(proj, proj, proj, proj, lb.reshape(1, A_QK), out_g.reshape(1, A_VAL), s0)


S5_SUPER = 4
S5_SUPER_U = B_WIDTH // S5_SUPER
S5_SUPER_X = 2 * (B_GROUPS // S5_SUPER) * B_STATE
S5_HALF = S5_SUPER_X // 2
S5_STATE_W = S5_SUPER * S5_SUPER_X
S5_SCAN_UNROLL = 8
S5_SCAN_GROUP = 2


def _s5_kernel(u_ref, wb_ref, wc_ref, lam_ref, d_ref, wglu_ref, bglu_ref, x0_ref,
               o_ref, xt_ref, xbuf_ref, state_ref, *, tt):
    t_idx = pl.program_id(1)
    rows = tt * SUBLANES

    @pl.when(t_idx == 0)
    def _():
        state_ref[...] = x0_ref[...]

    u = u_ref[...].reshape(rows, B_WIDTH)
    u_bf = u.astype(BF16)
    for j in range(S5_SUPER):
        xbuf_ref[:, j * S5_SUPER_X:(j + 1) * S5_SUPER_X] = _dot(
            u_bf[:, j * S5_SUPER_U:(j + 1) * S5_SUPER_U], wb_ref[j])

    for j0 in range(0, S5_SUPER, S5_SCAN_GROUP):
        cols, lams = [], []
        for j in range(j0, j0 + S5_SCAN_GROUP):
            re_cols = slice(j * S5_SUPER_X, j * S5_SUPER_X + S5_HALF)
            im_cols = slice(j * S5_SUPER_X + S5_HALF, (j + 1) * S5_SUPER_X)
            cols.append((re_cols, im_cols))
            lams.append((jnp.broadcast_to(lam_ref[0:1, re_cols], (SUBLANES, S5_HALF)),
                         jnp.broadcast_to(lam_ref[1:2, re_cols], (SUBLANES, S5_HALF))))

        def step(t, carry, cols=cols, lams=lams):
            rows_t = pl.ds(pl.multiple_of(t * SUBLANES, SUBLANES), SUBLANES)
            out = []
            for (re_cols, im_cols), (lam_re, lam_im), (x_re, x_im) in zip(cols, lams, carry):
                n_re = lam_re * x_re - lam_im * x_im + xbuf_ref[rows_t, re_cols]
                n_im = lam_re * x_im + lam_im * x_re + xbuf_ref[rows_t, im_cols]
                xbuf_ref[rows_t, re_cols] = n_re
                xbuf_ref[rows_t, im_cols] = n_im
                out.append((n_re, n_im))
            return tuple(out)

        init = tuple((state_ref[:, rc], state_ref[:, ic]) for rc, ic in cols)
        last = lax.fori_loop(0, tt, step, init, unroll=S5_SCAN_UNROLL)
        for (rc, ic), (x_re, x_im) in zip(cols, last):
            state_ref[:, rc] = x_re
            state_ref[:, ic] = x_im

    ys = []
    for j in range(S5_SUPER):
        xs = xbuf_ref[:, j * S5_SUPER_X:(j + 1) * S5_SUPER_X].astype(BF16)
        ys.append(_dot(xs, wc_ref[j]))
    y = jnp.concatenate(ys, axis=1) + d_ref[...] * u
    z = jax.nn.gelu(y)
    gate = jax.nn.sigmoid(_dot(z.astype(BF16), wglu_ref[...]) + bglu_ref[...])
    o_ref[...] = (z * gate).astype(o_ref.dtype).reshape(tt, SUBLANES, B_WIDTH)

    @pl.when(t_idx == pl.num_programs(1) - 1)
    def _():
        xt_ref[...] = state_ref[...]


def s5_mix(u_tb, wb, wc, lam, d, w_glu, b_glu, x0, tt_cap=64):
    t, bsz, _ = u_tb.shape
    tt = _row_tile(t, tt_cap)
    kern = functools.partial(_s5_kernel, tt=tt)
    return pl.pallas_call(
        kern,
        grid=(bsz // SUBLANES, t // tt),
        in_specs=[
            pl.BlockSpec((tt, SUBLANES, B_WIDTH), lambda b, i: (i, b, 0)),
            pl.BlockSpec((S5_SUPER, S5_SUPER_U, S5_SUPER_X), lambda b, i: (0, 0, 0)),
            pl.BlockSpec((S5_SUPER, S5_SUPER_X, S5_SUPER_U), lambda b, i: (0, 0, 0)),
            pl.BlockSpec((2, S5_STATE_W), lambda b, i: (0, 0)),
            pl.BlockSpec((1, B_WIDTH), lambda b, i: (0, 0)),
            pl.BlockSpec((B_WIDTH, B_WIDTH), lambda b, i: (0, 0)),
            pl.BlockSpec((1, B_WIDTH), lambda b, i: (0, 0)),
            pl.BlockSpec((SUBLANES, S5_STATE_W), lambda b, i: (b, 0)),
        ],
        out_specs=[
            pl.BlockSpec((tt, SUBLANES, B_WIDTH), lambda b, i: (i, b, 0)),
            pl.BlockSpec((SUBLANES, S5_STATE_W), lambda b, i: (b, 0)),
        ],
        out_shape=[
            jax.ShapeDtypeStruct((t, bsz, B_WIDTH), BF16),
            jax.ShapeDtypeStruct((bsz, S5_STATE_W), F32),
        ],
        scratch_shapes=[pltpu.VMEM((tt * SUBLANES, S5_STATE_W), F32),
                        pltpu.VMEM((SUBLANES, S5_STATE_W), F32)],
        compiler_params=_cparams(("parallel", "arbitrary")),
        name="s5_mix",
    )(u_tb, wb, wc, lam, d.reshape(1, B_WIDTH), w_glu, b_glu.reshape(1, B_WIDTH), x0)


def _s5_pack_cols(a):
    return a.reshape(a.shape[:-2] + (S5_SUPER, S5_HALF))


def s5_prepare(lam_re, lam_im, log_dt, b_re, b_im, c_re, c_im):
    lr = jnp.minimum(lam_re, LAMBDA_RE_CEIL)
    li = lam_im
    dt = jnp.exp(log_dt)[:, None]
    mag = jnp.exp(lr * dt)
    bar_re = mag * jnp.cos(li * dt)
    bar_im = mag * jnp.sin(li * dt)
    den = lr * lr + li * li
    coef_re = (((bar_re - 1.0) * lr + bar_im * li) / den)[:, :, None]
    coef_im = ((bar_im * lr - (bar_re - 1.0) * li) / den)[:, :, None]
    bb_re = coef_re * b_re - coef_im * b_im
    bb_im = coef_re * b_im + coef_im * b_re
    gps = B_GROUPS // S5_SUPER
    eye = jnp.eye(gps, dtype=F32)

    def in_block(bm):
        bm = bm.reshape(S5_SUPER, gps, B_STATE, B_GROUP_DIM)
        blk = jnp.einsum('jgph,gk->jghkp', bm, eye)
        return blk.reshape(S5_SUPER, gps * B_GROUP_DIM, gps * B_STATE)

    def out_block(cm):
        cm = cm.reshape(S5_SUPER, gps, B_GROUP_DIM, B_STATE)
        blk = jnp.einsum('jghp,gk->jkpgh', cm, eye)
        return blk.reshape(S5_SUPER, gps * B_STATE, gps * B_GROUP_DIM)

    wb = jnp.concatenate([in_block(bb_re), in_block(bb_im)], axis=2).astype(BF16)
    wc = jnp.concatenate([out_block(c_re), out_block(-c_im)], axis=1).astype(BF16)
    lam_rows = jnp.stack([
        jnp.concatenate([_s5_pack_cols(bar_re)] * 2, axis=-1).reshape(S5_STATE_W),
        jnp.concatenate([_s5_pack_cols(bar_im)] * 2, axis=-1).reshape(S5_STATE_W)])
    return wb, wc, lam_rows


def s5_pack_state(x_re, x_im):
    bsz = x_re.shape[0]
    return jnp.concatenate([_s5_pack_cols(x_re), _s5_pack_cols(x_im)], axis=-1).reshape(bsz, S5_STATE_W)


def s5_unpack_state(x):
    bsz = x.shape[0]
    x = x.reshape(bsz, S5_SUPER, 2, S5_HALF)
    return (x[:, :, 0].reshape(bsz, B_GROUPS, B_STATE), x[:, :, 1].reshape(bsz, B_GROUPS, B_STATE))


def _gmlp_kernel(u_ref, v_ref, gn_ref, ws_ref, bs_ref, o_ref, vr_ref, *, tt, chunk):
    n_chunks = tt // chunk
    u = jax.nn.gelu(u_ref[...].astype(F32))
    v = jax.nn.gelu(v_ref[...].astype(F32))
    vn = _rms(v, C_WIDTH) * gn_ref[...]
    vn_bf = vn.astype(BF16)
    row = lax.broadcasted_iota(jnp.int32, (chunk, chunk), 0)
    col = lax.broadcasted_iota(jnp.int32, (chunk, chunk), 1)
    keep = col <= row
    for g in range(C_GROUPS):
        w = jnp.where(keep, ws_ref[g], 0.0).astype(BF16)
        bias = bs_ref[:, g:g + 1]
        cs = slice(g * C_GROUP_DIM, (g + 1) * C_GROUP_DIM)
        for c in range(n_chunks):
            rs = slice(c * chunk, (c + 1) * chunk)
            s = _dot(w, vn_bf[rs, cs]) + bias
            o_ref[rs, cs] = (u[rs, cs] * s).astype(o_ref.dtype)

    @pl.when(pl.program_id(1) == pl.num_programs(1) - 1)
    def _():
        vr_ref[...] = vn[tt - chunk:, :]


def gmlp(proj, v_norm, w_s, b_s, tt_cap=512):
    bsz, t, _ = proj.shape
    chunk = min(t, C_CHUNK)
    tt = _row_tile(t, tt_cap)
    ws = w_s[:, :chunk, :chunk]
    bs_t = b_s[:, :chunk].T
    kern = functools.partial(_gmlp_kernel, tt=tt, chunk=chunk)
    return pl.pallas_call(
        kern,
        grid=(bsz, t // tt),
        in_specs=[
            pl.BlockSpec((None, tt, C_WIDTH), lambda b, i: (b, i, 0)),
            pl.BlockSpec((None, tt, C_WIDTH), lambda b, i: (b, i, 1)),
            pl.BlockSpec((1, C_WIDTH), lambda b, i: (0, 0)),
            pl.BlockSpec((C_GROUPS, chunk, chunk), lambda b, i: (0, 0, 0)),
            pl.BlockSpec((chunk, C_GROUPS), lambda b, i: (0, 0)),
        ],
        out_specs=[
            pl.BlockSpec((None, tt, C_WIDTH), lambda b, i: (b, i, 0)),
            pl.BlockSpec((None, chunk, C_WIDTH), lambda b, i: (b, 0, 0)),
        ],
        out_shape=[
            jax.ShapeDtypeStruct((bsz, t, C_WIDTH), BF16),
            jax.ShapeDtypeStruct((bsz, chunk, C_WIDTH), F32),
        ],
        compiler_params=_cparams(("parallel", "arbitrary")),
        name="gmlp",
    )(proj, proj, v_norm.reshape(1, C_WIDTH), ws, bs_t)


def _rope_lanes(x, cos_t, sin_lo, sin_hi):
    half = ROPE_DIM // 2
    return (x * cos_t + pltpu.roll(x, LANES - half, axis=1) * sin_lo
            + pltpu.roll(x, half, axis=1) * sin_hi)


def _mla_q_kernel(cq_ref, ckv_ref, kpe_ref, qn_ref, kvn_ref, wuq_ref, qg_ref,
                  qc_ref, qs_ref, kc_ref, ks1_ref, ks2_ref,
                  q_ref, ckv_o_ref, kpe_o_ref):
    cqn = (_rms(cq_ref[...].astype(F32), Q_LORA) * qn_ref[...]).astype(BF16)
    width = D_HEADS * HEAD_PAD
    q_all = _dot(cqn, wuq_ref[...])
    qc, qs = qc_ref[...], qs_ref[...]
    scale = QK_DIM ** -0.5 * math.log2(math.e)
    for h in range(D_HEADS):
        sl = slice(h * HEAD_PAD, (h + 1) * HEAD_PAD)
        rot = slice(width + h * HEAD_PAD, width + (h + 1) * HEAD_PAD)
        qh = q_all[:, sl] * qc + q_all[:, rot] * qs
        qh = _rms(qh, QK_DIM) * (qg_ref[...] * scale)
        q_ref[:, sl] = qh.astype(q_ref.dtype)
    ckv_o_ref[...] = _rms(ckv_ref[...].astype(F32), KV_LORA) * kvn_ref[...]
    kpe = _rope_lanes(kpe_ref[...].astype(F32), kc_ref[...], ks1_ref[...], ks2_ref[...])
    kpe_o_ref[...] = kpe[:, :ROPE_DIM]


def mla_q(proj, tables, q_norm, kv_norm, wuq_p, q_gain_p, tm_cap=512):
    bsz, t, _ = proj.shape
    tm = _row_tile(t, tm_cap)
    tab = pl.BlockSpec((tm, LANES), lambda b, i: (i, 0))
    row = lambda w: pl.BlockSpec((1, w), lambda b, i: (0, 0))
    return pl.pallas_call(
        _mla_q_kernel,
        grid=(bsz, t // tm),
        in_specs=[
            pl.BlockSpec((None, tm, Q_LORA), lambda b, i: (b, i, 2 * C_WIDTH // Q_LORA)),
            pl.BlockSpec((None, tm, KV_LORA), lambda b, i: (b, i, (2 * C_WIDTH + Q_LORA) // KV_LORA)),
            pl.BlockSpec((None, tm, LANES), lambda b, i: (b, i, (2 * C_WIDTH + Q_LORA + KV_LORA) // LANES)),
            row(Q_LORA), row(KV_LORA),
            pl.BlockSpec((Q_LORA, 2 * D_HEADS * HEAD_PAD), lambda b, i: (0, 0)),
            row(HEAD_PAD),
            tab, tab, tab, tab, tab,
        ],
        out_specs=[
            pl.BlockSpec((None, tm, D_HEADS * HEAD_PAD), lambda b, i: (b, i, 0)),
            pl.BlockSpec((None, tm, KV_LORA), lambda b, i: (b, i, 0)),
            pl.BlockSpec((None, tm, ROPE_DIM), lambda b, i: (b, i, 0)),
        ],
        out_shape=[
            jax.ShapeDtypeStruct((bsz, t, D_HEADS * HEAD_PAD), BF16),
            jax.ShapeDtypeStruct((bsz, t, KV_LORA), F32),
            jax.ShapeDtypeStruct((bsz, t, ROPE_DIM), F32),
        ],
        compiler_params=_cparams(("parallel", "parallel")),
        name="mla_q",
    )(proj, proj, proj, q_norm.reshape(1, Q_LORA), kv_norm.reshape(1, KV_LORA), wuq_p,
      q_gain_p, *tables)


def _mla_kv_kernel(ckv_ref, kpe_ref, wk_ref, wkpe_ref, wv_ref, kg_ref, k_ref, v_ref):
    ckv = ckv_ref[...].astype(BF16)
    k_all = _dot(ckv, wk_ref[...]) + _dot(kpe_ref[...].astype(BF16), wkpe_ref[...])
    for h in range(D_HEADS):
        sl = slice(h * HEAD_PAD, (h + 1) * HEAD_PAD)
        k_ref[:, sl] = (_rms(k_all[:, sl], QK_DIM) * kg_ref[...]).astype(k_ref.dtype)
    lane = lax.broadcasted_iota(jnp.int32, (1, D_HEADS * HEAD_PAD), 1)
    ones_col = jnp.where(_mod_pow2(lane, HEAD_PAD) == V_ONE_LANE, 1.0, 0.0)
    v_ref[...] = (_dot(ckv, wv_ref[...]) + ones_col).astype(v_ref.dtype)


def mla_kv(ckv_all, kpe_all, wk_p, wkpe_p, wv_p, k_gain_p, tm_cap=512):
    bsz, s, _ = ckv_all.shape
    tm = s if s % tm_cap else tm_cap
    width = D_HEADS * HEAD_PAD
    full = lambda r, c: pl.BlockSpec((r, c), lambda b, i: (0, 0))
    return pl.pallas_call(
        _mla_kv_kernel,
        grid=(bsz, s // tm),
        in_specs=[
            pl.BlockSpec((None, tm, KV_LORA), lambda b, i: (b, i, 0)),
            pl.BlockSpec((None, tm, ROPE_DIM), lambda b, i: (b, i, 0)),
            full(KV_LORA, width), full(ROPE_DIM, width), full(KV_LORA, width), full(1, HEAD_PAD),
        ],
        out_specs=[
            pl.BlockSpec((None, tm, width), lambda b, i: (b, i, 0)),
            pl.BlockSpec((None, tm, width), lambda b, i: (b, i, 0)),
        ],
        out_shape=[
            jax.ShapeDtypeStruct((bsz, s, width), BF16),
            jax.ShapeDtypeStruct((bsz, s, width), BF16),
        ],
        compiler_params=_cparams(("parallel", "parallel")),
        name="mla_kv",
    )(ckv_all, kpe_all, wk_p, wkpe_p, wv_p, k_gain_p)


def _attn_kernel(q_ref, k_ref, v_ref, o_ref, sc_ref, mx_ref, acc_ref, *, tq, tk, s_len, q_pos0):
    qi = pl.program_id(1)
    q_first = q_pos0 + qi * tq
    q_last = q_first + tq - 1
    vis_first = jnp.minimum(s_len, (q_first // CHUNK + 1) * CHUNK)
    vis_last = jnp.minimum(s_len, (q_last // CHUNK + 1) * CHUNK)
    n_full = vis_first // tk
    n_kv = (vis_last + tk - 1) // tk

    q_chunk = _div_pow2(q_first + lax.broadcasted_iota(jnp.int32, (tq, tk), 0), CHUNK)
    k_iota = lax.broadcasted_iota(jnp.int32, (tq, tk), 1)

    for g0 in range(0, D_HEADS, ATTN_HEAD_GROUP):
        heads = [slice(h * HEAD_PAD, (h + 1) * HEAD_PAD) for h in range(g0, g0 + ATTN_HEAD_GROUP)]

        def score_pass(masked, heads=heads):
            def body(j, carry):
                k0 = pl.multiple_of(j * tk, tk)
                visible = (_div_pow2(k0 + k_iota, CHUNK) <= q_chunk) if masked else None
                for h, sl in enumerate(heads):
                    s = _dot_nt(q_ref[:, sl], k_ref[pl.ds(k0, tk), sl])
                    if masked:
                        s = jnp.where(visible, s, NEG_BIG)
                    sc_ref[h, j] = s
                    tile_max = s[:, :LANES]
                    for c in range(LANES, tk, LANES):
                        tile_max = jnp.maximum(tile_max, s[:, c:c + LANES])
                    mx_ref[h] = jnp.maximum(mx_ref[h], tile_max)
                return carry
            return body

        def sum_pass(j, carry, heads=heads):
            k0 = pl.multiple_of(j * tk, tk)
            for h, sl in enumerate(heads):
                m = mx_ref[h]
                p = jnp.exp2(sc_ref[h, j] - jnp.concatenate([m] * (tk // LANES), axis=1))
                acc_ref[h] += _dot(p.astype(BF16), v_ref[pl.ds(k0, tk), sl])
            return carry

        mx_ref[...] = jnp.full_like(mx_ref, NEG_BIG)
        lax.fori_loop(0, n_full, score_pass(False), 0)
        lax.fori_loop(n_full, n_kv, score_pass(True), 0)
        for h in range(ATTN_HEAD_GROUP):
            m = jnp.max(mx_ref[h], axis=-1, keepdims=True)
            mx_ref[h] = jnp.broadcast_to(m, (tq, LANES))
        acc_ref[...] = jnp.zeros_like(acc_ref)
        lax.fori_loop(0, n_kv, sum_pass, 0)
        for h, sl in enumerate(heads):
            acc = acc_ref[h]
            o_ref[:, sl] = (acc / acc[:, V_ONE_LANE:V_ONE_LANE + 1]).astype(o_ref.dtype)


def attention(q, k, v, q_pos0, tq_cap=512, tk_cap=512):
    bsz, tq_len, width = q.shape
    s_len = k.shape[1]
    tq = _row_tile(tq_len, tq_cap)
    tk = _row_tile(s_len, tk_cap)
    assert tk % LANES == 0
    kern = functools.partial(_attn_kernel, tq=tq, tk=tk, s_len=s_len, q_pos0=q_pos0)
    return pl.pallas_call(
        kern,
        grid=(bsz, tq_len // tq),
        in_specs=[
            pl.BlockSpec((None, tq, width), lambda b, i: (b, i, 0)),
            pl.BlockSpec((None, s_len, width), lambda b, i: (b, 0, 0)),
            pl.BlockSpec((None, s_len, width), lambda b, i: (b, 0, 0)),
        ],
        out_specs=pl.BlockSpec((None, tq, width), lambda b, i: (b, i, 0)),
        out_shape=jax.ShapeDtypeStruct((bsz, tq_len, width), BF16),
        scratch_shapes=[pltpu.VMEM((ATTN_HEAD_GROUP, s_len // tk, tq, tk), F32),
                        pltpu.VMEM((ATTN_HEAD_GROUP, tq, LANES), F32),
                        pltpu.VMEM((ATTN_HEAD_GROUP, tq, HEAD_PAD), F32)],
        compiler_params=_cparams(("parallel", "arbitrary")),
        name="attention",
    )(q, k, v)


def _mla_decode_kernel(q_ref, ckv_ref, kpe_ref, wk_ref, wkt_ref, wv_ref, kg_ref, o_ref,
                       *, tq, s_len, q_pos0):
    s_pad = ckv_ref.shape[0]
    heads = [slice(h * HEAD_PAD, (h + 1) * HEAD_PAD) for h in range(D_HEADS)]
    ckv = ckv_ref[...].astype(BF16)
    kpe = kpe_ref[...].astype(BF16)
    eye = (lax.broadcasted_iota(jnp.int32, (LANES, LANES), 0)
           == lax.broadcasted_iota(jnp.int32, (LANES, LANES), 1))
    kpe_t = _dot_nt(jnp.where(eye, 1.0, 0.0).astype(BF16), kpe)
    pe_ss = jnp.sum(kpe_t * kpe_t, axis=0, keepdims=True)
    kn_t = _dot_nt(wkt_ref[...], ckv)
    inv_rms, q_lat, q_g = [], [], []
    for sl in heads:
        blk = kn_t[sl, :]
        ss = jnp.sum(blk * blk, axis=0, keepdims=True) + pe_ss
        inv_rms.append(jnp.broadcast_to(lax.rsqrt(ss * (1.0 / QK_DIM) + EPS), (tq, s_pad)))
        qg = (q_ref[:, sl].astype(F32) * kg_ref[...]).astype(BF16)
        q_g.append(qg)
        q_lat.append(_dot_nt(qg, wk_ref[:, sl]).astype(BF16))
    s = _dot_nt(jnp.concatenate(q_lat, axis=0), ckv) + _dot_nt(jnp.concatenate(q_g, axis=0), kpe)
    s = s * jnp.concatenate(inv_rms, axis=0)
    row = lax.broadcasted_iota(jnp.int32, s.shape, 0)
    col = lax.broadcasted_iota(jnp.int32, s.shape, 1)
    q_chunk = _div_pow2(q_pos0 + _mod_pow2(row, tq), CHUNK)
    visible = (_div_pow2(col, CHUNK) <= q_chunk) & (col < s_len)
    s = jnp.where(visible, s, NEG_BIG)
    p = jnp.exp2(s - jnp.max(s, axis=-1, keepdims=True))
    lat = _dot(p.astype(BF16), ckv) / jnp.sum(p, axis=-1, keepdims=True)
    for h, sl in enumerate(heads):
        o_ref[:, sl] = _dot(lat[h * tq:(h + 1) * tq].astype(BF16), wv_ref[:, sl]).astype(o_ref.dtype)


def mla_decode(q, ckv_all, kpe_all, wk_p, wv_p, k_gain_p, q_pos0):
    bsz, tq, width = q.shape
    s_len = ckv_all.shape[1]
    s_pad = -(-s_len // LANES) * LANES
    assert tq & (tq - 1) == 0
    ckv_pad = jnp.pad(ckv_all, ((0, 0), (0, s_pad - s_len), (0, 0)))
    kpe_pad = jnp.pad(kpe_all, ((0, 0), (0, s_pad - s_len), (NOPE_DIM, HEAD_PAD - QK_DIM)))
    full = lambda r, c: pl.BlockSpec((r, c), lambda b: (0, 0))
    kern = functools.partial(_mla_decode_kernel, tq=tq, s_len=s_len, q_pos0=q_pos0)
    return pl.pallas_call(
        kern,
        grid=(bsz,),
        in_specs=[
            pl.BlockSpec((None, tq, width), lambda b: (b, 0, 0)),
            pl.BlockSpec((None, s_pad, KV_LORA), lambda b: (b, 0, 0)),
            pl.BlockSpec((None, s_pad, HEAD_PAD), lambda b: (b, 0, 0)),
            full(KV_LORA, width), full(width, KV_LORA), full(KV_LORA, width), full(1, HEAD_PAD),
        ],
        out_specs=pl.BlockSpec((None, tq, width), lambda b: (b, 0, 0)),
        out_shape=jax.ShapeDtypeStruct((bsz, tq, width), BF16),
        compiler_params=_cparams(("parallel",)),
        name="mla_decode",
    )(q, ckv_pad, kpe_pad, wk_p, wk_p.T, wv_p, k_gain_p)


def rope_tables(pos):
    half = ROPE_DIM // 2
    inv = ROPE_BASE ** (-jnp.arange(half, dtype=F32) / half)
    ang = pos.astype(F32)[:, None] * inv[None, :]
    cos, sin = jnp.cos(ang), jnp.sin(ang)
    n = pos.shape[0]
    z = lambda w: jnp.zeros((n, w), F32)
    o = lambda w: jnp.ones((n, w), F32)
    tail = LANES - NOPE_DIM - ROPE_DIM
    q_cos = jnp.concatenate([o(NOPE_DIM), cos, cos, z(tail)], axis=1)
    q_sin = jnp.concatenate([z(NOPE_DIM), sin, sin, z(tail)], axis=1)
    k_cos = jnp.concatenate([cos, cos, z(LANES - ROPE_DIM)], axis=1)
    k_s1 = jnp.concatenate([-sin, z(LANES - half)], axis=1)
    k_s2 = jnp.concatenate([z(half), sin, z(LANES - ROPE_DIM)], axis=1)
    return q_cos, q_sin, k_cos, k_s1, k_s2


def pack_mla_weights(w_uq, w_ukv, q_gain, k_gain, cd_w_out):
    pad_q = HEAD_PAD - QK_DIM
    half = ROPE_DIM // 2
    wq = w_uq.reshape(Q_LORA, D_HEADS, QK_DIM)
    wuq_p = jnp.pad(wq, ((0, 0), (0, 0), (0, pad_q))).reshape(Q_LORA, D_HEADS * HEAD_PAD)
    wq_rot = jnp.concatenate([jnp.zeros_like(wq[:, :, :NOPE_DIM]), -wq[:, :, NOPE_DIM + half:],
                              wq[:, :, NOPE_DIM:NOPE_DIM + half], jnp.zeros_like(wq[:, :, :pad_q])], axis=2)
    wuq_p = jnp.concatenate([wuq_p, wq_rot.reshape(Q_LORA, D_HEADS * HEAD_PAD)], axis=1).astype(BF16)
    wkv = w_ukv.reshape(KV_LORA, D_HEADS, NOPE_DIM + V_DIM)
    wk_p = jnp.pad(wkv[:, :, :NOPE_DIM], ((0, 0), (0, 0), (0, HEAD_PAD - NOPE_DIM)))
    wk_p = wk_p.reshape(KV_LORA, D_HEADS * HEAD_PAD).astype(BF16)
    wv_p = jnp.pad(wkv[:, :, NOPE_DIM:], ((0, 0), (0, 0), (0, HEAD_PAD - V_DIM)))
    wv_p = wv_p.reshape(KV_LORA, D_HEADS * HEAD_PAD).astype(BF16)
    place = jnp.pad(jnp.eye(ROPE_DIM, dtype=F32), ((0, 0), (NOPE_DIM, HEAD_PAD - QK_DIM)))
    wkpe_p = jnp.tile(place, (1, D_HEADS)).astype(BF16)
    q_gain_p = jnp.pad(q_gain, (0, pad_q)).reshape(1, HEAD_PAD)
    k_gain_p = jnp.pad(k_gain, (0, pad_q)).reshape(1, HEAD_PAD)
    w_c = cd_w_out[:C_WIDTH].astype(BF16)
    w_d = cd_w_out[C_WIDTH:].reshape(D_HEADS, V_DIM, D_MODEL)
    w_d_p = jnp.pad(w_d, ((0, 0), (0, HEAD_PAD - V_DIM), (0, 0))).reshape(D_HEADS * HEAD_PAD, D_MODEL)
    return wuq_p, wk_p, wkpe_p, wv_p, q_gain_p, k_gain_p, w_c, w_d_p.astype(BF16)


def _trunk(x, pos0, w, past):
    bsz, t, _ = x.shape
    n = bsz * t
    x2 = x.reshape(n, D_MODEL)
    pos = pos0 + jnp.arange(t, dtype=jnp.int32)
    lb_all = jnp.cumsum(jax.nn.softmax(w['hgrn_lb_logits'], axis=0), axis=0)
    outs = {}
    ffn_w = lambda name, l: (w[name + '_norm'][l], w[name + '_w_gate'][l], w[name + '_w_up'][l],
                             w[name + '_w_down'][l])
    for l in range(2):
        w_in = w['ab_w_in'] if l == 0 else w['cd_w_in_p']
        x2, proj = ffn(x2, *ffn_w('ffn1', l), post=(w['mix_norm'][l], w_in))
        proj = proj.reshape(bsz, t, w_in.shape[1])
        if l == 0:
            s0 = None if past is None else past[0][0]
            a_out, s_t = hgrn2(proj, lb_all[0], w['hgrn_out_norm'][0], s0)
            if past is None:
                x0 = jnp.zeros((bsz, S5_STATE_W), F32)
            else:
                x0 = s5_pack_state(past[1][0], past[2][0])
            u_tb = jnp.transpose(proj[:, :, AB_IN - B_WIDTH:], (1, 0, 2)).astype(F32)
            b_tb, x_t = s5_mix(u_tb, w['s5_wb'], w['s5_wc'], w['s5_lam'], w['s5_d'],
                               w['s5_w_glu'], w['s5_b_glu'], x0)
            b_out = jnp.transpose(b_tb, (1, 0, 2))
            pre = (a_out.reshape(n, A_WIDTH), b_out.reshape(n, B_WIDTH),
                   w['ab_w_out_a'], w['ab_w_out_b'])
            outs['hgrn'] = s_t
            outs['s5re'], outs['s5im'] = s5_unpack_state(x_t)
        else:
            c_out, v_rows = gmlp(proj, w['gmlp_v_norm'], w['gmlp_w_s'], w['gmlp_b_s'])
            q, ckv, kpe = mla_q(proj, rope_tables(pos), w['mla_q_norm'], w['mla_kv_norm'],
                                w['mla_wuq_p'], w['mla_q_gain_p'])
            if past is None:
                k, v = mla_kv(ckv, kpe, w['mla_wk_p'], w['mla_wkpe_p'], w['mla_wv_p'],
                              w['mla_k_gain_p'])
                d_out = attention(q, k, v, pos0)
            else:
                ckv_all = jnp.concatenate([past[3][0], ckv], axis=1)
                kpe_all = jnp.concatenate([past[4][0], kpe], axis=1)
                d_out = mla_decode(q, ckv_all, kpe_all, w['mla_wk_p'], w['mla_wv_p'],
                                   w['mla_k_gain_p'], pos0)
            pre = (c_out.reshape(n, C_WIDTH), d_out.reshape(n, D_HEADS * HEAD_PAD),
                   w['cd_w_out_c'], w['cd_w_out_d'])
            outs['gv'], outs['ckv'], outs['kpe'] = v_rows, ckv, kpe
        x2 = ffn(x2, *ffn_w('ffn2', l), pre=pre)
    return x2.reshape(bsz, t, D_MODEL), outs


def kernel(x_prompt, x_sample, state_hgrn, state_s5_re, state_s5_im, cache_mla_ckv, cache_mla_kpe, ffn1_norm, ffn1_w_gate, ffn1_w_up, ffn1_w_down, mix_norm, ffn2_norm, ffn2_w_gate, ffn2_w_up, ffn2_w_down, ab_w_in, hgrn_lb_logits, hgrn_out_norm, s5_lambda_re, s5_lambda_im, s5_log_dt, s5_b_re, s5_b_im, s5_c_re, s5_c_im, s5_d, s5_w_glu, s5_b_glu, ab_w_out, cd_w_in, gmlp_v_norm, gmlp_w_s, gmlp_b_s, mla_q_norm, mla_w_uq, mla_kv_norm, mla_w_ukv, mla_q_gain, mla_k_gain, cd_w_out):
    bf = lambda a: a.astype(BF16)
    wb, wc, lam_rows = s5_prepare(s5_lambda_re[0], s5_lambda_im[0], s5_log_dt[0],
                                  s5_b_re[0], s5_b_im[0], s5_c_re[0], s5_c_im[0])
    wuq_p, wk_p, wkpe_p, wv_p, q_gain_p, k_gain_p, w_c, w_d_p = pack_mla_weights(
        mla_w_uq[0], mla_w_ukv[0], mla_q_gain[0], mla_k_gain[0], cd_w_out[0])
    w = dict(
        ffn1_norm=ffn1_norm, ffn1_w_gate=bf(ffn1_w_gate), ffn1_w_up=bf(ffn1_w_up), ffn1_w_down=bf(ffn1_w_down),
        ffn2_norm=ffn2_norm, ffn2_w_gate=bf(ffn2_w_gate), ffn2_w_up=bf(ffn2_w_up), ffn2_w_down=bf(ffn2_w_down),
        mix_norm=mix_norm, ab_w_in=bf(ab_w_in[0]), hgrn_lb_logits=hgrn_lb_logits, hgrn_out_norm=hgrn_out_norm,
        s5_wb=wb, s5_wc=wc, s5_lam=lam_rows, s5_d=s5_d[0].reshape(B_WIDTH), s5_w_glu=bf(s5_w_glu[0]),
        s5_b_glu=s5_b_glu[0], ab_w_out_a=bf(ab_w_out[0, :A_WIDTH]), ab_w_out_b=bf(ab_w_out[0, A_WIDTH:]),
        cd_w_in_p=bf(jnp.pad(cd_w_in[0], ((0, 0), (0, CD_IN_PAD - CD_IN)))),
        gmlp_v_norm=gmlp_v_norm[0], gmlp_w_s=gmlp_w_s[0], gmlp_b_s=gmlp_b_s[0],
        mla_q_norm=mla_q_norm[0], mla_kv_norm=mla_kv_norm[0], mla_wuq_p=wuq_p, mla_q_gain_p=q_gain_p,
        mla_wk_p=wk_p, mla_wkpe_p=wkpe_p, mla_wv_p=wv_p, mla_k_gain_p=k_gain_p,
        cd_w_out_c=w_c, cd_w_out_d=w_d_p,
    )
    past_len = cache_mla_ckv.shape[2]
    y_p, o_p = _trunk(x_prompt, 0, w, None)
    y_s, o_s = _trunk(x_sample, past_len, w,
                      (state_hgrn, state_s5_re, state_s5_im, cache_mla_ckv, cache_mla_kpe))
    e = lambda a: a[None]
    return (y_p, y_s, e(o_p['hgrn']), e(o_s['hgrn']), e(o_p['s5re']), e(o_p['s5im']),
            e(o_s['s5re']), e(o_s['s5im']), e(o_p['gv']), e(o_s['gv']),
            e(o_p['ckv']), e(o_p['kpe']), e(o_s['ckv']), e(o_s['kpe']))
```

```python
import functools
import math

import jax
import jax.numpy as jnp
from jax import lax
from jax.experimental import pallas as pl
from jax.experimental.pallas import tpu as pltpu

D_MODEL = 1024
D_FF = 2816
EPS = 1e-6
A_HEADS = 4
A_KEY = 128
A_VAL = 128
A_WIDTH = A_HEADS * A_VAL
HGRN_BLOCK = 64
B_GROUP_DIM = 16
B_WIDTH = 512
B_GROUPS = 32
B_STATE = 64
LAMBDA_RE_CEIL = -1e-4
C_CHUNK = 128
C_WIDTH = 512
C_GROUPS = 4
C_GROUP_DIM = 128
D_HEADS = 8
Q_LORA = 256
KV_LORA = 128
NOPE_DIM = 64
ROPE_DIM = 32
V_DIM = 64
QK_DIM = NOPE_DIM + ROPE_DIM
ROPE_BASE = 10000.0
CHUNK = 64
A_QK = A_HEADS * A_KEY
AB_IN = 2 * A_QK + 2 * A_WIDTH + B_WIDTH
CD_IN = 2 * C_WIDTH + Q_LORA + KV_LORA + ROPE_DIM
CD_IN_PAD = 1536

LANES = 128
SUBLANES = 8
VMEM_LIMIT = 56 * 1024 * 1024
HGRN_SUB = 16
HEAD_PAD = LANES
NEG_BIG = -1e30
V_ONE_LANE = V_DIM
ATTN_HEAD_GROUP = 4

BF16 = jnp.bfloat16
F32 = jnp.float32


def _cparams(sem):
    return pltpu.CompilerParams(dimension_semantics=sem, vmem_limit_bytes=VMEM_LIMIT)


def _rms(xf, width):
    return xf * lax.rsqrt(jnp.sum(xf * xf, axis=-1, keepdims=True) * (1.0 / width) + EPS)


def _dot(a, b):
    return jnp.dot(a, b, preferred_element_type=F32)


def _dot_nt(a, b):
    return lax.dot_general(a, b, (((1,), (1,)), ((), ())), preferred_element_type=F32)


def _dot_tn(a, b):
    return lax.dot_general(a, b, (((0,), (0,)), ((), ())), preferred_element_type=F32)


def _div_pow2(x, d):
    assert d & (d - 1) == 0
    return lax.shift_right_logical(x, jnp.int32(d.bit_length() - 1))


def _mod_pow2(x, d):
    assert d & (d - 1) == 0
    return x & jnp.int32(d - 1)


def _row_tile(n, cap):
    t = min(n, cap)
    assert n % t == 0, (n, t)
    return t


FFN_CHUNK = 256
POST_CHUNK = 512
FFN_ROWS = 512


def _ffn_kernel(*refs, has_pre, has_post, split_tail):
    refs = list(refs)
    x_ref = refs.pop(0)
    if has_pre:
        a1_ref, a2_ref, w1_ref, w2_ref = refs[:4]
        refs = refs[4:]
    g_ref, wg_ref, wu_ref, wd_ref = refs[:4]
    refs = refs[4:]
    if has_post:
        g2_ref, wp_ref = refs[:2]
        refs = refs[2:]
    o_ref = refs.pop(0)
    if has_post:
        p_ref = refs.pop(0)
    if split_tail:
        tail_ref = refs.pop(0)
    acc_ref = refs.pop(0)

    x = x_ref[...]
    if has_pre:
        x = x + _dot(a1_ref[...], w1_ref[...]) + _dot(a2_ref[...], w2_ref[...])
    h = (_rms(x, D_MODEL) * g_ref[...]).astype(BF16)
    for j in range(D_FF // FFN_CHUNK):
        cs = slice(j * FFN_CHUNK, (j + 1) * FFN_CHUNK)
        gate = _dot(h, wg_ref[:, cs])
        up = _dot(h, wu_ref[:, cs])
        act = (gate * jax.nn.sigmoid(gate) * up).astype(BF16)
        part = _dot(act, wd_ref[cs, :])
        if j == 0:
            acc_ref[...] = part
        else:
            acc_ref[...] += part
    out = x + 0.5 * acc_ref[...]
    o_ref[...] = out
    if has_post:
        h2 = (_rms(out, D_MODEL) * g2_ref[...]).astype(BF16)
        n_post = wp_ref.shape[1] // POST_CHUNK
        for c in range(n_post):
            cs = slice(c * POST_CHUNK, (c + 1) * POST_CHUNK)
            part = _dot(h2, wp_ref[:, cs]).astype(p_ref.dtype)
            if split_tail and c == n_post - 1:
                tail_ref[...] = part
            else:
                p_ref[:, cs] = part


def ffn(x, g, wg, wu, wd, pre=None, post=None, time_major=None, tm_cap=FFN_ROWS):
    n = x.shape[0]
    tm = _row_tile(n, tm_cap)
    rows = lambda w: pl.BlockSpec((tm, w), lambda i: (i, 0))
    fixed = lambda r, c: pl.BlockSpec((r, c), lambda i: (0, 0), pipeline_mode=pl.Buffered(1))
    if time_major is not None:
        streams, steps = time_major
        assert steps % tm == 0 and streams * steps == n
        per_stream = steps // tm
        by_time = lambda w: pl.BlockSpec((tm, w), lambda i: (i % per_stream, i // per_stream))
    args, specs = [x], [rows(D_MODEL)]
    if pre is not None:
        a1, a2, w1, w2 = pre
        args += [a1, a2, w1, w2]
        a2_spec = by_time(w2.shape[0]) if time_major is not None else rows(a2.shape[1])
        specs += [rows(a1.shape[1]), a2_spec, fixed(*w1.shape), fixed(*w2.shape)]
    args += [g.reshape(1, D_MODEL), wg, wu, wd]
    specs += [fixed(1, D_MODEL), fixed(D_MODEL, D_FF), fixed(D_MODEL, D_FF), fixed(D_FF, D_MODEL)]
    out_specs = [rows(D_MODEL)]
    out_shape = [jax.ShapeDtypeStruct((n, D_MODEL), F32)]
    split_tail = post is not None and time_major is not None
    if post is not None:
        g2, wp = post
        assert wp.shape[1] % POST_CHUNK == 0
        args += [g2.reshape(1, D_MODEL), wp]
        specs += [fixed(1, D_MODEL), fixed(*wp.shape)]
        head_w = wp.shape[1] - (POST_CHUNK if split_tail else 0)
        out_specs.append(rows(head_w))
        out_shape.append(jax.ShapeDtypeStruct((n, head_w), BF16))
        if split_tail:
            out_specs.append(by_time(POST_CHUNK))
            out_shape.append(jax.ShapeDtypeStruct((steps, streams * POST_CHUNK), BF16))
    kern = functools.partial(_ffn_kernel, has_pre=pre is not None, has_post=post is not None,
                             split_tail=split_tail)
    res = pl.pallas_call(
        kern,
        grid=(n // tm,),
        in_specs=specs,
        out_specs=out_specs,
        out_shape=out_shape,
        scratch_shapes=[pltpu.VMEM((tm, D_MODEL), F32)],
        compiler_params=_cparams(("parallel",)),
        name="ffn",
    )(*args)
    return res if post is not None else res[0]


def _hgrn_kernel(q_ref, f_ref, i_ref, g_ref, lb_ref, og_ref, s0_ref, y_ref, st_ref, state_ref,
                 *, tt, blk, has_past):
    t_idx = pl.program_id(1)
    n_chunks = tt // blk
    n_sub = blk // HGRN_SUB

    @pl.when(t_idx == 0)
    def _():
        if has_past:
            for h in range(A_HEADS):
                state_ref[h] = s0_ref[h].T
        else:
            state_ref[...] = jnp.zeros_like(state_ref)

    row = lax.broadcasted_iota(jnp.int32, (tt, tt), 0)
    col = lax.broadcasted_iota(jnp.int32, (tt, tt), 1)
    same_chunk = _div_pow2(row, blk) == _div_pow2(col, blk)
    causal = same_chunk & (col <= row)
    tril = jnp.where(causal, 1.0, 0.0).astype(BF16)
    heads = [slice(h * LANES, (h + 1) * LANES) for h in range(A_HEADS)]

    q = q_ref[...].astype(F32)
    v_bf = i_ref[...].astype(BF16)
    lb = lb_ref[...]
    f = lb + (1.0 - lb) * jax.nn.sigmoid(f_ref[...].astype(F32))
    k = 1.0 - f
    logf = jnp.log(f)
    hi = logf.astype(BF16)
    mid = (logf - hi.astype(F32)).astype(BF16)
    b = _dot(tril, hi) + _dot(tril, mid)

    q_slab, k_slab, ends = [], [], []
    for c in range(n_chunks):
        rows = [slice(c * blk + i * HGRN_SUB, c * blk + (i + 1) * HGRN_SUB) for i in range(n_sub)]
        mids = [b[r.start + HGRN_SUB // 2 - 1:r.start + HGRN_SUB // 2, :] for r in rows]
        q_slab.append([(q[r] * jnp.exp(b[r] - mids[i])).astype(BF16) for i, r in enumerate(rows)])
        k_slab.append([[(k[r] * jnp.exp(mids[i] - b[r])).astype(BF16) if j <= i else None
                        for i in range(n_sub)] for j, r in enumerate(rows)])
        ends.append(b[(c + 1) * blk - 1:(c + 1) * blk, :])
    b_end_rows = (jnp.broadcast_to(ends[0], (blk, A_WIDTH)) if n_chunks == 1 else
                  jnp.concatenate([jnp.broadcast_to(e, (blk, A_WIDTH)) for e in ends], axis=0))
    q_e = (q * jnp.exp(b)).astype(BF16)
    k_d = (k * jnp.exp(b_end_rows - b)).astype(BF16)
    zero_slab = jnp.zeros((HGRN_SUB, LANES), BF16)
    zero_blk = jnp.zeros((blk, LANES), BF16)
    chunk_rows = [slice(c * blk, (c + 1) * blk) for c in range(n_chunks)]

    outs = []
    for h, sl in enumerate(heads):
        q_hat = jnp.concatenate(
            [jnp.concatenate([q_slab[c][j][:, sl] if i == j else zero_slab for i in range(n_sub)], axis=1)
             for c in range(n_chunks) for j in range(n_sub)], axis=0)
        k_hat = jnp.concatenate(
            [jnp.concatenate([k_slab[c][j][i][:, sl] if j <= i else zero_slab for i in range(n_sub)], axis=1)
             for c in range(n_chunks) for j in range(n_sub)], axis=0)
        att = _dot_nt(q_hat, k_hat)
        att = jnp.where(causal, att, 0.0).astype(BF16)
        spread = lambda a: jnp.concatenate(
            [jnp.concatenate([a[rs, sl] if c == d else zero_blk for d in range(n_chunks)], axis=1)
             for c, rs in enumerate(chunk_rows)], axis=0)
        kv = _dot_tn(spread(v_bf), k_d[:, sl])
        s_t = state_ref[h]
        states = []
        for c in range(n_chunks):
            states.append(s_t.astype(BF16))
            s_t = s_t * jnp.exp(ends[c][:, sl]) + kv[c * A_VAL:(c + 1) * A_VAL]
        state_ref[h] = s_t
        o = _dot(att, v_bf[:, sl]) + _dot_nt(spread(q_e), jnp.concatenate(states, axis=1))
        outs.append(_rms(o, A_VAL) * og_ref[...])
    y = jnp.concatenate(outs, axis=1) * jax.nn.sigmoid(g_ref[...].astype(F32))
    y_ref[...] = y.astype(y_ref.dtype)

    @pl.when(t_idx == pl.num_programs(1) - 1)
    def _():
        for h in range(A_HEADS):
            st_ref[h] = state_ref[h].T


def hgrn2(proj, lb, out_g, s0, tt_cap=256):
    bsz, t, _ = proj.shape
    blk = min(HGRN_BLOCK, t)
    tt = _row_tile(t, tt_cap)
    has_past = s0 is not None
    if s0 is None:
        s0 = jnp.zeros((bsz, A_HEADS, A_KEY, A_VAL), F32)
    seg = lambda s: pl.BlockSpec((None, tt, A_WIDTH), lambda b, i, s=s: (b, i, s))
    kern = functools.partial(_hgrn_kernel, tt=tt, blk=blk, has_past=has_past)
    return pl.pallas_call(
        kern,
        grid=(bsz, t // tt),
        in_specs=[
            seg(0), seg(1), seg(2), seg(3),
            pl.BlockSpec((1, A_QK), lambda b, i: (0, 0)),
            pl.BlockSpec((1, A_VAL), lambda b, i: (0, 0)),
            pl.BlockSpec((None, A_HEADS, A_KEY, A_VAL), lambda b, i: (b, 0, 0, 0)),
        ],
        out_specs=[
            pl.BlockSpec((None, tt, A_WIDTH), lambda b, i: (b, i, 0)),
            pl.BlockSpec((None, A_HEADS, A_KEY, A_VAL), lambda b, i: (b, 0, 0, 0)),
        ],
        out_shape=[
            jax.ShapeDtypeStruct((bsz, t, A_WIDTH), BF16),
            jax.ShapeDtypeStruct((bsz, A_HEADS, A_KEY, A_VAL), F32),
        ],
        scratch_shapes=[pltpu.VMEM((A_HEADS, A_VAL, A_KEY), F32)],
        compiler_params=_cparams(("parallel", "arbitrary")),
        name="hgrn2",
    )(proj, proj, proj, proj, lb.reshape(1, A_QK), out_g.reshape(1, A_VAL), s0)


S5_SUPER = 4
S5_SUPER_U = B_WIDTH // S5_SUPER
S5_SUPER_X = 2 * (B_GROUPS // S5_SUPER) * B_STATE
S5_HALF = S5_SUPER_X // 2
S5_STATE_W = S5_SUPER * S5_SUPER_X
S5_SCAN_GROUP = 2


def _s5_kernel(u_ref, wb_ref, wc_ref, lam_ref, d_ref, wglu_ref, bglu_ref, x0_ref,
               o_ref, xt_ref, xbuf_ref, state_ref, *, tt):
    t_idx = pl.program_id(1)
    rows = tt * SUBLANES

    @pl.when(t_idx == 0)
    def _():
        state_ref[...] = x0_ref[...]

    u_bf = u_ref[...].reshape(rows, B_WIDTH)
    u = u_bf.astype(F32)
    for j in range(S5_SUPER):
        xbuf_ref[:, j * S5_SUPER_X:(j + 1) * S5_SUPER_X] = _dot(
            u_bf[:, j * S5_SUPER_U:(j + 1) * S5_SUPER_U], wb_ref[j])

    for j0 in range(0, S5_SUPER, S5_SCAN_GROUP):
        cols, lams = [], []
        for j in range(j0, j0 + S5_SCAN_GROUP):
            re_cols = slice(j * S5_SUPER_X, j * S5_SUPER_X + S5_HALF)
            im_cols = slice(j * S5_SUPER_X + S5_HALF, (j + 1) * S5_SUPER_X)
            cols.append((re_cols, im_cols))
            lams.append((jnp.broadcast_to(lam_ref[0:1, re_cols], (SUBLANES, S5_HALF)),
                         jnp.broadcast_to(lam_ref[1:2, re_cols], (SUBLANES, S5_HALF))))

        def step(t, carry, cols=cols, lams=lams):
            rows_t = pl.ds(pl.multiple_of(t * SUBLANES, SUBLANES), SUBLANES)
            out = []
            for (re_cols, im_cols), (lam_re, lam_im), (x_re, x_im) in zip(cols, lams, carry):
                n_re = lam_re * x_re - lam_im * x_im + xbuf_ref[rows_t, re_cols]
                n_im = lam_re * x_im + lam_im * x_re + xbuf_ref[rows_t, im_cols]
                xbuf_ref[rows_t, re_cols] = n_re
                xbuf_ref[rows_t, im_cols] = n_im
                out.append((n_re, n_im))
            return tuple(out)

        init = tuple((state_ref[:, rc], state_ref[:, ic]) for rc, ic in cols)
        last = lax.fori_loop(0, tt, step, init, unroll=True)
        for (rc, ic), (x_re, x_im) in zip(cols, last):
            state_ref[:, rc] = x_re
            state_ref[:, ic] = x_im

    ys = []
    for j in range(S5_SUPER):
        xs = xbuf_ref[:, j * S5_SUPER_X:(j + 1) * S5_SUPER_X].astype(BF16)
        ys.append(_dot(xs, wc_ref[j]))
    y = jnp.concatenate(ys, axis=1) + d_ref[...] * u
    z = jax.nn.gelu(y)
    gate = jax.nn.sigmoid(_dot(z.astype(BF16), wglu_ref[...]) + bglu_ref[...])
    o_ref[...] = (z * gate).astype(o_ref.dtype).reshape(tt, SUBLANES, B_WIDTH)

    @pl.when(t_idx == pl.num_programs(1) - 1)
    def _():
        xt_ref[...] = state_ref[...]


def s5_mix(u_tb, wb, wc, lam, d, w_glu, b_glu, x0, tt_cap=64):
    t, bsz, _ = u_tb.shape
    tt = _row_tile(t, tt_cap)
    kern = functools.partial(_s5_kernel, tt=tt)
    return pl.pallas_call(
        kern,
        grid=(bsz // SUBLANES, t // tt),
        in_specs=[
            pl.BlockSpec((tt, SUBLANES, B_WIDTH), lambda b, i: (i, b, 0)),
            pl.BlockSpec((S5_SUPER, S5_SUPER_U, S5_SUPER_X), lambda b, i: (0, 0, 0)),
            pl.BlockSpec((S5_SUPER, S5_SUPER_X, S5_SUPER_U), lambda b, i: (0, 0, 0)),
            pl.BlockSpec((2, S5_STATE_W), lambda b, i: (0, 0)),
            pl.BlockSpec((1, B_WIDTH), lambda b, i: (0, 0)),
            pl.BlockSpec((B_WIDTH, B_WIDTH), lambda b, i: (0, 0)),
            pl.BlockSpec((1, B_WIDTH), lambda b, i: (0, 0)),
            pl.BlockSpec((SUBLANES, S5_STATE_W), lambda b, i: (b, 0)),
        ],
        out_specs=[
            pl.BlockSpec((tt, SUBLANES, B_WIDTH), lambda b, i: (i, b, 0)),
            pl.BlockSpec((SUBLANES, S5_STATE_W), lambda b, i: (b, 0)),
        ],
        out_shape=[
            jax.ShapeDtypeStruct((t, bsz, B_WIDTH), BF16),
            jax.ShapeDtypeStruct((bsz, S5_STATE_W), F32),
        ],
        scratch_shapes=[pltpu.VMEM((tt * SUBLANES, S5_STATE_W), F32),
                        pltpu.VMEM((SUBLANES, S5_STATE_W), F32)],
        compiler_params=_cparams(("parallel", "arbitrary")),
        name="s5_mix",
    )(u_tb, wb, wc, lam, d.reshape(1, B_WIDTH), w_glu, b_glu.reshape(1, B_WIDTH), x0)


def _s5_pack_cols(a):
    return a.reshape(a.shape[:-2] + (S5_SUPER, S5_HALF))


def s5_prepare(lam_re, lam_im, log_dt, b_re, b_im, c_re, c_im):
    lr = jnp.minimum(lam_re, LAMBDA_RE_CEIL)
    li = lam_im
    dt = jnp.exp(log_dt)[:, None]
    mag = jnp.exp(lr * dt)
    bar_re = mag * jnp.cos(li * dt)
    bar_im = mag * jnp.sin(li * dt)
    den = lr * lr + li * li
    coef_re = (((bar_re - 1.0) * lr + bar_im * li) / den)[:, :, None]
    coef_im = ((bar_im * lr - (bar_re - 1.0) * li) / den)[:, :, None]
    bb_re = coef_re * b_re - coef_im * b_im
    bb_im = coef_re * b_im + coef_im * b_re
    gps = B_GROUPS // S5_SUPER
    eye = jnp.eye(gps, dtype=F32)

    def in_block(bm):
        bm = bm.reshape(S5_SUPER, gps, B_STATE, B_GROUP_DIM)
        blk = jnp.einsum('jgph,gk->jghkp', bm, eye)
        return blk.reshape(S5_SUPER, gps * B_GROUP_DIM, gps * B_STATE)

    def out_block(cm):
        cm = cm.reshape(S5_SUPER, gps, B_GROUP_DIM, B_STATE)
        blk = jnp.einsum('jghp,gk->jkpgh', cm, eye)
        return blk.reshape(S5_SUPER, gps * B_STATE, gps * B_GROUP_DIM)

    wb = jnp.concatenate([in_block(bb_re), in_block(bb_im)], axis=2).astype(BF16)
    wc = jnp.concatenate([out_block(c_re), out_block(-c_im)], axis=1).astype(BF16)
    lam_rows = jnp.stack([
        jnp.concatenate([_s5_pack_cols(bar_re)] * 2, axis=-1).reshape(S5_STATE_W),
        jnp.concatenate([_s5_pack_cols(bar_im)] * 2, axis=-1).reshape(S5_STATE_W)])
    return wb, wc, lam_rows


def s5_pack_state(x_re, x_im):
    bsz = x_re.shape[0]
    return jnp.concatenate([_s5_pack_cols(x_re), _s5_pack_cols(x_im)], axis=-1).reshape(bsz, S5_STATE_W)


def s5_unpack_state(x):
    bsz = x.shape[0]
    x = x.reshape(bsz, S5_SUPER, 2, S5_HALF)
    return (x[:, :, 0].reshape(bsz, B_GROUPS, B_STATE), x[:, :, 1].reshape(bsz, B_GROUPS, B_STATE))


def _gmlp_kernel(u_ref, v_ref, gn_ref, ws_ref, bs_ref, o_ref, vr_ref, *, tt, chunk):
    n_chunks = tt // chunk
    u = jax.nn.gelu(u_ref[...].astype(F32))
    v = jax.nn.gelu(v_ref[...].astype(F32))
    vn = _rms(v, C_WIDTH) * gn_ref[...]
    vn_bf = vn.astype(BF16)
    row = lax.broadcasted_iota(jnp.int32, (chunk, chunk), 0)
    col = lax.broadcasted_iota(jnp.int32, (chunk, chunk), 1)
    keep = col <= row
    for g in range(C_GROUPS):
        w = jnp.where(keep, ws_ref[g], 0.0).astype(BF16)
        bias = bs_ref[:, g:g + 1]
        cs = slice(g * C_GROUP_DIM, (g + 1) * C_GROUP_DIM)
        for c in range(n_chunks):
            rs = slice(c * chunk, (c + 1) * chunk)
            s = _dot(w, vn_bf[rs, cs]) + bias
            o_ref[rs, cs] = (u[rs, cs] * s).astype(o_ref.dtype)

    @pl.when(pl.program_id(1) == pl.num_programs(1) - 1)
    def _():
        vr_ref[...] = vn[tt - chunk:, :]


def gmlp(proj, v_norm, w_s, b_s, tt_cap=512):
    bsz, t, _ = proj.shape
    chunk = min(t, C_CHUNK)
    tt = _row_tile(t, tt_cap)
    ws = w_s[:, :chunk, :chunk]
    bs_t = b_s[:, :chunk].T
    kern = functools.partial(_gmlp_kernel, tt=tt, chunk=chunk)
    return pl.pallas_call(
        kern,
        grid=(bsz, t // tt),
        in_specs=[
            pl.BlockSpec((None, tt, C_WIDTH), lambda b, i: (b, i, 0)),
            pl.BlockSpec((None, tt, C_WIDTH), lambda b, i: (b, i, 1)),
            pl.BlockSpec((1, C_WIDTH), lambda b, i: (0, 0)),
            pl.BlockSpec((C_GROUPS, chunk, chunk), lambda b, i: (0, 0, 0)),
            pl.BlockSpec((chunk, C_GROUPS), lambda b, i: (0, 0)),
        ],
        out_specs=[
            pl.BlockSpec((None, tt, C_WIDTH), lambda b, i: (b, i, 0)),
            pl.BlockSpec((None, chunk, C_WIDTH), lambda b, i: (b, 0, 0)),
        ],
        out_shape=[
            jax.ShapeDtypeStruct((bsz, t, C_WIDTH), BF16),
            jax.ShapeDtypeStruct((bsz, chunk, C_WIDTH), F32),
        ],
        compiler_params=_cparams(("parallel", "arbitrary")),
        name="gmlp",
    )(proj, proj, v_norm.reshape(1, C_WIDTH), ws, bs_t)


def _rope_lanes(x, cos_t, sin_lo, sin_hi):
    half = ROPE_DIM // 2
    return (x * cos_t + pltpu.roll(x, LANES - half, axis=1) * sin_lo
            + pltpu.roll(x, half, axis=1) * sin_hi)


def _mla_q_kernel(cq_ref, ckv_ref, kpe_ref, qn_ref, kvn_ref, wuq_ref, qg_ref,
                  qc_ref, qs_ref, kc_ref, ks1_ref, ks2_ref,
                  q_ref, ckv_o_ref, kpe_o_ref, ckpe_ref):
    cqn = (_rms(cq_ref[...].astype(F32), Q_LORA) * qn_ref[...]).astype(BF16)
    width = D_HEADS * HEAD_PAD
    q_all = _dot(cqn, wuq_ref[...])
    qc, qs = qc_ref[...], qs_ref[...]
    scale = QK_DIM ** -0.5 * math.log2(math.e)
    for h in range(D_HEADS):
        sl = slice(h * HEAD_PAD, (h + 1) * HEAD_PAD)
        rot = slice(width + h * HEAD_PAD, width + (h + 1) * HEAD_PAD)
        qh = q_all[:, sl] * qc + q_all[:, rot] * qs
        qh = _rms(qh, QK_DIM) * (qg_ref[...] * scale)
        q_ref[:, sl] = qh.astype(q_ref.dtype)
    ckv = _rms(ckv_ref[...].astype(F32), KV_LORA) * kvn_ref[...]
    kpe = _rope_lanes(kpe_ref[...].astype(F32), kc_ref[...], ks1_ref[...], ks2_ref[...])
    ckv_o_ref[...] = ckv
    kpe_o_ref[...] = kpe[:, :ROPE_DIM]
    ckpe_ref[:, :KV_LORA] = ckv.astype(BF16)
    ckpe_ref[:, KV_LORA:] = kpe.astype(BF16)


def mla_q(proj, tables, q_norm, kv_norm, wuq_p, q_gain_p, tm_cap=512):
    bsz, t, _ = proj.shape
    tm = _row_tile(t, tm_cap)
    tab = pl.BlockSpec((tm, LANES), lambda b, i: (i, 0))
    row = lambda w: pl.BlockSpec((1, w), lambda b, i: (0, 0))
    return pl.pallas_call(
        _mla_q_kernel,
        grid=(bsz, t // tm),
        in_specs=[
            pl.BlockSpec((None, tm, Q_LORA), lambda b, i: (b, i, 2 * C_WIDTH // Q_LORA)),
            pl.BlockSpec((None, tm, KV_LORA), lambda b, i: (b, i, (2 * C_WIDTH + Q_LORA) // KV_LORA)),
            pl.BlockSpec((None, tm, LANES), lambda b, i: (b, i, (2 * C_WIDTH + Q_LORA + KV_LORA) // LANES)),
            row(Q_LORA), row(KV_LORA),
            pl.BlockSpec((Q_LORA, 2 * D_HEADS * HEAD_PAD), lambda b, i: (0, 0)),
            row(HEAD_PAD),
            tab, tab, tab, tab, tab,
        ],
        out_specs=[
            pl.BlockSpec((None, tm, D_HEADS * HEAD_PAD), lambda b, i: (b, i, 0)),
            pl.BlockSpec((None, tm, KV_LORA), lambda b, i: (b, i, 0)),
            pl.BlockSpec((None, tm, ROPE_DIM), lambda b, i: (b, i, 0)),
            pl.BlockSpec((None, tm, KV_LORA + LANES), lambda b, i: (b, i, 0)),
        ],
        out_shape=[
            jax.ShapeDtypeStruct((bsz, t, D_HEADS * HEAD_PAD), BF16),
            jax.ShapeDtypeStruct((bsz, t, KV_LORA), F32),
            jax.ShapeDtypeStruct((bsz, t, ROPE_DIM), F32),
            jax.ShapeDtypeStruct((bsz, t, KV_LORA + LANES), BF16),
        ],
        compiler_params=_cparams(("parallel", "parallel")),
        name="mla_q",
    )(proj, proj, proj, q_norm.reshape(1, Q_LORA), kv_norm.reshape(1, KV_LORA), wuq_p,
      q_gain_p, *tables)


def _mla_kv_kernel(ckpe_ref, wk_ref, wv_ref, kg_ref, k_ref, v_ref):
    ckpe = ckpe_ref[...]
    k_all = _dot(ckpe, wk_ref[...])
    for h in range(D_HEADS):
        sl = slice(h * HEAD_PAD, (h + 1) * HEAD_PAD)
        k_ref[:, sl] = (_rms(k_all[:, sl], QK_DIM) * kg_ref[...]).astype(k_ref.dtype)
    lane = lax.broadcasted_iota(jnp.int32, (1, D_HEADS * HEAD_PAD), 1)
    ones_col = jnp.where(_mod_pow2(lane, HEAD_PAD) == V_ONE_LANE, 1.0, 0.0)
    v_ref[...] = (_dot(ckpe[:, :KV_LORA], wv_ref[...]) + ones_col).astype(v_ref.dtype)


def mla_kv(ckpe, wk_full, wv_p, k_gain_p, tm_cap=512):
    bsz, s, kw = ckpe.shape
    tm = _row_tile(s, tm_cap)
    width = D_HEADS * HEAD_PAD
    full = lambda r, c: pl.BlockSpec((r, c), lambda b, i: (0, 0))
    return pl.pallas_call(
        _mla_kv_kernel,
        grid=(bsz, s // tm),
        in_specs=[
            pl.BlockSpec((None, tm, kw), lambda b, i: (b, i, 0)),
            full(kw, width), full(KV_LORA, width), full(1, HEAD_PAD),
        ],
        out_specs=[
            pl.BlockSpec((None, tm, width), lambda b, i: (b, i, 0)),
            pl.BlockSpec((None, tm, width), lambda b, i: (b, i, 0)),
        ],
        out_shape=[
            jax.ShapeDtypeStruct((bsz, s, width), BF16),
            jax.ShapeDtypeStruct((bsz, s, width), BF16),
        ],
        compiler_params=_cparams(("parallel", "parallel")),
        name="mla_kv",
    )(ckpe, wk_full, wv_p, k_gain_p)


def _attn_kernel(q_ref, k_ref, v_ref, o_ref, sc_ref, mx_ref, acc_ref, *, tq, tk, s_len, q_pos0):
    qi = pl.program_id(1)
    q_first = q_pos0 + qi * tq
    q_last = q_first + tq - 1
    vis_first = jnp.minimum(s_len, (q_first // CHUNK + 1) * CHUNK)
    vis_last = jnp.minimum(s_len, (q_last // CHUNK + 1) * CHUNK)
    n_full = vis_first // tk
    n_kv = (vis_last + tk - 1) // tk

    q_chunk = _div_pow2(q_first + lax.broadcasted_iota(jnp.int32, (tq, tk), 0), CHUNK)
    k_iota = lax.broadcasted_iota(jnp.int32, (tq, tk), 1)

    for g0 in range(0, D_HEADS, ATTN_HEAD_GROUP):
        heads = [slice(h * HEAD_PAD, (h + 1) * HEAD_PAD) for h in range(g0, g0 + ATTN_HEAD_GROUP)]

        def score_pass(masked, heads=heads):
            def body(j, carry):
                k0 = pl.multiple_of(j * tk, tk)
                visible = (_div_pow2(k0 + k_iota, CHUNK) <= q_chunk) if masked else None
                for h, sl in enumerate(heads):
                    s = _dot_nt(q_ref[:, sl], k_ref[pl.ds(k0, tk), sl])
                    if masked:
                        s = jnp.where(visible, s, NEG_BIG)
                    sc_ref[h, j] = s
                    tile_max = s[:, :LANES]
                    for c in range(LANES, tk, LANES):
                        tile_max = jnp.maximum(tile_max, s[:, c:c + LANES])
                    mx_ref[h] = jnp.maximum(mx_ref[h], tile_max)
                return carry
            return body

        def sum_pass(j, carry, heads=heads):
            k0 = pl.multiple_of(j * tk, tk)
            for h, sl in enumerate(heads):
                m = mx_ref[h]
                p = jnp.exp2(sc_ref[h, j] - jnp.concatenate([m] * (tk // LANES), axis=1))
                acc_ref[h] += _dot(p.astype(BF16), v_ref[pl.ds(k0, tk), sl])
            return carry

        mx_ref[...] = jnp.full_like(mx_ref, NEG_BIG)
        lax.fori_loop(0, n_full, score_pass(False), 0)
        lax.fori_loop(n_full, n_kv, score_pass(True), 0)
        for h in range(ATTN_HEAD_GROUP):
            m = jnp.max(mx_ref[h], axis=-1, keepdims=True)
            mx_ref[h] = jnp.broadcast_to(m, (tq, LANES))
        acc_ref[...] = jnp.zeros_like(acc_ref)
        lax.fori_loop(0, n_kv, sum_pass, 0)
        for h, sl in enumerate(heads):
            acc = acc_ref[h]
            o_ref[:, sl] = (acc / acc[:, V_ONE_LANE:V_ONE_LANE + 1]).astype(o_ref.dtype)


def attention(q, k, v, q_pos0, tq_cap=512, tk_cap=512):
    bsz, tq_len, width = q.shape
    s_len = k.shape[1]
    tq = _row_tile(tq_len, tq_cap)
    tk = _row_tile(s_len, tk_cap)
    assert tk % LANES == 0
    kern = functools.partial(_attn_kernel, tq=tq, tk=tk, s_len=s_len, q_pos0=q_pos0)
    return pl.pallas_call(
        kern,
        grid=(bsz, tq_len // tq),
        in_specs=[
            pl.BlockSpec((None, tq, width), lambda b, i: (b, i, 0)),
            pl.BlockSpec((None, s_len, width), lambda b, i: (b, 0, 0)),
            pl.BlockSpec((None, s_len, width), lambda b, i: (b, 0, 0)),
        ],
        out_specs=pl.BlockSpec((None, tq, width), lambda b, i: (b, i, 0)),
        out_shape=jax.ShapeDtypeStruct((bsz, tq_len, width), BF16),
        scratch_shapes=[pltpu.VMEM((ATTN_HEAD_GROUP, s_len // tk, tq, tk), F32),
                        pltpu.VMEM((ATTN_HEAD_GROUP, tq, LANES), F32),
                        pltpu.VMEM((ATTN_HEAD_GROUP, tq, HEAD_PAD), F32)],
        compiler_params=_cparams(("parallel", "arbitrary")),
        name="attention",
    )(q, k, v)


def _mla_decode_kernel(q_ref, ckv_ref, kpe_ref, wk_ref, wkt_ref, wv_ref, kg_ref, o_ref,
                       *, tq, s_len, q_pos0):
    s_pad = ckv_ref.shape[0]
    heads = [slice(h * HEAD_PAD, (h + 1) * HEAD_PAD) for h in range(D_HEADS)]
    ckv = ckv_ref[...].astype(BF16)
    kpe = kpe_ref[...].astype(BF16)
    eye = (lax.broadcasted_iota(jnp.int32, (LANES, LANES), 0)
           == lax.broadcasted_iota(jnp.int32, (LANES, LANES), 1))
    kpe_t = _dot_nt(jnp.where(eye, 1.0, 0.0).astype(BF16), kpe)
    pe_ss = jnp.sum(kpe_t * kpe_t, axis=0, keepdims=True)
    kn_t = _dot_nt(wkt_ref[...], ckv)
    inv_rms, q_lat, q_g = [], [], []
    for sl in heads:
        blk = kn_t[sl, :]
        ss = jnp.sum(blk * blk, axis=0, keepdims=True) + pe_ss
        inv_rms.append(jnp.broadcast_to(lax.rsqrt(ss * (1.0 / QK_DIM) + EPS), (tq, s_pad)))
        qg = (q_ref[:, sl].astype(F32) * kg_ref[...]).astype(BF16)
        q_g.append(qg)
        q_lat.append(_dot_nt(qg, wk_ref[:, sl]).astype(BF16))
    s = _dot_nt(jnp.concatenate(q_lat, axis=0), ckv) + _dot_nt(jnp.concatenate(q_g, axis=0), kpe)
    s = s * jnp.concatenate(inv_rms, axis=0)
    row = lax.broadcasted_iota(jnp.int32, s.shape, 0)
    col = lax.broadcasted_iota(jnp.int32, s.shape, 1)
    q_chunk = _div_pow2(q_pos0 + _mod_pow2(row, tq), CHUNK)
    visible = (_div_pow2(col, CHUNK) <= q_chunk) & (col < s_len)
    s = jnp.where(visible, s, NEG_BIG)
    p = jnp.exp2(s - jnp.max(s, axis=-1, keepdims=True))
    lat = _dot(p.astype(BF16), ckv) / jnp.sum(p, axis=-1, keepdims=True)
    for h, sl in enumerate(heads):
        o_ref[:, sl] = _dot(lat[h * tq:(h + 1) * tq].astype(BF16), wv_ref[:, sl]).astype(o_ref.dtype)


def mla_decode(q, ckv_all, kpe_all, wk_p, wv_p, k_gain_p, q_pos0):
    bsz, tq, width = q.shape
    s_len = ckv_all.shape[1]
    s_pad = -(-s_len // LANES) * LANES
    assert tq & (tq - 1) == 0
    ckv_pad = jnp.pad(ckv_all, ((0, 0), (0, s_pad - s_len), (0, 0)))
    kpe_pad = jnp.pad(kpe_all, ((0, 0), (0, s_pad - s_len), (NOPE_DIM, HEAD_PAD - QK_DIM)))
    full = lambda r, c: pl.BlockSpec((r, c), lambda b: (0, 0))
    kern = functools.partial(_mla_decode_kernel, tq=tq, s_len=s_len, q_pos0=q_pos0)
    return pl.pallas_call(
        kern,
        grid=(bsz,),
        in_specs=[
            pl.BlockSpec((None, tq, width), lambda b: (b, 0, 0)),
            pl.BlockSpec((None, s_pad, KV_LORA), lambda b: (b, 0, 0)),
            pl.BlockSpec((None, s_pad, HEAD_PAD), lambda b: (b, 0, 0)),
            full(KV_LORA, width), full(width, KV_LORA), full(KV_LORA, width), full(1, HEAD_PAD),
        ],
        out_specs=pl.BlockSpec((None, tq, width), lambda b: (b, 0, 0)),
        out_shape=jax.ShapeDtypeStruct((bsz, tq, width), BF16),
        compiler_params=_cparams(("parallel",)),
        name="mla_decode",
    )(q, ckv_pad, kpe_pad, wk_p, wk_p.T, wv_p, k_gain_p)


def rope_tables(pos):
    half = ROPE_DIM // 2
    inv = ROPE_BASE ** (-jnp.arange(half, dtype=F32) / half)
    ang = pos.astype(F32)[:, None] * inv[None, :]
    cos, sin = jnp.cos(ang), jnp.sin(ang)
    n = pos.shape[0]
    z = lambda w: jnp.zeros((n, w), F32)
    o = lambda w: jnp.ones((n, w), F32)
    tail = LANES - NOPE_DIM - ROPE_DIM
    q_cos = jnp.concatenate([o(NOPE_DIM), cos, cos, z(tail)], axis=1)
    q_sin = jnp.concatenate([z(NOPE_DIM), sin, sin, z(tail)], axis=1)
    k_cos = jnp.concatenate([cos, cos, z(LANES - ROPE_DIM)], axis=1)
    k_s1 = jnp.concatenate([-sin, z(LANES - half)], axis=1)
    k_s2 = jnp.concatenate([z(half), sin, z(LANES - ROPE_DIM)], axis=1)
    return q_cos, q_sin, k_cos, k_s1, k_s2


def pack_mla_weights(w_uq, w_ukv, q_gain, k_gain, cd_w_out):
    pad_q = HEAD_PAD - QK_DIM
    half = ROPE_DIM // 2
    wq = w_uq.reshape(Q_LORA, D_HEADS, QK_DIM)
    wuq_p = jnp.pad(wq, ((0, 0), (0, 0), (0, pad_q))).reshape(Q_LORA, D_HEADS * HEAD_PAD)
    wq_rot = jnp.concatenate([jnp.zeros_like(wq[:, :, :NOPE_DIM]), -wq[:, :, NOPE_DIM + half:],
                              wq[:, :, NOPE_DIM:NOPE_DIM + half], jnp.zeros_like(wq[:, :, :pad_q])], axis=2)
    wuq_p = jnp.concatenate([wuq_p, wq_rot.reshape(Q_LORA, D_HEADS * HEAD_PAD)], axis=1).astype(BF16)
    wkv = w_ukv.reshape(KV_LORA, D_HEADS, NOPE_DIM + V_DIM)
    wk_p = jnp.pad(wkv[:, :, :NOPE_DIM], ((0, 0), (0, 0), (0, HEAD_PAD - NOPE_DIM)))
    wk_p = wk_p.reshape(KV_LORA, D_HEADS * HEAD_PAD).astype(BF16)
    wv_p = jnp.pad(wkv[:, :, NOPE_DIM:], ((0, 0), (0, 0), (0, HEAD_PAD - V_DIM)))
    wv_p = wv_p.reshape(KV_LORA, D_HEADS * HEAD_PAD).astype(BF16)
    place = jnp.pad(jnp.eye(ROPE_DIM, dtype=F32), ((0, LANES - ROPE_DIM), (NOPE_DIM, HEAD_PAD - QK_DIM)))
    wk_full = jnp.concatenate([wk_p, jnp.tile(place, (1, D_HEADS)).astype(BF16)], axis=0)
    q_gain_p = jnp.pad(q_gain, (0, pad_q)).reshape(1, HEAD_PAD)
    k_gain_p = jnp.pad(k_gain, (0, pad_q)).reshape(1, HEAD_PAD)
    w_c = cd_w_out[:C_WIDTH].astype(BF16)
    w_d = cd_w_out[C_WIDTH:].reshape(D_HEADS, V_DIM, D_MODEL)
    w_d_p = jnp.pad(w_d, ((0, 0), (0, HEAD_PAD - V_DIM), (0, 0))).reshape(D_HEADS * HEAD_PAD, D_MODEL)
    return wuq_p, wk_p, wk_full, wv_p, q_gain_p, k_gain_p, w_c, w_d_p.astype(BF16)


def _trunk(x, pos0, w, past):
    bsz, t, _ = x.shape
    n = bsz * t
    x2 = x.reshape(n, D_MODEL)
    pos = pos0 + jnp.arange(t, dtype=jnp.int32)
    lb_all = jnp.cumsum(jax.nn.softmax(w['hgrn_lb_logits'], axis=0), axis=0)
    outs = {}
    ffn_w = lambda name, l: (w[name + '_norm'][l], w[name + '_w_gate'][l], w[name + '_w_up'][l],
                             w[name + '_w_down'][l])
    tmaj = (bsz, t) if t % FFN_ROWS == 0 else None
    for l in range(2):
        w_in = w['ab_w_in'] if l == 0 else w['cd_w_in_p']
        res = ffn(x2, *ffn_w('ffn1', l), post=(w['mix_norm'][l], w_in), time_major=tmaj if l == 0 else None)
        x2, proj = res[0], res[1].reshape(bsz, t, -1)
        if l == 0:
            s0 = None if past is None else past[0][0]
            a_out, s_t = hgrn2(proj, lb_all[0], w['hgrn_out_norm'][0], s0)
            if past is None:
                x0 = jnp.zeros((bsz, S5_STATE_W), F32)
            else:
                x0 = s5_pack_state(past[1][0], past[2][0])
            if tmaj is not None:
                u_tb = res[2].reshape(t, bsz, B_WIDTH)
            else:
                u_tb = jnp.transpose(proj[:, :, AB_IN - B_WIDTH:], (1, 0, 2))
            b_tb, x_t = s5_mix(u_tb, w['s5_wb'], w['s5_wc'], w['s5_lam'], w['s5_d'],
                               w['s5_w_glu'], w['s5_b_glu'], x0)
            if tmaj is not None:
                b_out = b_tb.reshape(t, bsz * B_WIDTH)
            else:
                b_out = jnp.transpose(b_tb, (1, 0, 2)).reshape(n, B_WIDTH)
            pre = (a_out.reshape(n, A_WIDTH), b_out, w['ab_w_out_a'], w['ab_w_out_b'])
            outs['hgrn'] = s_t
            outs['s5re'], outs['s5im'] = s5_unpack_state(x_t)
        else:
            c_out, v_rows = gmlp(proj, w['gmlp_v_norm'], w['gmlp_w_s'], w['gmlp_b_s'])
            q, ckv, kpe, ckpe = mla_q(proj, rope_tables(pos), w['mla_q_norm'], w['mla_kv_norm'],
                                      w['mla_wuq_p'], w['mla_q_gain_p'])
            if past is None:
                k, v = mla_kv(ckpe, w['mla_wk_full'], w['mla_wv_p'], w['mla_k_gain_p'])
                d_out = attention(q, k, v, pos0)
            else:
                ckv_all = jnp.concatenate([past[3][0], ckv], axis=1)
                kpe_all = jnp.concatenate([past[4][0], kpe], axis=1)
                d_out = mla_decode(q, ckv_all, kpe_all, w['mla_wk_p'], w['mla_wv_p'],
                                   w['mla_k_gain_p'], pos0)
            pre = (c_out.reshape(n, C_WIDTH), d_out.reshape(n, D_HEADS * HEAD_PAD),
                   w['cd_w_out_c'], w['cd_w_out_d'])
            outs['gv'], outs['ckv'], outs['kpe'] = v_rows, ckv, kpe
        x2 = ffn(x2, *ffn_w('ffn2', l), pre=pre, time_major=tmaj if l == 0 else None)
    return x2.reshape(bsz, t, D_MODEL), outs


def kernel(x_prompt, x_sample, state_hgrn, state_s5_re, state_s5_im, cache_mla_ckv, cache_mla_kpe, ffn1_norm, ffn1_w_gate, ffn1_w_up, ffn1_w_down, mix_norm, ffn2_norm, ffn2_w_gate, ffn2_w_up, ffn2_w_down, ab_w_in, hgrn_lb_logits, hgrn_out_norm, s5_lambda_re, s5_lambda_im, s5_log_dt, s5_b_re, s5_b_im, s5_c_re, s5_c_im, s5_d, s5_w_glu, s5_b_glu, ab_w_out, cd_w_in, gmlp_v_norm, gmlp_w_s, gmlp_b_s, mla_q_norm, mla_w_uq, mla_kv_norm, mla_w_ukv, mla_q_gain, mla_k_gain, cd_w_out):
    bf = lambda a: a.astype(BF16)
    wb, wc, lam_rows = s5_prepare(s5_lambda_re[0], s5_lambda_im[0], s5_log_dt[0],
                                  s5_b_re[0], s5_b_im[0], s5_c_re[0], s5_c_im[0])
    wuq_p, wk_p, wk_full, wv_p, q_gain_p, k_gain_p, w_c, w_d_p = pack_mla_weights(
        mla_w_uq[0], mla_w_ukv[0], mla_q_gain[0], mla_k_gain[0], cd_w_out[0])
    w = dict(
        ffn1_norm=ffn1_norm, ffn1_w_gate=bf(ffn1_w_gate), ffn1_w_up=bf(ffn1_w_up), ffn1_w_down=bf(ffn1_w_down),
        ffn2_norm=ffn2_norm, ffn2_w_gate=bf(ffn2_w_gate), ffn2_w_up=bf(ffn2_w_up), ffn2_w_down=bf(ffn2_w_down),
        mix_norm=mix_norm, ab_w_in=bf(ab_w_in[0]), hgrn_lb_logits=hgrn_lb_logits, hgrn_out_norm=hgrn_out_norm,
        s5_wb=wb, s5_wc=wc, s5_lam=lam_rows, s5_d=s5_d[0].reshape(B_WIDTH), s5_w_glu=bf(s5_w_glu[0]),
        s5_b_glu=s5_b_glu[0], ab_w_out_a=bf(ab_w_out[0, :A_WIDTH]), ab_w_out_b=bf(ab_w_out[0, A_WIDTH:]),
        cd_w_in_p=bf(jnp.pad(cd_w_in[0], ((0, 0), (0, CD_IN_PAD - CD_IN)))),
        gmlp_v_norm=gmlp_v_norm[0], gmlp_w_s=gmlp_w_s[0], gmlp_b_s=gmlp_b_s[0],
        mla_q_norm=mla_q_norm[0], mla_kv_norm=mla_kv_norm[0], mla_wuq_p=wuq_p, mla_q_gain_p=q_gain_p,
        mla_wk_p=wk_p, mla_wk_full=wk_full, mla_wv_p=wv_p, mla_k_gain_p=k_gain_p,
        cd_w_out_c=w_c, cd_w_out_d=w_d_p,
    )
    past_len = cache_mla_ckv.shape[2]
    y_p, o_p = _trunk(x_prompt, 0, w, None)
    y_s, o_s = _trunk(x_sample, past_len, w,
                      (state_hgrn, state_s5_re, state_s5_im, cache_mla_ckv, cache_mla_kpe))
    e = lambda a: a[None]
    return (y_p, y_s, e(o_p['hgrn']), e(o_s['hgrn']), e(o_p['s5re']), e(o_p['s5im']),
            e(o_s['s5re']), e(o_s['s5im']), e(o_p['gv']), e(o_s['gv']),
            e(o_p['ckv']), e(o_p['kpe']), e(o_s['ckv']), e(o_s['kpe']))
```

```python
import functools
import math

import jax
import jax.numpy as jnp
from jax import lax
from jax.experimental import pallas as pl
from jax.experimental.pallas import tpu as pltpu

D_MODEL = 1024
D_FF = 2816
EPS = 1e-6
A_HEADS = 4
A_KEY = 128
A_VAL = 128
A_WIDTH = A_HEADS * A_VAL
HGRN_BLOCK = 64
B_GROUP_DIM = 16
B_WIDTH = 512
B_GROUPS = 32
B_STATE = 64
LAMBDA_RE_CEIL = -1e-4
C_CHUNK = 128
C_WIDTH = 512
C_GROUPS = 4
C_GROUP_DIM = 128
D_HEADS = 8
Q_LORA = 256
KV_LORA = 128
NOPE_DIM = 64
ROPE_DIM = 32
V_DIM = 64
QK_DIM = NOPE_DIM + ROPE_DIM
ROPE_BASE = 10000.0
CHUNK = 64
A_QK = A_HEADS * A_KEY
AB_IN = 2 * A_QK + 2 * A_WIDTH + B_WIDTH
CD_IN = 2 * C_WIDTH + Q_LORA + KV_LORA + ROPE_DIM
CD_IN_PAD = 1536

LANES = 128
SUBLANES = 8
VMEM_LIMIT = 56 * 1024 * 1024
HGRN_SUB = 16
HEAD_PAD = LANES
NEG_BIG = -1e30
EXP_CLAMP = 80.0
V_ONE_ROW = V_DIM
ATTN_TK = 512
ATTN_HEAD_GROUP = 4

BF16 = jnp.bfloat16
F32 = jnp.float32


def _cparams(sem):
    return pltpu.CompilerParams(dimension_semantics=sem, vmem_limit_bytes=VMEM_LIMIT)


def _rms(xf, width):
    return xf * lax.rsqrt(jnp.sum(xf * xf, axis=-1, keepdims=True) * (1.0 / width) + EPS)


def _dot(a, b):
    return jnp.dot(a, b, preferred_element_type=F32)


def _dot_nt(a, b):
    return lax.dot_general(a, b, (((1,), (1,)), ((), ())), preferred_element_type=F32)


def _dot_tn(a, b):
    return lax.dot_general(a, b, (((0,), (0,)), ((), ())), preferred_element_type=F32)


def _div_pow2(x, d):
    assert d & (d - 1) == 0
    return lax.shift_right_logical(x, jnp.int32(d.bit_length() - 1))


def _mod_pow2(x, d):
    assert d & (d - 1) == 0
    return x & jnp.int32(d - 1)


def _row_tile(n, cap):
    t = min(n, cap)
    assert n % t == 0, (n, t)
    return t


FFN_CHUNK = 256
POST_CHUNK = 512
FFN_ROWS = 512


def _ffn_kernel(*refs, has_pre, has_post, split_tail):
    refs = list(refs)
    x_ref = refs.pop(0)
    if has_pre:
        a1_ref, a2_ref, w1_ref, w2_ref = refs[:4]
        refs = refs[4:]
    g_ref, wg_ref, wu_ref, wd_ref = refs[:4]
    refs = refs[4:]
    if has_post:
        g2_ref, wp_ref = refs[:2]
        refs = refs[2:]
    o_ref = refs.pop(0)
    if has_post:
        p_ref = refs.pop(0)
    if split_tail:
        tail_ref = refs.pop(0)
    acc_ref = refs.pop(0)

    x = x_ref[...]
    if has_pre:
        x = x + _dot(a1_ref[...], w1_ref[...]) + _dot(a2_ref[...], w2_ref[...])
    h = (_rms(x, D_MODEL) * g_ref[...]).astype(BF16)
    for j in range(D_FF // FFN_CHUNK):
        cs = slice(j * FFN_CHUNK, (j + 1) * FFN_CHUNK)
        gate = _dot(h, wg_ref[:, cs])
        up = _dot(h, wu_ref[:, cs])
        act = (gate * jax.nn.sigmoid(gate) * up).astype(BF16)
        part = _dot(act, wd_ref[cs, :])
        if j == 0:
            acc_ref[...] = part
        else:
            acc_ref[...] += part
    out = x + 0.5 * acc_ref[...]
    o_ref[...] = out
    if has_post:
        h2 = (_rms(out, D_MODEL) * g2_ref[...]).astype(BF16)
        n_post = wp_ref.shape[1] // POST_CHUNK
        for c in range(n_post):
            cs = slice(c * POST_CHUNK, (c + 1) * POST_CHUNK)
            part = _dot(h2, wp_ref[:, cs]).astype(p_ref.dtype)
            if split_tail and c == n_post - 1:
                tail_ref[...] = part
            else:
                p_ref[:, cs] = part


def ffn(x, g, wg, wu, wd, pre=None, post=None, time_major=None, tm_cap=FFN_ROWS):
    n = x.shape[0]
    tm = _row_tile(n, tm_cap)
    rows = lambda w: pl.BlockSpec((tm, w), lambda i: (i, 0))
    fixed = lambda r, c: pl.BlockSpec((r, c), lambda i: (0, 0), pipeline_mode=pl.Buffered(1))
    if time_major is not None:
        streams, steps = time_major
        assert steps % tm == 0 and streams * steps == n
        per_stream = steps // tm
        by_time = lambda w: pl.BlockSpec((tm, w), lambda i: (i % per_stream, i // per_stream))
    args, specs = [x], [rows(D_MODEL)]
    if pre is not None:
        a1, a2, w1, w2 = pre
        args += [a1, a2, w1, w2]
        a2_spec = by_time(w2.shape[0]) if time_major is not None else rows(a2.shape[1])
        specs += [rows(a1.shape[1]), a2_spec, fixed(*w1.shape), fixed(*w2.shape)]
    args += [g.reshape(1, D_MODEL), wg, wu, wd]
    specs += [fixed(1, D_MODEL), fixed(D_MODEL, D_FF), fixed(D_MODEL, D_FF), fixed(D_FF, D_MODEL)]
    out_specs = [rows(D_MODEL)]
    out_shape = [jax.ShapeDtypeStruct((n, D_MODEL), F32)]
    split_tail = post is not None and time_major is not None
    if post is not None:
        g2, wp = post
        assert wp.shape[1] % POST_CHUNK == 0
        args += [g2.reshape(1, D_MODEL), wp]
        specs += [fixed(1, D_MODEL), fixed(*wp.shape)]
        head_w = wp.shape[1] - (POST_CHUNK if split_tail else 0)
        out_specs.append(rows(head_w))
        out_shape.append(jax.ShapeDtypeStruct((n, head_w), BF16))
        if split_tail:
            out_specs.append(by_time(POST_CHUNK))
            out_shape.append(jax.ShapeDtypeStruct((steps, streams * POST_CHUNK), BF16))
    kern = functools.partial(_ffn_kernel, has_pre=pre is not None, has_post=post is not None,
                             split_tail=split_tail)
    res = pl.pallas_call(
        kern,
        grid=(n // tm,),
        in_specs=specs,
        out_specs=out_specs,
        out_shape=out_shape,
        scratch_shapes=[pltpu.VMEM((tm, D_MODEL), F32)],
        compiler_params=_cparams(("parallel",)),
        name="ffn",
    )(*args)
    return res if post is not None else res[0]


def _hgrn_kernel(q_ref, f_ref, i_ref, g_ref, lb_ref, og_ref, s0_ref, y_ref, st_ref, state_ref,
                 *, tt, blk, has_past):
    t_idx = pl.program_id(1)
    n_chunks = tt // blk
    n_sub = blk // HGRN_SUB

    @pl.when(t_idx == 0)
    def _():
        if has_past:
            for h in range(A_HEADS):
                state_ref[h] = s0_ref[h].T
        else:
            state_ref[...] = jnp.zeros_like(state_ref)

    row = lax.broadcasted_iota(jnp.int32, (tt, tt), 0)
    col = lax.broadcasted_iota(jnp.int32, (tt, tt), 1)
    same_chunk = _div_pow2(row, blk) == _div_pow2(col, blk)
    causal = same_chunk & (col <= row)
    tril = jnp.where(causal, 1.0, 0.0).astype(BF16)
    heads = [slice(h * LANES, (h + 1) * LANES) for h in range(A_HEADS)]

    q = q_ref[...].astype(F32)
    v_bf = i_ref[...].astype(BF16)
    lb = lb_ref[...]
    f = lb + (1.0 - lb) * jax.nn.sigmoid(f_ref[...].astype(F32))
    k = 1.0 - f
    logf = jnp.log(f)
    hi = logf.astype(BF16)
    mid = (logf - hi.astype(F32)).astype(BF16)
    b = _dot(tril, hi) + _dot(tril, mid)

    q_slab, k_slab, ends = [], [], []
    for c in range(n_chunks):
        rows = [slice(c * blk + i * HGRN_SUB, c * blk + (i + 1) * HGRN_SUB) for i in range(n_sub)]
        mids = [b[r.start + HGRN_SUB // 2 - 1:r.start + HGRN_SUB // 2, :] for r in rows]
        q_slab.append([(q[r] * jnp.exp(jnp.minimum(b[r] - mids[i], EXP_CLAMP))).astype(BF16)
                       for i, r in enumerate(rows)])
        k_slab.append([[None if j > i else
                        (k[r] * jnp.exp(jnp.minimum(mids[i] - b[r], EXP_CLAMP) if j == i
                                        else mids[i] - b[r])).astype(BF16)
                        for i in range(n_sub)] for j, r in enumerate(rows)])
        ends.append(b[(c + 1) * blk - 1:(c + 1) * blk, :])
    b_end_rows = (jnp.broadcast_to(ends[0], (blk, A_WIDTH)) if n_chunks == 1 else
                  jnp.concatenate([jnp.broadcast_to(e, (blk, A_WIDTH)) for e in ends], axis=0))
    q_e = (q * jnp.exp(b)).astype(BF16)
    k_d = (k * jnp.exp(b_end_rows - b)).astype(BF16)
    zero_slab = jnp.zeros((HGRN_SUB, LANES), BF16)
    zero_blk = jnp.zeros((blk, LANES), BF16)
    chunk_rows = [slice(c * blk, (c + 1) * blk) for c in range(n_chunks)]

    outs = []
    for h, sl in enumerate(heads):
        q_hat = jnp.concatenate(
            [jnp.concatenate([q_slab[c][j][:, sl] if i == j else zero_slab for i in range(n_sub)], axis=1)
             for c in range(n_chunks) for j in range(n_sub)], axis=0)
        k_hat = jnp.concatenate(
            [jnp.concatenate([k_slab[c][j][i][:, sl] if j <= i else zero_slab for i in range(n_sub)], axis=1)
             for c in range(n_chunks) for j in range(n_sub)], axis=0)
        att = _dot_nt(q_hat, k_hat)
        att = jnp.where(causal, att, 0.0).astype(BF16)
        spread = lambda a: jnp.concatenate(
            [jnp.concatenate([a[rs, sl] if c == d else zero_blk for d in range(n_chunks)], axis=1)
             for c, rs in enumerate(chunk_rows)], axis=0)
        kv = _dot_tn(spread(v_bf), k_d[:, sl])
        s_t = state_ref[h]
        states = []
        for c in range(n_chunks):
            states.append(s_t.astype(BF16))
            s_t = s_t * jnp.exp(ends[c][:, sl]) + kv[c * A_VAL:(c + 1) * A_VAL]
        state_ref[h] = s_t
        o = _dot(att, v_bf[:, sl]) + _dot_nt(spread(q_e), jnp.concatenate(states, axis=1))
        outs.append(_rms(o, A_VAL) * og_ref[...])
    y = jnp.concatenate(outs, axis=1) * jax.nn.sigmoid(g_ref[...].astype(F32))
    y_ref[...] = y.astype(y_ref.dtype)

    @pl.when(t_idx == pl.num_programs(1) - 1)
    def _():
        for h in range(A_HEADS):
            st_ref[h] = state_ref[h].T


def hgrn2(proj, lb, out_g, s0, tt_cap=256):
    bsz, t, _ = proj.shape
    blk = min(HGRN_BLOCK, t)
    tt = _row_tile(t, tt_cap)
    has_past = s0 is not None
    if s0 is None:
        s0 = jnp.zeros((bsz, A_HEADS, A_KEY, A_VAL), F32)
    seg = lambda s: pl.BlockSpec((None, tt, A_WIDTH), lambda b, i, s=s: (b, i, s))
    kern = functools.partial(_hgrn_kernel, tt=tt, blk=blk, has_past=has_past)
    return pl.pallas_call(
        kern,
        grid=(bsz, t // tt),
        in_specs=[
            seg(0), seg(1), seg(2), seg(3),
            pl.BlockSpec((1, A_QK), lambda b, i: (0, 0)),
            pl.BlockSpec((1, A_VAL), lambda b, i: (0, 0)),
            pl.BlockSpec((None, A_HEADS, A_KEY, A_VAL), lambda b, i: (b, 0, 0, 0)),
        ],
        out_specs=[
            pl.BlockSpec((None, tt, A_WIDTH), lambda b, i: (b, i, 0)),
            pl.BlockSpec((None, A_HEADS, A_KEY, A_VAL), lambda b, i: (b, 0, 0, 0)),
        ],
        out_shape=[
            jax.ShapeDtypeStruct((bsz, t, A_WIDTH), BF16),
            jax.ShapeDtypeStruct((bsz, A_HEADS, A_KEY, A_VAL), F32),
        ],
        scratch_shapes=[pltpu.VMEM((A_HEADS, A_VAL, A_KEY), F32)],
        compiler_params=_cparams(("parallel", "arbitrary")),
        name="hgrn2",
    )(proj, proj, proj, proj, lb.reshape(1, A_QK), out_g.reshape(1, A_VAL), s0)


S5_SUPER = 4
S5_SUPER_U = B_WIDTH // S5_SUPER
S5_SUPER_X = 2 * (B_GROUPS // S5_SUPER) * B_STATE
S5_HALF = S5_SUPER_X // 2
S5_STATE_W = S5_SUPER * S5_SUPER_X
S5_SCAN_GROUP = 2


def _s5_kernel(u_ref, wb_ref, wc_ref, lam_ref, d_ref, wglu_ref, bglu_ref, x0_ref,
               o_ref, xt_ref, xbuf_ref, state_ref, *, tt):
    t_idx = pl.program_id(1)
    rows = tt * SUBLANES

    @pl.when(t_idx == 0)
    def _():
        state_ref[...] = x0_ref[...]

    u_bf = u_ref[...].reshape(rows, B_WIDTH)
    u = u_bf.astype(F32)
    for j in range(S5_SUPER):
        xbuf_ref[:, j * S5_SUPER_X:(j + 1) * S5_SUPER_X] = _dot(
            u_bf[:, j * S5_SUPER_U:(j + 1) * S5_SUPER_U], wb_ref[j])

    for j0 in range(0, S5_SUPER, S5_SCAN_GROUP):
        cols, lams = [], []
        for j in range(j0, j0 + S5_SCAN_GROUP):
            re_cols = slice(j * S5_SUPER_X, j * S5_SUPER_X + S5_HALF)
            im_cols = slice(j * S5_SUPER_X + S5_HALF, (j + 1) * S5_SUPER_X)
            cols.append((re_cols, im_cols))
            lams.append((jnp.broadcast_to(lam_ref[0:1, re_cols], (SUBLANES, S5_HALF)),
                         jnp.broadcast_to(lam_ref[1:2, re_cols], (SUBLANES, S5_HALF))))

        def step(t, carry, cols=cols, lams=lams):
            rows_t = pl.ds(pl.multiple_of(t * SUBLANES, SUBLANES), SUBLANES)
            out = []
            for (re_cols, im_cols), (lam_re, lam_im), (x_re, x_im) in zip(cols, lams, carry):
                n_re = lam_re * x_re - lam_im * x_im + xbuf_ref[rows_t, re_cols]
                n_im = lam_re * x_im + lam_im * x_re + xbuf_ref[rows_t, im_cols]
                xbuf_ref[rows_t, re_cols] = n_re
                xbuf_ref[rows_t, im_cols] = n_im
                out.append((n_re, n_im))
            return tuple(out)

        init = tuple((state_ref[:, rc], state_ref[:, ic]) for rc, ic in cols)
        last = lax.fori_loop(0, tt, step, init, unroll=True)
        for (rc, ic), (x_re, x_im) in zip(cols, last):
            state_ref[:, rc] = x_re
            state_ref[:, ic] = x_im

    ys = []
    for j in range(S5_SUPER):
        xs = xbuf_ref[:, j * S5_SUPER_X:(j + 1) * S5_SUPER_X].astype(BF16)
        ys.append(_dot(xs, wc_ref[j]))
    y = jnp.concatenate(ys, axis=1) + d_ref[...] * u
    z = jax.nn.gelu(y)
    gate = jax.nn.sigmoid(_dot(z.astype(BF16), wglu_ref[...]) + bglu_ref[...])
    o_ref[...] = (z * gate).astype(o_ref.dtype).reshape(tt, SUBLANES, B_WIDTH)

    @pl.when(t_idx == pl.num_programs(1) - 1)
    def _():
        xt_ref[...] = state_ref[...]


def s5_mix(u_tb, wb, wc, lam, d, w_glu, b_glu, x0, tt_cap=64):
    t, bsz, _ = u_tb.shape
    tt = _row_tile(t, tt_cap)
    kern = functools.partial(_s5_kernel, tt=tt)
    return pl.pallas_call(
        kern,
        grid=(bsz // SUBLANES, t // tt),
        in_specs=[
            pl.BlockSpec((tt, SUBLANES, B_WIDTH), lambda b, i: (i, b, 0)),
            pl.BlockSpec((S5_SUPER, S5_SUPER_U, S5_SUPER_X), lambda b, i: (0, 0, 0)),
            pl.BlockSpec((S5_SUPER, S5_SUPER_X, S5_SUPER_U), lambda b, i: (0, 0, 0)),
            pl.BlockSpec((2, S5_STATE_W), lambda b, i: (0, 0)),
            pl.BlockSpec((1, B_WIDTH), lambda b, i: (0, 0)),
            pl.BlockSpec((B_WIDTH, B_WIDTH), lambda b, i: (0, 0)),
            pl.BlockSpec((1, B_WIDTH), lambda b, i: (0, 0)),
            pl.BlockSpec((SUBLANES, S5_STATE_W), lambda b, i: (b, 0)),
        ],
        out_specs=[
            pl.BlockSpec((tt, SUBLANES, B_WIDTH), lambda b, i: (i, b, 0)),
            pl.BlockSpec((SUBLANES, S5_STATE_W), lambda b, i: (b, 0)),
        ],
        out_shape=[
            jax.ShapeDtypeStruct((t, bsz, B_WIDTH), BF16),
            jax.ShapeDtypeStruct((bsz, S5_STATE_W), F32),
        ],
        scratch_shapes=[pltpu.VMEM((tt * SUBLANES, S5_STATE_W), F32),
                        pltpu.VMEM((SUBLANES, S5_STATE_W), F32)],
        compiler_params=_cparams(("parallel", "arbitrary")),
        name="s5_mix",
    )(u_tb, wb, wc, lam, d.reshape(1, B_WIDTH), w_glu, b_glu.reshape(1, B_WIDTH), x0)


def _s5_pack_cols(a):
    return a.reshape(a.shape[:-2] + (S5_SUPER, S5_HALF))


def s5_prepare(lam_re, lam_im, log_dt, b_re, b_im, c_re, c_im):
    lr = jnp.minimum(lam_re, LAMBDA_RE_CEIL)
    li = lam_im
    dt = jnp.exp(log_dt)[:, None]
    mag = jnp.exp(lr * dt)
    bar_re = mag * jnp.cos(li * dt)
    bar_im = mag * jnp.sin(li * dt)
    den = lr * lr + li * li
    coef_re = (((bar_re - 1.0) * lr + bar_im * li) / den)[:, :, None]
    coef_im = ((bar_im * lr - (bar_re - 1.0) * li) / den)[:, :, None]
    bb_re = coef_re * b_re - coef_im * b_im
    bb_im = coef_re * b_im + coef_im * b_re
    gps = B_GROUPS // S5_SUPER
    eye = jnp.eye(gps, dtype=F32)

    def in_block(bm):
        bm = bm.reshape(S5_SUPER, gps, B_STATE, B_GROUP_DIM)
        blk = jnp.einsum('jgph,gk->jghkp', bm, eye)
        return blk.reshape(S5_SUPER, gps * B_GROUP_DIM, gps * B_STATE)

    def out_block(cm):
        cm = cm.reshape(S5_SUPER, gps, B_GROUP_DIM, B_STATE)
        blk = jnp.einsum('jghp,gk->jkpgh', cm, eye)
        return blk.reshape(S5_SUPER, gps * B_STATE, gps * B_GROUP_DIM)

    wb = jnp.concatenate([in_block(bb_re), in_block(bb_im)], axis=2).astype(BF16)
    wc = jnp.concatenate([out_block(c_re), out_block(-c_im)], axis=1).astype(BF16)
    lam_rows = jnp.stack([
        jnp.concatenate([_s5_pack_cols(bar_re)] * 2, axis=-1).reshape(S5_STATE_W),
        jnp.concatenate([_s5_pack_cols(bar_im)] * 2, axis=-1).reshape(S5_STATE_W)])
    return wb, wc, lam_rows


def s5_pack_state(x_re, x_im):
    bsz = x_re.shape[0]
    return jnp.concatenate([_s5_pack_cols(x_re), _s5_pack_cols(x_im)], axis=-1).reshape(bsz, S5_STATE_W)


def s5_unpack_state(x):
    bsz = x.shape[0]
    x = x.reshape(bsz, S5_SUPER, 2, S5_HALF)
    return (x[:, :, 0].reshape(bsz, B_GROUPS, B_STATE), x[:, :, 1].reshape(bsz, B_GROUPS, B_STATE))


def _gmlp_kernel(u_ref, v_ref, gn_ref, ws_ref, bs_ref, o_ref, vr_ref, *, tt, chunk):
    n_chunks = tt // chunk
    u = jax.nn.gelu(u_ref[...].astype(F32))
    v = jax.nn.gelu(v_ref[...].astype(F32))
    vn = _rms(v, C_WIDTH) * gn_ref[...]
    vn_bf = vn.astype(BF16)
    row = lax.broadcasted_iota(jnp.int32, (chunk, chunk), 0)
    col = lax.broadcasted_iota(jnp.int32, (chunk, chunk), 1)
    keep = col <= row
    for g in range(C_GROUPS):
        w = jnp.where(keep, ws_ref[g], 0.0).astype(BF16)
        bias = bs_ref[:, g:g + 1]
        cs = slice(g * C_GROUP_DIM, (g + 1) * C_GROUP_DIM)
        for c in range(n_chunks):
            rs = slice(c * chunk, (c + 1) * chunk)
            s = _dot(w, vn_bf[rs, cs]) + bias
            o_ref[rs, cs] = (u[rs, cs] * s).astype(o_ref.dtype)

    @pl.when(pl.program_id(1) == pl.num_programs(1) - 1)
    def _():
        vr_ref[...] = vn[tt - chunk:, :]


def gmlp(proj, v_norm, w_s, b_s, tt_cap=1024):
    bsz, t, _ = proj.shape
    chunk = min(t, C_CHUNK)
    tt = _row_tile(t, tt_cap)
    ws = w_s[:, :chunk, :chunk]
    bs_t = b_s[:, :chunk].T
    kern = functools.partial(_gmlp_kernel, tt=tt, chunk=chunk)
    return pl.pallas_call(
        kern,
        grid=(bsz, t // tt),
        in_specs=[
            pl.BlockSpec((None, tt, C_WIDTH), lambda b, i: (b, i, 0)),
            pl.BlockSpec((None, tt, C_WIDTH), lambda b, i: (b, i, 1)),
            pl.BlockSpec((1, C_WIDTH), lambda b, i: (0, 0)),
            pl.BlockSpec((C_GROUPS, chunk, chunk), lambda b, i: (0, 0, 0)),
            pl.BlockSpec((chunk, C_GROUPS), lambda b, i: (0, 0)),
        ],
        out_specs=[
            pl.BlockSpec((None, tt, C_WIDTH), lambda b, i: (b, i, 0)),
            pl.BlockSpec((None, chunk, C_WIDTH), lambda b, i: (b, 0, 0)),
        ],
        out_shape=[
            jax.ShapeDtypeStruct((bsz, t, C_WIDTH), BF16),
            jax.ShapeDtypeStruct((bsz, chunk, C_WIDTH), F32),
        ],
        compiler_params=_cparams(("parallel", "arbitrary")),
        name="gmlp",
    )(proj, proj, v_norm.reshape(1, C_WIDTH), ws, bs_t)


def _rope_lanes(x, cos_t, sin_lo, sin_hi):
    half = ROPE_DIM // 2
    return (x * cos_t + pltpu.roll(x, LANES - half, axis=1) * sin_lo
            + pltpu.roll(x, half, axis=1) * sin_hi)


def _mla_q_kernel(cq_ref, ckv_ref, kpe_ref, qn_ref, kvn_ref, wuq_ref, qg_ref,
                  qc_ref, qs_ref, kc_ref, ks1_ref, ks2_ref,
                  q_ref, ckv_o_ref, kpe_o_ref, ckpe_ref):
    cqn = (_rms(cq_ref[...].astype(F32), Q_LORA) * qn_ref[...]).astype(BF16)
    width = D_HEADS * HEAD_PAD
    q_all = _dot(cqn, wuq_ref[...])
    qc, qs = qc_ref[...], qs_ref[...]
    scale = QK_DIM ** -0.5 * math.log2(math.e)
    for h in range(D_HEADS):
        sl = slice(h * HEAD_PAD, (h + 1) * HEAD_PAD)
        rot = slice(width + h * HEAD_PAD, width + (h + 1) * HEAD_PAD)
        qh = q_all[:, sl] * qc + q_all[:, rot] * qs
        qh = _rms(qh, QK_DIM) * (qg_ref[...] * scale)
        q_ref[:, sl] = qh.astype(q_ref.dtype)
    ckv = _rms(ckv_ref[...].astype(F32), KV_LORA) * kvn_ref[...]
    kpe = _rope_lanes(kpe_ref[...].astype(F32), kc_ref[...], ks1_ref[...], ks2_ref[...])
    ckv_o_ref[...] = ckv
    kpe_o_ref[...] = kpe[:, :ROPE_DIM]
    ckpe_ref[:, :KV_LORA] = ckv.astype(BF16)
    ckpe_ref[:, KV_LORA:] = kpe.astype(BF16)


def mla_q(proj, tables, q_norm, kv_norm, wuq_p, q_gain_p, tm_cap=1024):
    bsz, t, _ = proj.shape
    tm = _row_tile(t, tm_cap)
    tab = pl.BlockSpec((tm, LANES), lambda b, i: (i, 0))
    row = lambda w: pl.BlockSpec((1, w), lambda b, i: (0, 0))
    return pl.pallas_call(
        _mla_q_kernel,
        grid=(bsz, t // tm),
        in_specs=[
            pl.BlockSpec((None, tm, Q_LORA), lambda b, i: (b, i, 2 * C_WIDTH // Q_LORA)),
            pl.BlockSpec((None, tm, KV_LORA), lambda b, i: (b, i, (2 * C_WIDTH + Q_LORA) // KV_LORA)),
            pl.BlockSpec((None, tm, LANES), lambda b, i: (b, i, (2 * C_WIDTH + Q_LORA + KV_LORA) // LANES)),
            row(Q_LORA), row(KV_LORA),
            pl.BlockSpec((Q_LORA, 2 * D_HEADS * HEAD_PAD), lambda b, i: (0, 0)),
            row(HEAD_PAD),
            tab, tab, tab, tab, tab,
        ],
        out_specs=[
            pl.BlockSpec((None, tm, D_HEADS * HEAD_PAD), lambda b, i: (b, i, 0)),
            pl.BlockSpec((None, tm, KV_LORA), lambda b, i: (b, i, 0)),
            pl.BlockSpec((None, tm, ROPE_DIM), lambda b, i: (b, i, 0)),
            pl.BlockSpec((None, tm, KV_LORA + LANES), lambda b, i: (b, i, 0)),
        ],
        out_shape=[
            jax.ShapeDtypeStruct((bsz, t, D_HEADS * HEAD_PAD), BF16),
            jax.ShapeDtypeStruct((bsz, t, KV_LORA), F32),
            jax.ShapeDtypeStruct((bsz, t, ROPE_DIM), F32),
            jax.ShapeDtypeStruct((bsz, t, KV_LORA + LANES), BF16),
        ],
        compiler_params=_cparams(("parallel", "parallel")),
        name="mla_q",
    )(proj, proj, proj, q_norm.reshape(1, Q_LORA), kv_norm.reshape(1, KV_LORA), wuq_p,
      q_gain_p, *tables)


def _mla_kv_kernel(ckpe_ref, wk_ref, wvt_ref, kg_ref, k_ref, vt_ref):
    ckpe = ckpe_ref[...]
    k_all = _dot(ckpe, wk_ref[...])
    for h in range(D_HEADS):
        sl = slice(h * HEAD_PAD, (h + 1) * HEAD_PAD)
        k_ref[:, sl] = (_rms(k_all[:, sl], QK_DIM) * kg_ref[...]).astype(k_ref.dtype)
    row = lax.broadcasted_iota(jnp.int32, (D_HEADS * HEAD_PAD, 1), 0)
    ones_row = jnp.where(_mod_pow2(row, HEAD_PAD) == V_ONE_ROW, 1.0, 0.0)
    vt_ref[...] = (_dot_nt(wvt_ref[...], ckpe[:, :KV_LORA]) + ones_row).astype(vt_ref.dtype)


def mla_kv(ckpe, wk_full, wv_p, k_gain_p):
    bsz, s, kw = ckpe.shape
    tm = _row_tile(s, ATTN_TK)
    width = D_HEADS * HEAD_PAD
    full = lambda r, c: pl.BlockSpec((r, c), lambda b, i: (0, 0))
    return pl.pallas_call(
        _mla_kv_kernel,
        grid=(bsz, s // tm),
        in_specs=[
            pl.BlockSpec((None, tm, kw), lambda b, i: (b, i, 0)),
            full(kw, width), full(width, KV_LORA), full(1, HEAD_PAD),
        ],
        out_specs=[
            pl.BlockSpec((None, tm, width), lambda b, i: (b, i, 0)),
            pl.BlockSpec((None, None, width, tm), lambda b, i: (b, i, 0, 0)),
        ],
        out_shape=[
            jax.ShapeDtypeStruct((bsz, s, width), BF16),
            jax.ShapeDtypeStruct((bsz, s // tm, width, tm), BF16),
        ],
        compiler_params=_cparams(("parallel", "parallel")),
        name="mla_kv",
    )(ckpe, wk_full, wv_p.T, k_gain_p)


def _for_blocks(lo, hi, block_fn):
    n_pairs = (hi - lo) // 2

    def pair(i, carry):
        block_fn(lo + 2 * i)
        block_fn(lo + 2 * i + 1)
        return carry

    lax.fori_loop(0, n_pairs, pair, 0)

    @pl.when(lo + 2 * n_pairs < hi)
    def _():
        block_fn(hi - 1)


def _attn_kernel(q_ref, k_ref, vt_ref, o_ref, sc_ref, mx_ref, acc_ref, *, tq, tk, s_len, q_pos0):
    qi = pl.program_id(1)
    q_first = q_pos0 + qi * tq
    q_last = q_first + tq - 1
    vis_first = jnp.minimum(s_len, (q_first // CHUNK + 1) * CHUNK)
    vis_last = jnp.minimum(s_len, (q_last // CHUNK + 1) * CHUNK)
    n_full = vis_first // tk
    n_kv = (vis_last + tk - 1) // tk

    q_chunk = _div_pow2(q_first + lax.broadcasted_iota(jnp.int32, (tk, tq), 1), CHUNK)
    k_iota = lax.broadcasted_iota(jnp.int32, (tk, tq), 0)

    for g0 in range(0, D_HEADS, ATTN_HEAD_GROUP):
        heads = [slice(h * HEAD_PAD, (h + 1) * HEAD_PAD) for h in range(g0, g0 + ATTN_HEAD_GROUP)]

        def score_block(j, masked, heads=heads):
            k0 = pl.multiple_of(j * tk, tk)
            visible = (_div_pow2(k0 + k_iota, CHUNK) <= q_chunk) if masked else None
            for h, sl in enumerate(heads):
                s = _dot_nt(k_ref[pl.ds(k0, tk), sl], q_ref[:, sl])
                if masked:
                    s = jnp.where(visible, s, NEG_BIG)
                sc_ref[h, j] = s
                blk_max = jnp.max(s.reshape(tk // SUBLANES, SUBLANES, tq), axis=0)
                mx_ref[h] = jnp.maximum(mx_ref[h], blk_max)

        def sum_block(j, heads=heads):
            for h, sl in enumerate(heads):
                p = jnp.exp2(sc_ref[h, j] - mx_ref[h][0:1, :])
                acc_ref[h] += _dot(vt_ref[j, sl, :], p.astype(BF16))

        mx_ref[...] = jnp.full_like(mx_ref, NEG_BIG)
        _for_blocks(0, n_full, functools.partial(score_block, masked=False))
        _for_blocks(n_full, n_kv, functools.partial(score_block, masked=True))
        for h in range(ATTN_HEAD_GROUP):
            m = jnp.max(mx_ref[h], axis=0, keepdims=True)
            mx_ref[h] = jnp.broadcast_to(m, (SUBLANES, tq))
        acc_ref[...] = jnp.zeros_like(acc_ref)
        _for_blocks(0, n_kv, sum_block)
        for h, sl in enumerate(heads):
            acc = acc_ref[h]
            o_ref[:, sl] = (acc / acc[V_ONE_ROW:V_ONE_ROW + 1, :]).T.astype(o_ref.dtype)


def attention(q, k, vt, q_pos0, tq_cap=512):
    bsz, tq_len, width = q.shape
    s_len = k.shape[1]
    n_blocks, _, tk = vt.shape[1:]
    assert n_blocks * tk == s_len and tk % LANES == 0
    tq = _row_tile(tq_len, tq_cap)
    kern = functools.partial(_attn_kernel, tq=tq, tk=tk, s_len=s_len, q_pos0=q_pos0)
    return pl.pallas_call(
        kern,
        grid=(bsz, tq_len // tq),
        in_specs=[
            pl.BlockSpec((None, tq, width), lambda b, i: (b, i, 0)),
            pl.BlockSpec((None, s_len, width), lambda b, i: (b, 0, 0)),
            pl.BlockSpec((None, n_blocks, width, tk), lambda b, i: (b, 0, 0, 0)),
        ],
        out_specs=pl.BlockSpec((None, tq, width), lambda b, i: (b, i, 0)),
        out_shape=jax.ShapeDtypeStruct((bsz, tq_len, width), BF16),
        scratch_shapes=[pltpu.VMEM((ATTN_HEAD_GROUP, n_blocks, tk, tq), F32),
                        pltpu.VMEM((ATTN_HEAD_GROUP, SUBLANES, tq), F32),
                        pltpu.VMEM((ATTN_HEAD_GROUP, HEAD_PAD, tq), F32)],
        compiler_params=_cparams(("parallel", "arbitrary")),
        name="attention",
    )(q, k, vt)


def _mla_decode_kernel(q_ref, ckv_c_ref, kpe_c_ref, ckv_n_ref, kpe_n_ref, wk_ref, wkt_ref, wv_ref, kg_ref,
                       o_ref, *, tq, s_len, q_pos0):
    s_pad = ckv_c_ref.shape[0] + ckv_n_ref.shape[0]
    heads = [slice(h * HEAD_PAD, (h + 1) * HEAD_PAD) for h in range(D_HEADS)]
    ckv = jnp.concatenate([ckv_c_ref[...], ckv_n_ref[...]], axis=0).astype(BF16)
    kpe_raw = jnp.concatenate([kpe_c_ref[...], kpe_n_ref[...]], axis=0).astype(BF16)
    lane = lax.broadcasted_iota(jnp.int32, (ROPE_DIM, LANES), 1)
    src = lax.broadcasted_iota(jnp.int32, (ROPE_DIM, LANES), 0)
    place = jnp.where(lane == src + NOPE_DIM, 1.0, 0.0).astype(BF16)
    kpe = _dot(kpe_raw, place).astype(BF16)
    eye = (lax.broadcasted_iota(jnp.int32, (LANES, LANES), 0)
           == lax.broadcasted_iota(jnp.int32, (LANES, LANES), 1))
    kpe_t = _dot_nt(jnp.where(eye, 1.0, 0.0).astype(BF16), kpe)
    pe_ss = jnp.sum(kpe_t * kpe_t, axis=0, keepdims=True)
    kn_t = _dot_nt(wkt_ref[...], ckv)
    inv_rms, q_lat, q_g = [], [], []
    for sl in heads:
        blk = kn_t[sl, :]
        ss = jnp.sum(blk * blk, axis=0, keepdims=True) + pe_ss
        inv_rms.append(jnp.broadcast_to(lax.rsqrt(ss * (1.0 / QK_DIM) + EPS), (tq, s_pad)))
        qg = (q_ref[:, sl].astype(F32) * kg_ref[...]).astype(BF16)
        q_g.append(qg)
        q_lat.append(_dot_nt(qg, wk_ref[:, sl]).astype(BF16))
    s = _dot_nt(jnp.concatenate(q_lat, axis=0), ckv) + _dot_nt(jnp.concatenate(q_g, axis=0), kpe)
    s = s * jnp.concatenate(inv_rms, axis=0)
    row = lax.broadcasted_iota(jnp.int32, s.shape, 0)
    col = lax.broadcasted_iota(jnp.int32, s.shape, 1)
    q_chunk = _div_pow2(q_pos0 + _mod_pow2(row, tq), CHUNK)
    visible = (_div_pow2(col, CHUNK) <= q_chunk) & (col < s_len)
    s = jnp.where(visible, s, NEG_BIG)
    p = jnp.exp2(s - jnp.max(s, axis=-1, keepdims=True))
    lat = _dot(p.astype(BF16), ckv) / jnp.sum(p, axis=-1, keepdims=True)
    for h, sl in enumerate(heads):
        o_ref[:, sl] = _dot(lat[h * tq:(h + 1) * tq].astype(BF16), wv_ref[:, sl]).astype(o_ref.dtype)


def mla_decode(q, cache_ckv, cache_kpe, ckv_new, kpe_new, wk_p, wv_p, k_gain_p, q_pos0):
    bsz, tq, width = q.shape
    n_cache, n_new = cache_ckv.shape[1], ckv_new.shape[1]
    assert n_cache % LANES == 0 and tq & (tq - 1) == 0
    s_len = n_cache + n_new
    new_pad = -(-n_new // LANES) * LANES
    ckv_new = jnp.pad(ckv_new, ((0, 0), (0, new_pad - n_new), (0, 0)))
    kpe_new = jnp.pad(kpe_new, ((0, 0), (0, new_pad - n_new), (0, 0)))
    full = lambda r, c: pl.BlockSpec((r, c), lambda b: (0, 0))
    per_b = lambda r, c: pl.BlockSpec((None, r, c), lambda b: (b, 0, 0))
    kern = functools.partial(_mla_decode_kernel, tq=tq, s_len=s_len, q_pos0=q_pos0)
    return pl.pallas_call(
        kern,
        grid=(bsz,),
        in_specs=[
            per_b(tq, width), per_b(n_cache, KV_LORA), per_b(n_cache, ROPE_DIM),
            per_b(new_pad, KV_LORA), per_b(new_pad, ROPE_DIM),
            full(KV_LORA, width), full(width, KV_LORA), full(KV_LORA, width), full(1, HEAD_PAD),
        ],
        out_specs=per_b(tq, width),
        out_shape=jax.ShapeDtypeStruct((bsz, tq, width), BF16),
        compiler_params=_cparams(("parallel",)),
        name="mla_decode",
    )(q, cache_ckv, cache_kpe, ckv_new, kpe_new, wk_p, wk_p.T, wv_p, k_gain_p)


def rope_tables(pos):
    half = ROPE_DIM // 2
    inv = ROPE_BASE ** (-jnp.arange(half, dtype=F32) / half)
    ang = pos.astype(F32)[:, None] * inv[None, :]
    cos, sin = jnp.cos(ang), jnp.sin(ang)
    n = pos.shape[0]
    z = lambda w: jnp.zeros((n, w), F32)
    o = lambda w: jnp.ones((n, w), F32)
    tail = LANES - NOPE_DIM - ROPE_DIM
    q_cos = jnp.concatenate([o(NOPE_DIM), cos, cos, z(tail)], axis=1)
    q_sin = jnp.concatenate([z(NOPE_DIM), sin, sin, z(tail)], axis=1)
    k_cos = jnp.concatenate([cos, cos, z(LANES - ROPE_DIM)], axis=1)
    k_s1 = jnp.concatenate([-sin, z(LANES - half)], axis=1)
    k_s2 = jnp.concatenate([z(half), sin, z(LANES - ROPE_DIM)], axis=1)
    return q_cos, q_sin, k_cos, k_s1, k_s2


def pack_mla_weights(w_uq, w_ukv, q_gain, k_gain, cd_w_out):
    pad_q = HEAD_PAD - QK_DIM
    half = ROPE_DIM // 2
    wq = w_uq.reshape(Q_LORA, D_HEADS, QK_DIM)
    wuq_p = jnp.pad(wq, ((0, 0), (0, 0), (0, pad_q))).reshape(Q_LORA, D_HEADS * HEAD_PAD)
    wq_rot = jnp.concatenate([jnp.zeros_like(wq[:, :, :NOPE_DIM]), -wq[:, :, NOPE_DIM + half:],
                              wq[:, :, NOPE_DIM:NOPE_DIM + half], jnp.zeros_like(wq[:, :, :pad_q])], axis=2)
    wuq_p = jnp.concatenate([wuq_p, wq_rot.reshape(Q_LORA, D_HEADS * HEAD_PAD)], axis=1).astype(BF16)
    wkv = w_ukv.reshape(KV_LORA, D_HEADS, NOPE_DIM + V_DIM)
    wk_p = jnp.pad(wkv[:, :, :NOPE_DIM], ((0, 0), (0, 0), (0, HEAD_PAD - NOPE_DIM)))
    wk_p = wk_p.reshape(KV_LORA, D_HEADS * HEAD_PAD).astype(BF16)
    wv_p = jnp.pad(wkv[:, :, NOPE_DIM:], ((0, 0), (0, 0), (0, HEAD_PAD - V_DIM)))
    wv_p = wv_p.reshape(KV_LORA, D_HEADS * HEAD_PAD).astype(BF16)
    place = jnp.pad(jnp.eye(ROPE_DIM, dtype=F32), ((0, LANES - ROPE_DIM), (NOPE_DIM, HEAD_PAD - QK_DIM)))
    wk_full = jnp.concatenate([wk_p, jnp.tile(place, (1, D_HEADS)).astype(BF16)], axis=0)
    q_gain_p = jnp.pad(q_gain, (0, pad_q)).reshape(1, HEAD_PAD)
    k_gain_p = jnp.pad(k_gain, (0, pad_q)).reshape(1, HEAD_PAD)
    w_c = cd_w_out[:C_WIDTH].astype(BF16)
    w_d = cd_w_out[C_WIDTH:].reshape(D_HEADS, V_DIM, D_MODEL)
    w_d_p = jnp.pad(w_d, ((0, 0), (0, HEAD_PAD - V_DIM), (0, 0))).reshape(D_HEADS * HEAD_PAD, D_MODEL)
    return wuq_p, wk_p, wk_full, wv_p, q_gain_p, k_gain_p, w_c, w_d_p.astype(BF16)


def _trunk(x, pos0, w, past):
    bsz, t, _ = x.shape
    n = bsz * t
    x2 = x.reshape(n, D_MODEL)
    pos = pos0 + jnp.arange(t, dtype=jnp.int32)
    lb_all = jnp.cumsum(jax.nn.softmax(w['hgrn_lb_logits'], axis=0), axis=0)
    outs = {}
    ffn_w = lambda name, l: (w[name + '_norm'][l], w[name + '_w_gate'][l], w[name + '_w_up'][l],
                             w[name + '_w_down'][l])
    tmaj = (bsz, t) if t % FFN_ROWS == 0 else None
    for l in range(2):
        w_in = w['ab_w_in'] if l == 0 else w['cd_w_in_p']
        res = ffn(x2, *ffn_w('ffn1', l), post=(w['mix_norm'][l], w_in), time_major=tmaj if l == 0 else None)
        x2, proj = res[0], res[1].reshape(bsz, t, -1)
        if l == 0:
            s0 = None if past is None else past[0][0]
            a_out, s_t = hgrn2(proj, lb_all[0], w['hgrn_out_norm'][0], s0)
            if past is None:
                x0 = jnp.zeros((bsz, S5_STATE_W), F32)
            else:
                x0 = s5_pack_state(past[1][0], past[2][0])
            if tmaj is not None:
                u_tb = res[2].reshape(t, bsz, B_WIDTH)
            else:
                u_tb = jnp.transpose(proj[:, :, AB_IN - B_WIDTH:], (1, 0, 2))
            b_tb, x_t = s5_mix(u_tb, w['s5_wb'], w['s5_wc'], w['s5_lam'], w['s5_d'],
                               w['s5_w_glu'], w['s5_b_glu'], x0)
            if tmaj is not None:
                b_out = b_tb.reshape(t, bsz * B_WIDTH)
            else:
                b_out = jnp.transpose(b_tb, (1, 0, 2)).reshape(n, B_WIDTH)
            pre = (a_out.reshape(n, A_WIDTH), b_out, w['ab_w_out_a'], w['ab_w_out_b'])
            outs['hgrn'] = s_t
            outs['s5re'], outs['s5im'] = s5_unpack_state(x_t)
        else:
            c_out, v_rows = gmlp(proj, w['gmlp_v_norm'], w['gmlp_w_s'], w['gmlp_b_s'])
            q, ckv, kpe, ckpe = mla_q(proj, rope_tables(pos), w['mla_q_norm'], w['mla_kv_norm'],
                                      w['mla_wuq_p'], w['mla_q_gain_p'])
            if past is None:
                k, v = mla_kv(ckpe, w['mla_wk_full'], w['mla_wv_p'], w['mla_k_gain_p'])
                d_out = attention(q, k, v, pos0)
            else:
                d_out = mla_decode(q, past[3][0], past[4][0], ckv, kpe, w['mla_wk_p'], w['mla_wv_p'],
                                   w['mla_k_gain_p'], pos0)
            pre = (c_out.reshape(n, C_WIDTH), d_out.reshape(n, D_HEADS * HEAD_PAD),
                   w['cd_w_out_c'], w['cd_w_out_d'])
            outs['gv'], outs['ckv'], outs['kpe'] = v_rows, ckv, kpe
        x2 = ffn(x2, *ffn_w('ffn2', l), pre=pre, time_major=tmaj if l == 0 else None)
    return x2.reshape(bsz, t, D_MODEL), outs


def kernel(x_prompt, x_sample, state_hgrn, state_s5_re, state_s5_im, cache_mla_ckv, cache_mla_kpe, ffn1_norm, ffn1_w_gate, ffn1_w_up, ffn1_w_down, mix_norm, ffn2_norm, ffn2_w_gate, ffn2_w_up, ffn2_w_down, ab_w_in, hgrn_lb_logits, hgrn_out_norm, s5_lambda_re, s5_lambda_im, s5_log_dt, s5_b_re, s5_b_im, s5_c_re, s5_c_im, s5_d, s5_w_glu, s5_b_glu, ab_w_out, cd_w_in, gmlp_v_norm, gmlp_w_s, gmlp_b_s, mla_q_norm, mla_w_uq, mla_kv_norm, mla_w_ukv, mla_q_gain, mla_k_gain, cd_w_out):
    bf = lambda a: a.astype(BF16)
    wb, wc, lam_rows = s5_prepare(s5_lambda_re[0], s5_lambda_im[0], s5_log_dt[0],
                                  s5_b_re[0], s5_b_im[0], s5_c_re[0], s5_c_im[0])
    wuq_p, wk_p, wk_full, wv_p, q_gain_p, k_gain_p, w_c, w_d_p = pack_mla_weights(
        mla_w_uq[0], mla_w_ukv[0], mla_q_gain[0], mla_k_gain[0], cd_w_out[0])
    w = dict(
        ffn1_norm=ffn1_norm, ffn1_w_gate=bf(ffn1_w_gate), ffn1_w_up=bf(ffn1_w_up), ffn1_w_down=bf(ffn1_w_down),
        ffn2_norm=ffn2_norm, ffn2_w_gate=bf(ffn2_w_gate), ffn2_w_up=bf(ffn2_w_up), ffn2_w_down=bf(ffn2_w_down),
        mix_norm=mix_norm, ab_w_in=bf(ab_w_in[0]), hgrn_lb_logits=hgrn_lb_logits, hgrn_out_norm=hgrn_out_norm,
        s5_wb=wb, s5_wc=wc, s5_lam=lam_rows, s5_d=s5_d[0].reshape(B_WIDTH), s5_w_glu=bf(s5_w_glu[0]),
        s5_b_glu=s5_b_glu[0], ab_w_out_a=bf(ab_w_out[0, :A_WIDTH]), ab_w_out_b=bf(ab_w_out[0, A_WIDTH:]),
        cd_w_in_p=bf(jnp.pad(cd_w_in[0], ((0, 0), (0, CD_IN_PAD - CD_IN)))),
        gmlp_v_norm=gmlp_v_norm[0], gmlp_w_s=gmlp_w_s[0], gmlp_b_s=gmlp_b_s[0],
        mla_q_norm=mla_q_norm[0], mla_kv_norm=mla_kv_norm[0], mla_wuq_p=wuq_p, mla_q_gain_p=q_gain_p,
        mla_wk_p=wk_p, mla_wk_full=wk_full, mla_wv_p=wv_p, mla_k_gain_p=k_gain_p,
        cd_w_out_c=w_c, cd_w_out_d=w_d_p,
    )
    past_len = cache_mla_ckv.shape[2]
    y_p, o_p = _trunk(x_prompt, 0, w, None)
    y_s, o_s = _trunk(x_sample, past_len, w,
                      (state_hgrn, state_s5_re, state_s5_im, cache_mla_ckv, cache_mla_kpe))
    e = lambda a: a[None]
    return (y_p, y_s, e(o_p['hgrn']), e(o_s['hgrn']), e(o_p['s5re']), e(o_p['s5im']),
            e(o_s['s5re']), e(o_s['s5im']), e(o_p['gv']), e(o_s['gv']),
            e(o_p['ckv']), e(o_p['kpe']), e(o_s['ckv']), e(o_s['kpe']))
```

```python
import functools
import math

import jax
import jax.numpy as jnp
from jax import lax
from jax.experimental import pallas as pl
from jax.experimental.pallas import tpu as pltpu

D_MODEL = 1024
D_FF = 2816
EPS = 1e-6
A_HEADS = 4
A_KEY = 128
A_VAL = 128
A_WIDTH = A_HEADS * A_VAL
HGRN_BLOCK = 64
B_GROUP_DIM = 16
B_WIDTH = 512
B_GROUPS = 32
B_STATE = 64
LAMBDA_RE_CEIL = -1e-4
C_CHUNK = 128
C_WIDTH = 512
C_GROUPS = 4
C_GROUP_DIM = 128
D_HEADS = 8
Q_LORA = 256
KV_LORA = 128
NOPE_DIM = 64
ROPE_DIM = 32
V_DIM = 64
QK_DIM = NOPE_DIM + ROPE_DIM
ROPE_BASE = 10000.0
CHUNK = 64
A_QK = A_HEADS * A_KEY
AB_IN = 2 * A_QK + 2 * A_WIDTH + B_WIDTH
CD_IN = 2 * C_WIDTH + Q_LORA + KV_LORA + ROPE_DIM
CD_IN_PAD = 1536

LANES = 128
SUBLANES = 8
VMEM_LIMIT = 56 * 1024 * 1024
HGRN_SUB = 16
HGRN_STEP_ROWS = 128
HEAD_PAD = LANES
NEG_BIG = -1e30
EXP_CLAMP = 80.0
V_ONE_ROW = V_DIM
ATTN_TK = 512
ATTN_HEAD_GROUP = 4

BF16 = jnp.bfloat16
F32 = jnp.float32


def _cparams(sem):
    return pltpu.CompilerParams(dimension_semantics=sem, vmem_limit_bytes=VMEM_LIMIT)


def _rms(xf, width):
    return xf * lax.rsqrt(jnp.sum(xf * xf, axis=-1, keepdims=True) * (1.0 / width) + EPS)


def _dot(a, b):
    return jnp.dot(a, b, preferred_element_type=F32)


def _dot_nt(a, b):
    return lax.dot_general(a, b, (((1,), (1,)), ((), ())), preferred_element_type=F32)


def _dot_tn(a, b):
    return lax.dot_general(a, b, (((0,), (0,)), ((), ())), preferred_element_type=F32)


def _div_pow2(x, d):
    assert d & (d - 1) == 0
    return lax.shift_right_logical(x, jnp.int32(d.bit_length() - 1))


def _mod_pow2(x, d):
    assert d & (d - 1) == 0
    return x & jnp.int32(d - 1)


def _row_tile(n, cap):
    t = min(n, cap)
    assert n % t == 0, (n, t)
    return t


FFN_CHUNK = 256
POST_CHUNK = 512
FFN_ROWS = 512


def _ffn_rows(x_ref, pre_refs, w_refs, post_refs, o_ref, p_ref, tail_ref, acc_ref):
    g_ref, wg_ref, wu_ref, wd_ref = w_refs
    rows = x_ref.shape[0]
    x = x_ref[...]
    if pre_refs is not None:
        a1_ref, a2_ref, w1_ref, w2_ref = pre_refs
        x = x + _dot(a1_ref[...], w1_ref[...]) + _dot(a2_ref[...], w2_ref[...])
    h = (_rms(x, D_MODEL) * g_ref[...]).astype(BF16)
    for j in range(D_FF // FFN_CHUNK):
        cs = slice(j * FFN_CHUNK, (j + 1) * FFN_CHUNK)
        gate = _dot(h, wg_ref[:, cs])
        up = _dot(h, wu_ref[:, cs])
        act = (gate * jax.nn.sigmoid(gate) * up).astype(BF16)
        part = _dot(act, wd_ref[cs, :])
        if j == 0:
            acc_ref[:rows] = part
        else:
            acc_ref[:rows] += part
    out = x + 0.5 * acc_ref[:rows]
    o_ref[...] = out
    if post_refs is not None:
        g2_ref, wp_ref = post_refs
        h2 = (_rms(out, D_MODEL) * g2_ref[...]).astype(BF16)
        n_post = wp_ref.shape[1] // POST_CHUNK
        for c in range(n_post):
            cs = slice(c * POST_CHUNK, (c + 1) * POST_CHUNK)
            part = _dot(h2, wp_ref[:, cs]).astype(p_ref.dtype)
            if tail_ref is not None and c == n_post - 1:
                tail_ref[...] = part
            else:
                p_ref[:, cs] = part


def _ffn_kernel(*refs, tiles, has_pre, has_post, split_tails):
    refs = list(refs)
    take = lambda k: [refs.pop(0) for _ in range(k)]
    group_in = [take(3 if has_pre else 1) for _ in tiles]
    pre_w = take(2) if has_pre else None
    w_refs = take(4)
    post_refs = take(2) if has_post else None
    group_out = [take(1 + has_post + split) for split in split_tails]
    acc_ref, = refs
    step = pl.program_id(0)
    first = 0
    for n_tiles, ins, outs, split in zip(tiles, group_in, group_out, split_tails):
        pre_refs = (ins[1], ins[2], *pre_w) if has_pre else None
        p_ref = outs[1] if has_post else None
        tail_ref = outs[2] if split else None

        @pl.when((step >= first) & (step < first + n_tiles))
        def _(ins=ins, outs=outs, pre_refs=pre_refs, p_ref=p_ref, tail_ref=tail_ref):
            _ffn_rows(ins[0], pre_refs, w_refs, post_refs, outs[0], p_ref, tail_ref, acc_ref)

        first += n_tiles


def ffn(groups, g, wg, wu, wd, pre_w=None, post=None, tm_cap=FFN_ROWS):
    fixed = lambda r, c: pl.BlockSpec((r, c), lambda i: (0, 0), pipeline_mode=pl.Buffered(1))
    args, specs, out_specs, out_shape, tiles, split_tails = [], [], [], [], [], []
    first = 0
    for gi, grp in enumerate(groups):
        x = grp['x']
        n = x.shape[0]
        tm = _row_tile(n, tm_cap)
        n_tiles = n // tm
        local = lambda i, first=first, n_tiles=n_tiles: jnp.clip(i - first, 0, n_tiles - 1)
        mode = {} if gi == 0 else dict(pipeline_mode=pl.Buffered(1))
        rows = lambda w, local=local, tm=tm, mode=mode: pl.BlockSpec((tm, w), lambda i: (local(i), 0), **mode)
        tmaj = grp.get('time_major')
        if tmaj is not None:
            streams, steps = tmaj
            assert steps % tm == 0 and streams * steps == n
            per_stream = steps // tm
            by_time = lambda w, local=local, tm=tm, per_stream=per_stream: pl.BlockSpec(
                (tm, w), lambda i: (local(i) % per_stream, local(i) // per_stream))
        args.append(x)
        specs.append(rows(D_MODEL))
        if pre_w is not None:
            args += [grp['a1'], grp['a2']]
            specs += [rows(grp['a1'].shape[1]),
                      by_time(pre_w[1].shape[0]) if tmaj is not None else rows(grp['a2'].shape[1])]
        out_specs.append(rows(D_MODEL))
        out_shape.append(jax.ShapeDtypeStruct((n, D_MODEL), F32))
        split = post is not None and tmaj is not None
        if post is not None:
            head_w = post[1].shape[1] - (POST_CHUNK if split else 0)
            out_specs.append(rows(head_w))
            out_shape.append(jax.ShapeDtypeStruct((n, head_w), BF16))
            if split:
                out_specs.append(by_time(POST_CHUNK))
                out_shape.append(jax.ShapeDtypeStruct((steps, streams * POST_CHUNK), BF16))
        tiles.append(n_tiles)
        split_tails.append(split)
        first += n_tiles
    if pre_w is not None:
        args += list(pre_w)
        specs += [fixed(*pre_w[0].shape), fixed(*pre_w[1].shape)]
    args += [g.reshape(1, D_MODEL), wg, wu, wd]
    specs += [fixed(1, D_MODEL), fixed(D_MODEL, D_FF), fixed(D_MODEL, D_FF), fixed(D_FF, D_MODEL)]
    if post is not None:
        assert post[1].shape[1] % POST_CHUNK == 0
        args += [post[0].reshape(1, D_MODEL), post[1]]
        specs += [fixed(1, D_MODEL), fixed(*post[1].shape)]
    kern = functools.partial(_ffn_kernel, tiles=tuple(tiles), has_pre=pre_w is not None,
                             has_post=post is not None, split_tails=tuple(split_tails))
    res = pl.pallas_call(
        kern,
        grid=(first,),
        in_specs=specs,
        out_specs=out_specs,
        out_shape=out_shape,
        scratch_shapes=[pltpu.VMEM((max(min(tm_cap, grp['x'].shape[0]) for grp in groups), D_MODEL), F32)],
        compiler_params=_cparams(("arbitrary",)),
        name="ffn",
    )(*args)
    out, k = [], 0
    for split in split_tails:
        width = 1 + (post is not None) + split
        out.append(tuple(res[k:k + width]))
        k += width
    return out


def _hgrn_kernel(q_ref, f_ref, i_ref, g_ref, lb_ref, og_ref, s0_ref, y_ref, st_ref, state_ref,
                 *, tt, blk, has_past):
    for si in range(q_ref.shape[0]):
        _hgrn_stream(q_ref.at[si], f_ref.at[si], i_ref.at[si], g_ref.at[si], lb_ref, og_ref, s0_ref.at[si],
                     y_ref.at[si], st_ref.at[si], state_ref.at[si], tt=tt, blk=blk, has_past=has_past)


def _hgrn_stream(q_ref, f_ref, i_ref, g_ref, lb_ref, og_ref, s0_ref, y_ref, st_ref, state_ref,
                 *, tt, blk, has_past):
    t_idx = pl.program_id(1)
    n_chunks = tt // blk
    n_sub = blk // HGRN_SUB

    @pl.when(t_idx == 0)
    def _():
        if has_past:
            for h in range(A_HEADS):
                state_ref[h] = s0_ref[h].T
        else:
            state_ref[...] = jnp.zeros_like(state_ref)

    row = lax.broadcasted_iota(jnp.int32, (tt, tt), 0)
    col = lax.broadcasted_iota(jnp.int32, (tt, tt), 1)
    same_chunk = _div_pow2(row, blk) == _div_pow2(col, blk)
    causal = same_chunk & (col <= row)
    tril = jnp.where(causal, 1.0, 0.0).astype(BF16)
    heads = [slice(h * LANES, (h + 1) * LANES) for h in range(A_HEADS)]

    q = q_ref[...].astype(F32)
    v_bf = i_ref[...].astype(BF16)
    lb = lb_ref[...]
    f = lb + (1.0 - lb) * jax.nn.sigmoid(f_ref[...].astype(F32))
    k = 1.0 - f
    logf = jnp.log(f)
    hi = logf.astype(BF16)
    mid = (logf - hi.astype(F32)).astype(BF16)
    b = _dot(tril, hi) + _dot(tril, mid)

    q_slab, k_slab, ends = [], [], []
    for c in range(n_chunks):
        rows = [slice(c * blk + i * HGRN_SUB, c * blk + (i + 1) * HGRN_SUB) for i in range(n_sub)]
        mids = [b[r.start + HGRN_SUB // 2 - 1:r.start + HGRN_SUB // 2, :] for r in rows]
        q_slab.append([(q[r] * jnp.exp(jnp.minimum(b[r] - mids[i], EXP_CLAMP))).astype(BF16)
                       for i, r in enumerate(rows)])
        k_slab.append([[None if j > i else
                        (k[r] * jnp.exp(jnp.minimum(mids[i] - b[r], EXP_CLAMP) if j == i
                                        else mids[i] - b[r])).astype(BF16)
                        for i in range(n_sub)] for j, r in enumerate(rows)])
        ends.append(b[(c + 1) * blk - 1:(c + 1) * blk, :])
    b_end_rows = (jnp.broadcast_to(ends[0], (blk, A_WIDTH)) if n_chunks == 1 else
                  jnp.concatenate([jnp.broadcast_to(e, (blk, A_WIDTH)) for e in ends], axis=0))
    q_e = (q * jnp.exp(b)).astype(BF16)
    k_d = (k * jnp.exp(b_end_rows - b)).astype(BF16)
    zero_slab = jnp.zeros((HGRN_SUB, LANES), BF16)
    zero_blk = jnp.zeros((blk, LANES), BF16)
    chunk_rows = [slice(c * blk, (c + 1) * blk) for c in range(n_chunks)]

    outs = []
    for h, sl in enumerate(heads):
        q_hat = jnp.concatenate(
            [jnp.concatenate([q_slab[c][j][:, sl] if i == j else zero_slab for i in range(n_sub)], axis=1)
             for c in range(n_chunks) for j in range(n_sub)], axis=0)
        k_hat = jnp.concatenate(
            [jnp.concatenate([k_slab[c][j][i][:, sl] if j <= i else zero_slab for i in range(n_sub)], axis=1)
             for c in range(n_chunks) for j in range(n_sub)], axis=0)
        att = _dot_nt(q_hat, k_hat)
        att = jnp.where(causal, att, 0.0).astype(BF16)
        spread = lambda a: jnp.concatenate(
            [jnp.concatenate([a[rs, sl] if c == d else zero_blk for d in range(n_chunks)], axis=1)
             for c, rs in enumerate(chunk_rows)], axis=0)
        kv = _dot_tn(spread(v_bf), k_d[:, sl])
        s_t = state_ref[h]
        states = []
        for c in range(n_chunks):
            states.append(s_t.astype(BF16))
            s_t = s_t * jnp.exp(ends[c][:, sl]) + kv[c * A_VAL:(c + 1) * A_VAL]
        state_ref[h] = s_t
        o = _dot(att, v_bf[:, sl]) + _dot_nt(spread(q_e), jnp.concatenate(states, axis=1))
        outs.append(_rms(o, A_VAL) * og_ref[...])
    y = jnp.concatenate(outs, axis=1) * jax.nn.sigmoid(g_ref[...].astype(F32))
    y_ref[...] = y.astype(y_ref.dtype)

    @pl.when(t_idx == pl.num_programs(1) - 1)
    def _():
        for h in range(A_HEADS):
            st_ref[h] = state_ref[h].T


def hgrn2(proj, lb, out_g, s0, tt_cap=256):
    bsz, t, _ = proj.shape
    blk = min(HGRN_BLOCK, t)
    tt = _row_tile(t, tt_cap)
    nb = _row_tile(bsz, max(1, HGRN_STEP_ROWS // tt))
    has_past = s0 is not None
    if s0 is None:
        s0 = jnp.zeros((bsz, A_HEADS, A_KEY, A_VAL), F32)
    seg = lambda s: pl.BlockSpec((nb, tt, A_WIDTH), lambda b, i, s=s: (b, i, s))
    kern = functools.partial(_hgrn_kernel, tt=tt, blk=blk, has_past=has_past)
    return pl.pallas_call(
        kern,
        grid=(bsz // nb, t // tt),
        in_specs=[
            seg(0), seg(1), seg(2), seg(3),
            pl.BlockSpec((1, A_QK), lambda b, i: (0, 0)),
            pl.BlockSpec((1, A_VAL), lambda b, i: (0, 0)),
            pl.BlockSpec((nb, A_HEADS, A_KEY, A_VAL), lambda b, i: (b, 0, 0, 0)),
        ],
        out_specs=[
            pl.BlockSpec((nb, tt, A_WIDTH), lambda b, i: (b, i, 0)),
            pl.BlockSpec((nb, A_HEADS, A_KEY, A_VAL), lambda b, i: (b, 0, 0, 0)),
        ],
        out_shape=[
            jax.ShapeDtypeStruct((bsz, t, A_WIDTH), BF16),
            jax.ShapeDtypeStruct((bsz, A_HEADS, A_KEY, A_VAL), F32),
        ],
        scratch_shapes=[pltpu.VMEM((nb, A_HEADS, A_VAL, A_KEY), F32)],
        compiler_params=_cparams(("parallel", "arbitrary")),
        name="hgrn2",
    )(proj, proj, proj, proj, lb.reshape(1, A_QK), out_g.reshape(1, A_VAL), s0)


S5_SUPER = 4
S5_SUPER_U = B_WIDTH // S5_SUPER
S5_SUPER_X = 2 * (B_GROUPS // S5_SUPER) * B_STATE
S5_HALF = S5_SUPER_X // 2
S5_STATE_W = S5_SUPER * S5_SUPER_X
S5_SCAN_GROUP = 2


def _s5_kernel(u_ref, wb_ref, wc_ref, lam_ref, d_ref, wglu_ref, bglu_ref, x0_ref,
               o_ref, xt_ref, xbuf_ref, state_ref, *, tt):
    t_idx = pl.program_id(1)
    rows = tt * SUBLANES

    @pl.when(t_idx == 0)
    def _():
        state_ref[...] = x0_ref[...]

    u_bf = u_ref[...].reshape(rows, B_WIDTH)
    u = u_bf.astype(F32)
    for j in range(S5_SUPER):
        xbuf_ref[:, j * S5_SUPER_X:(j + 1) * S5_SUPER_X] = _dot(
            u_bf[:, j * S5_SUPER_U:(j + 1) * S5_SUPER_U], wb_ref[j])

    for j0 in range(0, S5_SUPER, S5_SCAN_GROUP):
        cols, lams = [], []
        for j in range(j0, j0 + S5_SCAN_GROUP):
            re_cols = slice(j * S5_SUPER_X, j * S5_SUPER_X + S5_HALF)
            im_cols = slice(j * S5_SUPER_X + S5_HALF, (j + 1) * S5_SUPER_X)
            cols.append((re_cols, im_cols))
            lams.append((jnp.broadcast_to(lam_ref[0:1, re_cols], (SUBLANES, S5_HALF)),
                         jnp.broadcast_to(lam_ref[1:2, re_cols], (SUBLANES, S5_HALF))))

        def step(t, carry, cols=cols, lams=lams):
            rows_t = pl.ds(pl.multiple_of(t * SUBLANES, SUBLANES), SUBLANES)
            out = []
            for (re_cols, im_cols), (lam_re, lam_im), (x_re, x_im) in zip(cols, lams, carry):
                n_re = lam_re * x_re - lam_im * x_im + xbuf_ref[rows_t, re_cols]
                n_im = lam_re * x_im + lam_im * x_re + xbuf_ref[rows_t, im_cols]
                xbuf_ref[rows_t, re_cols] = n_re
                xbuf_ref[rows_t, im_cols] = n_im
                out.append((n_re, n_im))
            return tuple(out)

        init = tuple((state_ref[:, rc], state_ref[:, ic]) for rc, ic in cols)
        last = lax.fori_loop(0, tt, step, init, unroll=True)
        for (rc, ic), (x_re, x_im) in zip(cols, last):
            state_ref[:, rc] = x_re
            state_ref[:, ic] = x_im

    ys = []
    for j in range(S5_SUPER):
        xs = xbuf_ref[:, j * S5_SUPER_X:(j + 1) * S5_SUPER_X].astype(BF16)
        ys.append(_dot(xs, wc_ref[j]))
    y = jnp.concatenate(ys, axis=1) + d_ref[...] * u
    z = jax.nn.gelu(y)
    gate = jax.nn.sigmoid(_dot(z.astype(BF16), wglu_ref[...]) + bglu_ref[...])
    o_ref[...] = (z * gate).astype(o_ref.dtype).reshape(tt, SUBLANES, B_WIDTH)

    @pl.when(t_idx == pl.num_programs(1) - 1)
    def _():
        xt_ref[...] = state_ref[...]


def s5_mix(u_tb, wb, wc, lam, d, w_glu, b_glu, x0, tt_cap=64):
    t, bsz, _ = u_tb.shape
    tt = _row_tile(t, tt_cap)
    kern = functools.partial(_s5_kernel, tt=tt)
    return pl.pallas_call(
        kern,
        grid=(bsz // SUBLANES, t // tt),
        in_specs=[
            pl.BlockSpec((tt, SUBLANES, B_WIDTH), lambda b, i: (i, b, 0)),
            pl.BlockSpec((S5_SUPER, S5_SUPER_U, S5_SUPER_X), lambda b, i: (0, 0, 0)),
            pl.BlockSpec((S5_SUPER, S5_SUPER_X, S5_SUPER_U), lambda b, i: (0, 0, 0)),
            pl.BlockSpec((2, S5_STATE_W), lambda b, i: (0, 0)),
            pl.BlockSpec((1, B_WIDTH), lambda b, i: (0, 0)),
            pl.BlockSpec((B_WIDTH, B_WIDTH), lambda b, i: (0, 0)),
            pl.BlockSpec((1, B_WIDTH), lambda b, i: (0, 0)),
            pl.BlockSpec((SUBLANES, S5_STATE_W), lambda b, i: (b, 0)),
        ],
        out_specs=[
            pl.BlockSpec((tt, SUBLANES, B_WIDTH), lambda b, i: (i, b, 0)),
            pl.BlockSpec((SUBLANES, S5_STATE_W), lambda b, i: (b, 0)),
        ],
        out_shape=[
            jax.ShapeDtypeStruct((t, bsz, B_WIDTH), BF16),
            jax.ShapeDtypeStruct((bsz, S5_STATE_W), F32),
        ],
        scratch_shapes=[pltpu.VMEM((tt * SUBLANES, S5_STATE_W), F32),
                        pltpu.VMEM((SUBLANES, S5_STATE_W), F32)],
        compiler_params=_cparams(("parallel", "arbitrary")),
        name="s5_mix",
    )(u_tb, wb, wc, lam, d.reshape(1, B_WIDTH), w_glu, b_glu.reshape(1, B_WIDTH), x0)


def _s5_pack_cols(a):
    return a.reshape(a.shape[:-2] + (S5_SUPER, S5_HALF))


def s5_prepare(lam_re, lam_im, log_dt, b_re, b_im, c_re, c_im):
    lr = jnp.minimum(lam_re, LAMBDA_RE_CEIL)
    li = lam_im
    dt = jnp.exp(log_dt)[:, None]
    mag = jnp.exp(lr * dt)
    bar_re = mag * jnp.cos(li * dt)
    bar_im = mag * jnp.sin(li * dt)
    den = lr * lr + li * li
    coef_re = (((bar_re - 1.0) * lr + bar_im * li) / den)[:, :, None]
    coef_im = ((bar_im * lr - (bar_re - 1.0) * li) / den)[:, :, None]
    bb_re = coef_re * b_re - coef_im * b_im
    bb_im = coef_re * b_im + coef_im * b_re
    gps = B_GROUPS // S5_SUPER
    eye = jnp.eye(gps, dtype=F32)

    def in_block(bm):
        bm = bm.reshape(S5_SUPER, gps, B_STATE, B_GROUP_DIM)
        blk = jnp.einsum('jgph,gk->jghkp', bm, eye)
        return blk.reshape(S5_SUPER, gps * B_GROUP_DIM, gps * B_STATE)

    def out_block(cm):
        cm = cm.reshape(S5_SUPER, gps, B_GROUP_DIM, B_STATE)
        blk = jnp.einsum('jghp,gk->jkpgh', cm, eye)
        return blk.reshape(S5_SUPER, gps * B_STATE, gps * B_GROUP_DIM)

    wb = jnp.concatenate([in_block(bb_re), in_block(bb_im)], axis=2).astype(BF16)
    wc = jnp.concatenate([out_block(c_re), out_block(-c_im)], axis=1).astype(BF16)
    lam_rows = jnp.stack([
        jnp.concatenate([_s5_pack_cols(bar_re)] * 2, axis=-1).reshape(S5_STATE_W),
        jnp.concatenate([_s5_pack_cols(bar_im)] * 2, axis=-1).reshape(S5_STATE_W)])
    return wb, wc, lam_rows


def s5_pack_state(x_re, x_im):
    bsz = x_re.shape[0]
    return jnp.concatenate([_s5_pack_cols(x_re), _s5_pack_cols(x_im)], axis=-1).reshape(bsz, S5_STATE_W)


def s5_unpack_state(x):
    bsz = x.shape[0]
    x = x.reshape(bsz, S5_SUPER, 2, S5_HALF)
    return (x[:, :, 0].reshape(bsz, B_GROUPS, B_STATE), x[:, :, 1].reshape(bsz, B_GROUPS, B_STATE))


def _gmlp_kernel(u_ref, v_ref, gn_ref, ws_ref, bs_ref, o_ref, vr_ref, *, tt, chunk, all_rows):
    n_chunks = tt // chunk
    u = jax.nn.gelu(u_ref[...].astype(F32))
    v = jax.nn.gelu(v_ref[...].astype(F32))
    vn = _rms(v, C_WIDTH) * gn_ref[...]
    vn_bf = vn.astype(BF16)
    row = lax.broadcasted_iota(jnp.int32, (chunk, chunk), 0)
    col = lax.broadcasted_iota(jnp.int32, (chunk, chunk), 1)
    keep = col <= row
    for g in range(C_GROUPS):
        w = jnp.where(keep, ws_ref[g], 0.0).astype(BF16)
        bias = bs_ref[:, g:g + 1]
        cs = slice(g * C_GROUP_DIM, (g + 1) * C_GROUP_DIM)
        for c in range(n_chunks):
            rs = slice(c * chunk, (c + 1) * chunk)
            s = _dot(w, vn_bf[rs, cs]) + bias
            o_ref[rs, cs] = (u[rs, cs] * s).astype(o_ref.dtype)

    if all_rows:
        vr_ref[...] = vn
    else:
        @pl.when(pl.program_id(1) == pl.num_programs(1) - 1)
        def _():
            vr_ref[...] = vn[tt - chunk:, :]


def gmlp(proj, v_norm, w_s, b_s, tt_cap=1024):
    bsz, t, width = proj.shape
    chunk = min(t, C_CHUNK)
    all_rows = t == chunk
    if all_rows:
        c_out, v_rows = gmlp_call(proj.reshape(1, bsz * t, width), v_norm, w_s, b_s, chunk, True, tt_cap)
        return c_out.reshape(bsz, t, C_WIDTH), v_rows.reshape(bsz, t, C_WIDTH)
    return gmlp_call(proj, v_norm, w_s, b_s, chunk, False, tt_cap)


def gmlp_call(proj, v_norm, w_s, b_s, chunk, all_rows, tt_cap):
    bsz, t, _ = proj.shape
    tt = _row_tile(t, tt_cap)
    vr_rows = tt if all_rows else chunk
    ws = w_s[:, :chunk, :chunk]
    bs_t = b_s[:, :chunk].T
    kern = functools.partial(_gmlp_kernel, tt=tt, chunk=chunk, all_rows=all_rows)
    return pl.pallas_call(
        kern,
        grid=(bsz, t // tt),
        in_specs=[
            pl.BlockSpec((None, tt, C_WIDTH), lambda b, i: (b, i, 0)),
            pl.BlockSpec((None, tt, C_WIDTH), lambda b, i: (b, i, 1)),
            pl.BlockSpec((1, C_WIDTH), lambda b, i: (0, 0)),
            pl.BlockSpec((C_GROUPS, chunk, chunk), lambda b, i: (0, 0, 0)),
            pl.BlockSpec((chunk, C_GROUPS), lambda b, i: (0, 0)),
        ],
        out_specs=[
            pl.BlockSpec((None, tt, C_WIDTH), lambda b, i: (b, i, 0)),
            pl.BlockSpec((None, vr_rows, C_WIDTH), (lambda b, i: (b, i, 0)) if all_rows else (lambda b, i: (b, 0, 0))),
        ],
        out_shape=[
            jax.ShapeDtypeStruct((bsz, t, C_WIDTH), BF16),
            jax.ShapeDtypeStruct((bsz, t if all_rows else chunk, C_WIDTH), F32),
        ],
        compiler_params=_cparams(("parallel", "arbitrary")),
        name="gmlp",
    )(proj, proj, v_norm.reshape(1, C_WIDTH), ws, bs_t)


def _rope_lanes(x, cos_t, sin_lo, sin_hi):
    half = ROPE_DIM // 2
    return (x * cos_t + pltpu.roll(x, LANES - half, axis=1) * sin_lo
            + pltpu.roll(x, half, axis=1) * sin_hi)


def _mla_q_kernel(cq_ref, ckv_ref, kpe_ref, qn_ref, kvn_ref, wuq_ref, qg_ref,
                  qc_ref, qs_ref, kc_ref, ks1_ref, ks2_ref,
                  q_ref, ckv_o_ref, kpe_o_ref, ckpe_ref):
    cqn = (_rms(cq_ref[...].astype(F32), Q_LORA) * qn_ref[...]).astype(BF16)
    width = D_HEADS * HEAD_PAD
    q_all = _dot(cqn, wuq_ref[...])
    qc, qs = qc_ref[...], qs_ref[...]
    scale = QK_DIM ** -0.5 * math.log2(math.e)
    for h in range(D_HEADS):
        sl = slice(h * HEAD_PAD, (h + 1) * HEAD_PAD)
        rot = slice(width + h * HEAD_PAD, width + (h + 1) * HEAD_PAD)
        qh = q_all[:, sl] * qc + q_all[:, rot] * qs
        qh = _rms(qh, QK_DIM) * (qg_ref[...] * scale)
        q_ref[:, sl] = qh.astype(q_ref.dtype)
    ckv = _rms(ckv_ref[...].astype(F32), KV_LORA) * kvn_ref[...]
    kpe = _rope_lanes(kpe_ref[...].astype(F32), kc_ref[...], ks1_ref[...], ks2_ref[...])
    ckv_o_ref[...] = ckv
    kpe_o_ref[...] = kpe[:, :ROPE_DIM]
    ckpe_ref[:, :KV_LORA] = ckv.astype(BF16)
    ckpe_ref[:, KV_LORA:] = kpe.astype(BF16)


MLA_Q_ROWS = 1024


def mla_q(proj, tables, q_norm, kv_norm, wuq_p, q_gain_p, tm_cap=MLA_Q_ROWS):
    bsz, t, _ = proj.shape
    tm = _row_tile(t, tm_cap)
    tab = pl.BlockSpec((tm, LANES), lambda b, i: (i, 0))
    row = lambda w: pl.BlockSpec((1, w), lambda b, i: (0, 0))
    return pl.pallas_call(
        _mla_q_kernel,
        grid=(bsz, t // tm),
        in_specs=[
            pl.BlockSpec((None, tm, Q_LORA), lambda b, i: (b, i, 2 * C_WIDTH // Q_LORA)),
            pl.BlockSpec((None, tm, KV_LORA), lambda b, i: (b, i, (2 * C_WIDTH + Q_LORA) // KV_LORA)),
            pl.BlockSpec((None, tm, LANES), lambda b, i: (b, i, (2 * C_WIDTH + Q_LORA + KV_LORA) // LANES)),
            row(Q_LORA), row(KV_LORA),
            pl.BlockSpec((Q_LORA, 2 * D_HEADS * HEAD_PAD), lambda b, i: (0, 0)),
            row(HEAD_PAD),
            tab, tab, tab, tab, tab,
        ],
        out_specs=[
            pl.BlockSpec((None, tm, D_HEADS * HEAD_PAD), lambda b, i: (b, i, 0)),
            pl.BlockSpec((None, tm, KV_LORA), lambda b, i: (b, i, 0)),
            pl.BlockSpec((None, tm, ROPE_DIM), lambda b, i: (b, i, 0)),
            pl.BlockSpec((None, tm, KV_LORA + LANES), lambda b, i: (b, i, 0)),
        ],
        out_shape=[
            jax.ShapeDtypeStruct((bsz, t, D_HEADS * HEAD_PAD), BF16),
            jax.ShapeDtypeStruct((bsz, t, KV_LORA), F32),
            jax.ShapeDtypeStruct((bsz, t, ROPE_DIM), F32),
            jax.ShapeDtypeStruct((bsz, t, KV_LORA + LANES), BF16),
        ],
        compiler_params=_cparams(("parallel", "parallel")),
        name="mla_q",
    )(proj, proj, proj, q_norm.reshape(1, Q_LORA), kv_norm.reshape(1, KV_LORA), wuq_p,
      q_gain_p, *tables)


def _mla_kv_kernel(ckpe_ref, wk_ref, wvt_ref, kg_ref, k_ref, vt_ref):
    ckpe = ckpe_ref[...]
    k_all = _dot(ckpe, wk_ref[...])
    for h in range(D_HEADS):
        sl = slice(h * HEAD_PAD, (h + 1) * HEAD_PAD)
        k_ref[:, sl] = (_rms(k_all[:, sl], QK_DIM) * kg_ref[...]).astype(k_ref.dtype)
    row = lax.broadcasted_iota(jnp.int32, (D_HEADS * HEAD_PAD, 1), 0)
    ones_row = jnp.where(_mod_pow2(row, HEAD_PAD) == V_ONE_ROW, 1.0, 0.0)
    vt_ref[...] = (_dot_nt(wvt_ref[...], ckpe[:, :KV_LORA]) + ones_row).astype(vt_ref.dtype)


def mla_kv(ckpe, wk_full, wv_p, k_gain_p):
    bsz, s, kw = ckpe.shape
    tm = _row_tile(s, ATTN_TK)
    width = D_HEADS * HEAD_PAD
    full = lambda r, c: pl.BlockSpec((r, c), lambda b, i: (0, 0))
    return pl.pallas_call(
        _mla_kv_kernel,
        grid=(bsz, s // tm),
        in_specs=[
            pl.BlockSpec((None, tm, kw), lambda b, i: (b, i, 0)),
            full(kw, width), full(width, KV_LORA), full(1, HEAD_PAD),
        ],
        out_specs=[
            pl.BlockSpec((None, tm, width), lambda b, i: (b, i, 0)),
            pl.BlockSpec((None, None, width, tm), lambda b, i: (b, i, 0, 0)),
        ],
        out_shape=[
            jax.ShapeDtypeStruct((bsz, s, width), BF16),
            jax.ShapeDtypeStruct((bsz, s // tm, width, tm), BF16),
        ],
        compiler_params=_cparams(("parallel", "parallel")),
        name="mla_kv",
    )(ckpe, wk_full, wv_p.T, k_gain_p)


def _for_blocks(lo, hi, block_fn):
    n_pairs = (hi - lo) // 2

    def pair(i, carry):
        block_fn(lo + 2 * i)
        block_fn(lo + 2 * i + 1)
        return carry

    lax.fori_loop(0, n_pairs, pair, 0)

    @pl.when(lo + 2 * n_pairs < hi)
    def _():
        block_fn(hi - 1)


def _attn_kernel(q_ref, k_ref, vt_ref, o_ref, sc_ref, mx_ref, acc_ref, *, tq, tk, s_len, q_pos0):
    qi = pl.program_id(1)
    q_first = q_pos0 + qi * tq
    q_last = q_first + tq - 1
    vis_first = jnp.minimum(s_len, (q_first // CHUNK + 1) * CHUNK)
    vis_last = jnp.minimum(s_len, (q_last // CHUNK + 1) * CHUNK)
    n_full = vis_first // tk
    n_kv = (vis_last + tk - 1) // tk

    q_chunk = _div_pow2(q_first + lax.broadcasted_iota(jnp.int32, (tk, tq), 1), CHUNK)
    k_iota = lax.broadcasted_iota(jnp.int32, (tk, tq), 0)

    for g0 in range(0, D_HEADS, ATTN_HEAD_GROUP):
        heads = [slice(h * HEAD_PAD, (h + 1) * HEAD_PAD) for h in range(g0, g0 + ATTN_HEAD_GROUP)]

        def score_block(j, masked, heads=heads):
            k0 = pl.multiple_of(j * tk, tk)
            visible = (_div_pow2(k0 + k_iota, CHUNK) <= q_chunk) if masked else None
            for h, sl in enumerate(heads):
                s = _dot_nt(k_ref[pl.ds(k0, tk), sl], q_ref[:, sl])
                if masked:
                    s = jnp.where(visible, s, NEG_BIG)
                sc_ref[h, j] = s
                blk_max = jnp.max(s.reshape(tk // SUBLANES, SUBLANES, tq), axis=0)
                mx_ref[h] = jnp.maximum(mx_ref[h], blk_max)

        def sum_block(j, heads=heads):
            for h, sl in enumerate(heads):
                p = jnp.exp2(sc_ref[h, j] - mx_ref[h][0:1, :])
                acc_ref[h] += _dot(vt_ref[j, sl, :], p.astype(BF16))

        mx_ref[...] = jnp.full_like(mx_ref, NEG_BIG)
        _for_blocks(0, n_full, functools.partial(score_block, masked=False))
        _for_blocks(n_full, n_kv, functools.partial(score_block, masked=True))
        for h in range(ATTN_HEAD_GROUP):
            m = jnp.max(mx_ref[h], axis=0, keepdims=True)
            mx_ref[h] = jnp.broadcast_to(m, (SUBLANES, tq))
        acc_ref[...] = jnp.zeros_like(acc_ref)
        _for_blocks(0, n_kv, sum_block)
        for hp in range(0, ATTN_HEAD_GROUP, 2):
            pair = [acc_ref[h][:V_DIM, :] / acc_ref[h][V_ONE_ROW:V_ONE_ROW + 1, :] for h in (hp, hp + 1)]
            lanes = slice((g0 + hp) * V_DIM, (g0 + hp + 2) * V_DIM)
            o_ref[:, lanes] = jnp.concatenate(pair, axis=0).T.astype(o_ref.dtype)


def attention(q, k, vt, q_pos0, tq_cap=512):
    bsz, tq_len, width = q.shape
    s_len = k.shape[1]
    n_blocks, _, tk = vt.shape[1:]
    assert n_blocks * tk == s_len and tk % LANES == 0
    tq = _row_tile(tq_len, tq_cap)
    kern = functools.partial(_attn_kernel, tq=tq, tk=tk, s_len=s_len, q_pos0=q_pos0)
    return pl.pallas_call(
        kern,
        grid=(bsz, tq_len // tq),
        in_specs=[
            pl.BlockSpec((None, tq, width), lambda b, i: (b, i, 0)),
            pl.BlockSpec((None, s_len, width), lambda b, i: (b, 0, 0)),
            pl.BlockSpec((None, n_blocks, width, tk), lambda b, i: (b, 0, 0, 0)),
        ],
        out_specs=pl.BlockSpec((None, tq, D_HEADS * V_DIM), lambda b, i: (b, i, 0)),
        out_shape=jax.ShapeDtypeStruct((bsz, tq_len, D_HEADS * V_DIM), BF16),
        scratch_shapes=[pltpu.VMEM((ATTN_HEAD_GROUP, n_blocks, tk, tq), F32),
                        pltpu.VMEM((ATTN_HEAD_GROUP, SUBLANES, tq), F32),
                        pltpu.VMEM((ATTN_HEAD_GROUP, HEAD_PAD, tq), F32)],
        compiler_params=_cparams(("parallel", "arbitrary")),
        name="attention",
    )(q, k, vt)


def _mla_decode_kernel(q_ref, ckv_c_ref, kpe_c_ref, ckv_n_ref, kpe_n_ref, wk_ref, wkt_ref, wv_ref, kg_ref,
                       o_ref, *, tq, s_len, q_pos0):
    s_pad = ckv_c_ref.shape[0] + ckv_n_ref.shape[0]
    heads = [slice(h * HEAD_PAD, (h + 1) * HEAD_PAD) for h in range(D_HEADS)]
    ckv = jnp.concatenate([ckv_c_ref[...], ckv_n_ref[...]], axis=0).astype(BF16)
    kpe_raw = jnp.concatenate([kpe_c_ref[...], kpe_n_ref[...]], axis=0).astype(BF16)
    lane = lax.broadcasted_iota(jnp.int32, (ROPE_DIM, LANES), 1)
    src = lax.broadcasted_iota(jnp.int32, (ROPE_DIM, LANES), 0)
    place = jnp.where(lane == src + NOPE_DIM, 1.0, 0.0).astype(BF16)
    kpe = _dot(kpe_raw, place).astype(BF16)
    eye = (lax.broadcasted_iota(jnp.int32, (LANES, LANES), 0)
           == lax.broadcasted_iota(jnp.int32, (LANES, LANES), 1))
    kpe_t = _dot_nt(jnp.where(eye, 1.0, 0.0).astype(BF16), kpe)
    pe_ss = jnp.sum(kpe_t * kpe_t, axis=0, keepdims=True)
    kn_t = _dot_nt(wkt_ref[...], ckv)
    inv_rms, q_lat, q_g = [], [], []
    for h, sl in enumerate(heads):
        blk = kn_t[h * NOPE_DIM:(h + 1) * NOPE_DIM, :]
        ss = jnp.sum(blk * blk, axis=0, keepdims=True) + pe_ss
        inv_rms.append(jnp.broadcast_to(lax.rsqrt(ss * (1.0 / QK_DIM) + EPS), (tq, s_pad)))
        qg = (q_ref[:, sl].astype(F32) * kg_ref[...]).astype(BF16)
        q_g.append(qg)
        q_lat.append(_dot_nt(qg, wk_ref[:, sl]).astype(BF16))
    s = _dot_nt(jnp.concatenate(q_lat, axis=0), ckv) + _dot_nt(jnp.concatenate(q_g, axis=0), kpe)
    s = s * jnp.concatenate(inv_rms, axis=0)
    row = lax.broadcasted_iota(jnp.int32, s.shape, 0)
    col = lax.broadcasted_iota(jnp.int32, s.shape, 1)
    q_chunk = _div_pow2(q_pos0 + _mod_pow2(row, tq), CHUNK)
    visible = (_div_pow2(col, CHUNK) <= q_chunk) & (col < s_len)
    s = jnp.where(visible, s, NEG_BIG)
    p = jnp.exp2(s - jnp.max(s, axis=-1, keepdims=True))
    lat = _dot(p.astype(BF16), ckv) / jnp.sum(p, axis=-1, keepdims=True)
    lat = lat.astype(BF16)
    for hp in range(D_HEADS // 2):
        both = jnp.concatenate([lat[2 * hp * tq:(2 * hp + 1) * tq], lat[(2 * hp + 1) * tq:(2 * hp + 2) * tq]], axis=1)
        o_ref[:, hp * LANES:(hp + 1) * LANES] = _dot(both, wv_ref[hp]).astype(o_ref.dtype)


def mla_decode(q, cache_ckv, cache_kpe, ckv_new, kpe_new, wk_p, wv_pairs, k_gain_p, q_pos0):
    bsz, tq, width = q.shape
    n_cache, n_new = cache_ckv.shape[1], ckv_new.shape[1]
    assert n_cache % LANES == 0 and tq & (tq - 1) == 0
    s_len = n_cache + n_new
    new_pad = -(-n_new // LANES) * LANES
    ckv_new = jnp.pad(ckv_new, ((0, 0), (0, new_pad - n_new), (0, 0)))
    kpe_new = jnp.pad(kpe_new, ((0, 0), (0, new_pad - n_new), (0, 0)))
    wk_rows = wk_p.T.reshape(D_HEADS, HEAD_PAD, KV_LORA)[:, :NOPE_DIM].reshape(D_HEADS * NOPE_DIM, KV_LORA)
    full = lambda r, c: pl.BlockSpec((r, c), lambda b: (0, 0))
    per_b = lambda r, c: pl.BlockSpec((None, r, c), lambda b: (b, 0, 0))
    kern = functools.partial(_mla_decode_kernel, tq=tq, s_len=s_len, q_pos0=q_pos0)
    return pl.pallas_call(
        kern,
        grid=(bsz,),
        in_specs=[
            per_b(tq, width), per_b(n_cache, KV_LORA), per_b(n_cache, ROPE_DIM),
            per_b(new_pad, KV_LORA), per_b(new_pad, ROPE_DIM),
            full(KV_LORA, width), full(D_HEADS * NOPE_DIM, KV_LORA),
            pl.BlockSpec((D_HEADS // 2, 2 * KV_LORA, LANES), lambda b: (0, 0, 0)), full(1, HEAD_PAD),
        ],
        out_specs=per_b(tq, D_HEADS * V_DIM),
        out_shape=jax.ShapeDtypeStruct((bsz, tq, D_HEADS * V_DIM), BF16),
        compiler_params=_cparams(("parallel",)),
        name="mla_decode",
    )(q, cache_ckv, cache_kpe, ckv_new, kpe_new, wk_p, wk_rows, wv_pairs, k_gain_p)


def rope_tables(pos):
    half = ROPE_DIM // 2
    inv = ROPE_BASE ** (-jnp.arange(half, dtype=F32) / half)
    ang = pos.astype(F32)[:, None] * inv[None, :]
    cos, sin = jnp.cos(ang), jnp.sin(ang)
    n = pos.shape[0]
    z = lambda w: jnp.zeros((n, w), F32)
    o = lambda w: jnp.ones((n, w), F32)
    tail = LANES - NOPE_DIM - ROPE_DIM
    q_cos = jnp.concatenate([o(NOPE_DIM), cos, cos, z(tail)], axis=1)
    q_sin = jnp.concatenate([z(NOPE_DIM), sin, sin, z(tail)], axis=1)
    k_cos = jnp.concatenate([cos, cos, z(LANES - ROPE_DIM)], axis=1)
    k_s1 = jnp.concatenate([-sin, z(LANES - half)], axis=1)
    k_s2 = jnp.concatenate([z(half), sin, z(LANES - ROPE_DIM)], axis=1)
    return q_cos, q_sin, k_cos, k_s1, k_s2


def pack_mla_weights(w_uq, w_ukv, q_gain, k_gain, cd_w_out):
    pad_q = HEAD_PAD - QK_DIM
    half = ROPE_DIM // 2
    wq = w_uq.reshape(Q_LORA, D_HEADS, QK_DIM)
    wuq_p = jnp.pad(wq, ((0, 0), (0, 0), (0, pad_q))).reshape(Q_LORA, D_HEADS * HEAD_PAD)
    wq_rot = jnp.concatenate([jnp.zeros_like(wq[:, :, :NOPE_DIM]), -wq[:, :, NOPE_DIM + half:],
                              wq[:, :, NOPE_DIM:NOPE_DIM + half], jnp.zeros_like(wq[:, :, :pad_q])], axis=2)
    wuq_p = jnp.concatenate([wuq_p, wq_rot.reshape(Q_LORA, D_HEADS * HEAD_PAD)], axis=1).astype(BF16)
    wkv = w_ukv.reshape(KV_LORA, D_HEADS, NOPE_DIM + V_DIM)
    wk_p = jnp.pad(wkv[:, :, :NOPE_DIM], ((0, 0), (0, 0), (0, HEAD_PAD - NOPE_DIM)))
    wk_p = wk_p.reshape(KV_LORA, D_HEADS * HEAD_PAD).astype(BF16)
    wv_p = jnp.pad(wkv[:, :, NOPE_DIM:], ((0, 0), (0, 0), (0, HEAD_PAD - V_DIM)))
    wv_p = wv_p.reshape(KV_LORA, D_HEADS * HEAD_PAD).astype(BF16)
    place = jnp.pad(jnp.eye(ROPE_DIM, dtype=F32), ((0, LANES - ROPE_DIM), (NOPE_DIM, HEAD_PAD - QK_DIM)))
    wk_full = jnp.concatenate([wk_p, jnp.tile(place, (1, D_HEADS)).astype(BF16)], axis=0)
    q_gain_p = jnp.pad(q_gain, (0, pad_q)).reshape(1, HEAD_PAD)
    k_gain_p = jnp.pad(k_gain, (0, pad_q)).reshape(1, HEAD_PAD)
    wv = wkv[:, :, NOPE_DIM:].reshape(KV_LORA, D_HEADS // 2, 2, V_DIM)
    zero = jnp.zeros_like(wv[:, :, 0])
    wv_pairs = jnp.concatenate([jnp.concatenate([wv[:, :, 0], zero], axis=2),
                                jnp.concatenate([zero, wv[:, :, 1]], axis=2)], axis=0)
    wv_pairs = jnp.transpose(wv_pairs, (1, 0, 2)).astype(BF16)
    w_c = cd_w_out[:C_WIDTH].astype(BF16)
    w_d = cd_w_out[C_WIDTH:].astype(BF16)
    return wuq_p, wk_p, wk_full, wv_p, wv_pairs, q_gain_p, k_gain_p, w_c, w_d


def _mix_even(proj, tail, w, lb, past, tmaj):
    bsz, t, _ = proj.shape
    n = bsz * t
    s0 = None if past is None else past[0][0]
    a_out, s_t = hgrn2(proj, lb, w['hgrn_out_norm'][0], s0)
    x0 = jnp.zeros((bsz, S5_STATE_W), F32) if past is None else s5_pack_state(past[1][0], past[2][0])
    if tmaj is not None:
        u_tb = tail.reshape(t, bsz, B_WIDTH)
    else:
        u_tb = jnp.transpose(proj[:, :, AB_IN - B_WIDTH:], (1, 0, 2))
    b_tb, x_t = s5_mix(u_tb, w['s5_wb'], w['s5_wc'], w['s5_lam'], w['s5_d'], w['s5_w_glu'], w['s5_b_glu'], x0)
    if tmaj is not None:
        b_out = b_tb.reshape(t, bsz * B_WIDTH)
    else:
        b_out = jnp.transpose(b_tb, (1, 0, 2)).reshape(n, B_WIDTH)
    s5re, s5im = s5_unpack_state(x_t)
    return a_out.reshape(n, A_WIDTH), b_out, dict(hgrn=s_t, s5re=s5re, s5im=s5im)


def _mix_odd(proj, w, pos0, past):
    bsz, t, _ = proj.shape
    n = bsz * t
    c_out, v_rows = gmlp(proj, w['gmlp_v_norm'], w['gmlp_w_s'], w['gmlp_b_s'])
    tables = rope_tables(pos0 + jnp.arange(t, dtype=jnp.int32))
    if t < MLA_Q_ROWS:
        flat = mla_q(proj.reshape(1, n, -1), [jnp.tile(tb, (bsz, 1)) for tb in tables],
                     w['mla_q_norm'], w['mla_kv_norm'], w['mla_wuq_p'], w['mla_q_gain_p'])
        q, ckv, kpe, ckpe = [a.reshape(bsz, t, -1) for a in flat]
    else:
        q, ckv, kpe, ckpe = mla_q(proj, tables, w['mla_q_norm'], w['mla_kv_norm'],
                                  w['mla_wuq_p'], w['mla_q_gain_p'])
    if past is None:
        k, vt = mla_kv(ckpe, w['mla_wk_full'], w['mla_wv_p'], w['mla_k_gain_p'])
        d_out = attention(q, k, vt, pos0)
    else:
        d_out = mla_decode(q, past[3][0], past[4][0], ckv, kpe, w['mla_wk_p'], w['mla_wv_pairs'],
                           w['mla_k_gain_p'], pos0)
    return c_out.reshape(n, C_WIDTH), d_out.reshape(n, D_HEADS * V_DIM), dict(gv=v_rows, ckv=ckv, kpe=kpe)


def _trunks(streams, w):
    lb_all = jnp.cumsum(jax.nn.softmax(w['hgrn_lb_logits'], axis=0), axis=0)
    ffn_w = lambda name, l: (w[name + '_norm'][l], w[name + '_w_gate'][l], w[name + '_w_up'][l],
                             w[name + '_w_down'][l])
    shapes = [x.shape[:2] for x, _, _ in streams]
    tmajs = [(bsz, t) if t % FFN_ROWS == 0 else None for bsz, t in shapes]
    xs = [x.reshape(-1, D_MODEL) for x, _, _ in streams]
    outs = [{} for _ in streams]
    for l in range(2):
        even = l % 2 == 0
        w_in = w['ab_w_in'] if even else w['cd_w_in_p']
        res = ffn([dict(x=x, time_major=tm if even else None) for x, tm in zip(xs, tmajs)],
                  *ffn_w('ffn1', l), post=(w['mix_norm'][l], w_in))
        groups = []
        for r, (bsz, t), tm, (_, pos0, past), o in zip(res, shapes, tmajs, streams, outs):
            proj = r[1].reshape(bsz, t, -1)
            if even:
                a1, a2, new = _mix_even(proj, r[2] if tm is not None else None, w, lb_all[l], past, tm)
            else:
                a1, a2, new = _mix_odd(proj, w, pos0, past)
            o.update(new)
            groups.append(dict(x=r[0], a1=a1, a2=a2, time_major=tm if even else None))
        pre_w = (w['ab_w_out_a'], w['ab_w_out_b']) if even else (w['cd_w_out_c'], w['cd_w_out_d'])
        xs = [r[0] for r in ffn(groups, *ffn_w('ffn2', l), pre_w=pre_w)]
    return [x.reshape(bsz, t, D_MODEL) for x, (bsz, t) in zip(xs, shapes)], outs


def kernel(x_prompt, x_sample, state_hgrn, state_s5_re, state_s5_im, cache_mla_ckv, cache_mla_kpe, ffn1_norm, ffn1_w_gate, ffn1_w_up, ffn1_w_down, mix_norm, ffn2_norm, ffn2_w_gate, ffn2_w_up, ffn2_w_down, ab_w_in, hgrn_lb_logits, hgrn_out_norm, s5_lambda_re, s5_lambda_im, s5_log_dt, s5_b_re, s5_b_im, s5_c_re, s5_c_im, s5_d, s5_w_glu, s5_b_glu, ab_w_out, cd_w_in, gmlp_v_norm, gmlp_w_s, gmlp_b_s, mla_q_norm, mla_w_uq, mla_kv_norm, mla_w_ukv, mla_q_gain, mla_k_gain, cd_w_out):
    bf = lambda a: a.astype(BF16)
    wb, wc, lam_rows = s5_prepare(s5_lambda_re[0], s5_lambda_im[0], s5_log_dt[0],
                                  s5_b_re[0], s5_b_im[0], s5_c_re[0], s5_c_im[0])
    wuq_p, wk_p, wk_full, wv_p, wv_pairs, q_gain_p, k_gain_p, w_c, w_d = pack_mla_weights(
        mla_w_uq[0], mla_w_ukv[0], mla_q_gain[0], mla_k_gain[0], cd_w_out[0])
    w = dict(
        ffn1_norm=ffn1_norm, ffn1_w_gate=bf(ffn1_w_gate), ffn1_w_up=bf(ffn1_w_up), ffn1_w_down=bf(ffn1_w_down),
        ffn2_norm=ffn2_norm, ffn2_w_gate=bf(ffn2_w_gate), ffn2_w_up=bf(ffn2_w_up), ffn2_w_down=bf(ffn2_w_down),
        mix_norm=mix_norm, ab_w_in=bf(ab_w_in[0]), hgrn_lb_logits=hgrn_lb_logits, hgrn_out_norm=hgrn_out_norm,
        s5_wb=wb, s5_wc=wc, s5_lam=lam_rows, s5_d=s5_d[0].reshape(B_WIDTH), s5_w_glu=bf(s5_w_glu[0]),
        s5_b_glu=s5_b_glu[0], ab_w_out_a=bf(ab_w_out[0, :A_WIDTH]), ab_w_out_b=bf(ab_w_out[0, A_WIDTH:]),
        cd_w_in_p=bf(jnp.pad(cd_w_in[0], ((0, 0), (0, CD_IN_PAD - CD_IN)))),
        gmlp_v_norm=gmlp_v_norm[0], gmlp_w_s=gmlp_w_s[0], gmlp_b_s=gmlp_b_s[0],
        mla_q_norm=mla_q_norm[0], mla_kv_norm=mla_kv_norm[0], mla_wuq_p=wuq_p, mla_q_gain_p=q_gain_p,
        mla_wk_p=wk_p, mla_wk_full=wk_full, mla_wv_p=wv_p, mla_wv_pairs=wv_pairs, mla_k_gain_p=k_gain_p,
        cd_w_out_c=w_c, cd_w_out_d=w_d,
    )
    past_len = cache_mla_ckv.shape[2]
    (y_p, y_s), (o_p, o_s) = _trunks(
        [(x_prompt, 0, None),
         (x_sample, past_len, (state_hgrn, state_s5_re, state_s5_im, cache_mla_ckv, cache_mla_kpe))], w)
    e = lambda a: a[None]
    return (y_p, y_s, e(o_p['hgrn']), e(o_s['hgrn']), e(o_p['s5re']), e(o_p['s5im']),
            e(o_s['s5re']), e(o_s['s5im']), e(o_p['gv']), e(o_s['gv']),
            e(o_p['ckv']), e(o_p['kpe']), e(o_s['ckv']), e(o_s['kpe']))
```

```python
import functools
import math

import jax
import jax.numpy as jnp
from jax import lax
from jax.experimental import pallas as pl
from jax.experimental.pallas import tpu as pltpu

D_MODEL = 1024
D_FF = 2816
EPS = 1e-6
A_HEADS = 4
A_KEY = 128
A_VAL = 128
A_WIDTH = A_HEADS * A_VAL
HGRN_BLOCK = 64
B_GROUP_DIM = 16
B_WIDTH = 512
B_GROUPS = 32
B_STATE = 64
LAMBDA_RE_CEIL = -1e-4
C_CHUNK = 128
C_WIDTH = 512
C_GROUPS = 4
C_GROUP_DIM = 128
D_HEADS = 8
Q_LORA = 256
KV_LORA = 128
NOPE_DIM = 64
ROPE_DIM = 32
V_DIM = 64
QK_DIM = NOPE_DIM + ROPE_DIM
ROPE_BASE = 10000.0
CHUNK = 64
A_QK = A_HEADS * A_KEY
AB_IN = 2 * A_QK + 2 * A_WIDTH + B_WIDTH
CD_IN = 2 * C_WIDTH + Q_LORA + KV_LORA + ROPE_DIM
CD_IN_PAD = 1536

LANES = 128
SUBLANES = 8
VMEM_LIMIT = 56 * 1024 * 1024
HGRN_SUB = 16
HGRN_STEP_ROWS = 128
HEAD_PAD = LANES
NEG_BIG = -1e30
EXP_CLAMP = 80.0
V_ONE_ROW = V_DIM
ATTN_TK = 512
ATTN_HEAD_GROUP = 4

BF16 = jnp.bfloat16
F32 = jnp.float32


def _cparams(sem):
    return pltpu.CompilerParams(dimension_semantics=sem, vmem_limit_bytes=VMEM_LIMIT)


def _rms(xf, width):
    return xf * lax.rsqrt(jnp.sum(xf * xf, axis=-1, keepdims=True) * (1.0 / width) + EPS)


def _dot(a, b):
    return jnp.dot(a, b, preferred_element_type=F32)


def _dot_nt(a, b):
    return lax.dot_general(a, b, (((1,), (1,)), ((), ())), preferred_element_type=F32)


def _dot_tn(a, b):
    return lax.dot_general(a, b, (((0,), (0,)), ((), ())), preferred_element_type=F32)


def _div_pow2(x, d):
    assert d & (d - 1) == 0
    return lax.shift_right_logical(x, jnp.int32(d.bit_length() - 1))


def _mod_pow2(x, d):
    assert d & (d - 1) == 0
    return x & jnp.int32(d - 1)


def _row_tile(n, cap):
    t = min(n, cap)
    assert n % t == 0, (n, t)
    return t


FFN_CHUNK = 256
POST_CHUNK = 512
FFN_ROWS = 512


def _ffn_rows(x_ref, pre_refs, w_refs, post_refs, o_ref, p_ref, tail_ref, acc_ref):
    g_ref, wg_ref, wu_ref, wd_ref = w_refs
    rows = x_ref.shape[0]
    x = x_ref[...]
    if pre_refs is not None:
        a1_ref, a2_ref, w1_ref, w2_ref = pre_refs
        x = x + _dot(a1_ref[...], w1_ref[...]) + _dot(a2_ref[...], w2_ref[...])
    h = (_rms(x, D_MODEL) * g_ref[...]).astype(BF16)
    for j in range(D_FF // FFN_CHUNK):
        cs = slice(j * FFN_CHUNK, (j + 1) * FFN_CHUNK)
        gate = _dot(h, wg_ref[:, cs])
        up = _dot(h, wu_ref[:, cs])
        act = (gate * jax.nn.sigmoid(gate) * up).astype(BF16)
        part = _dot(act, wd_ref[cs, :])
        if j == 0:
            acc_ref[:rows] = part
        else:
            acc_ref[:rows] += part
    out = x + 0.5 * acc_ref[:rows]
    o_ref[...] = out
    if post_refs is not None:
        g2_ref, wp_ref = post_refs
        h2 = (_rms(out, D_MODEL) * g2_ref[...]).astype(BF16)
        n_post = wp_ref.shape[1] // POST_CHUNK
        for c in range(n_post):
            cs = slice(c * POST_CHUNK, (c + 1) * POST_CHUNK)
            part = _dot(h2, wp_ref[:, cs]).astype(p_ref.dtype)
            if tail_ref is not None and c == n_post - 1:
                tail_ref[...] = part
            else:
                p_ref[:, cs] = part


def _ffn_kernel(*refs, tiles, has_pre, has_post, split_tails):
    refs = list(refs)
    take = lambda k: [refs.pop(0) for _ in range(k)]
    group_in = [take(3 if has_pre else 1) for _ in tiles]
    pre_w = take(2) if has_pre else None
    w_refs = take(4)
    post_refs = take(2) if has_post else None
    group_out = [take(1 + has_post + split) for split in split_tails]
    acc_ref, = refs
    step = pl.program_id(0)
    first = 0
    for n_tiles, ins, outs, split in zip(tiles, group_in, group_out, split_tails):
        pre_refs = (ins[1], ins[2], *pre_w) if has_pre else None
        p_ref = outs[1] if has_post else None
        tail_ref = outs[2] if split else None

        @pl.when((step >= first) & (step < first + n_tiles))
        def _(ins=ins, outs=outs, pre_refs=pre_refs, p_ref=p_ref, tail_ref=tail_ref):
            _ffn_rows(ins[0], pre_refs, w_refs, post_refs, outs[0], p_ref, tail_ref, acc_ref)

        first += n_tiles


def ffn(groups, g, wg, wu, wd, pre_w=None, post=None, tm_cap=FFN_ROWS):
    fixed = lambda r, c: pl.BlockSpec((r, c), lambda i: (0, 0), pipeline_mode=pl.Buffered(1))
    args, specs, out_specs, out_shape, tiles, split_tails = [], [], [], [], [], []
    first = 0
    for gi, grp in enumerate(groups):
        x = grp['x']
        n = x.shape[0]
        tm = _row_tile(n, tm_cap)
        n_tiles = n // tm
        local = lambda i, first=first, n_tiles=n_tiles: jnp.clip(i - first, 0, n_tiles - 1)
        mode = {} if gi == 0 else dict(pipeline_mode=pl.Buffered(1))
        rows = lambda w, local=local, tm=tm, mode=mode: pl.BlockSpec((tm, w), lambda i: (local(i), 0), **mode)
        tmaj = grp.get('time_major')
        if tmaj is not None:
            streams, steps = tmaj
            assert steps % tm == 0 and streams * steps == n
            per_stream = steps // tm
            by_time = lambda w, local=local, tm=tm, per_stream=per_stream: pl.BlockSpec(
                (tm, w), lambda i: (local(i) % per_stream, local(i) // per_stream))
        args.append(x)
        specs.append(rows(D_MODEL))
        if pre_w is not None:
            args += [grp['a1'], grp['a2']]
            specs += [rows(grp['a1'].shape[1]),
                      by_time(pre_w[1].shape[0]) if tmaj is not None else rows(grp['a2'].shape[1])]
        out_specs.append(rows(D_MODEL))
        out_shape.append(jax.ShapeDtypeStruct((n, D_MODEL), F32))
        split = post is not None and tmaj is not None
        if post is not None:
            head_w = post[1].shape[1] - (POST_CHUNK if split else 0)
            out_specs.append(rows(head_w))
            out_shape.append(jax.ShapeDtypeStruct((n, head_w), BF16))
            if split:
                out_specs.append(by_time(POST_CHUNK))
                out_shape.append(jax.ShapeDtypeStruct((steps, streams * POST_CHUNK), BF16))
        tiles.append(n_tiles)
        split_tails.append(split)
        first += n_tiles
    if pre_w is not None:
        args += list(pre_w)
        specs += [fixed(*pre_w[0].shape), fixed(*pre_w[1].shape)]
    args += [g.reshape(1, D_MODEL), wg, wu, wd]
    specs += [fixed(1, D_MODEL), fixed(D_MODEL, D_FF), fixed(D_MODEL, D_FF), fixed(D_FF, D_MODEL)]
    if post is not None:
        assert post[1].shape[1] % POST_CHUNK == 0
        args += [post[0].reshape(1, D_MODEL), post[1]]
        specs += [fixed(1, D_MODEL), fixed(*post[1].shape)]
    kern = functools.partial(_ffn_kernel, tiles=tuple(tiles), has_pre=pre_w is not None,
                             has_post=post is not None, split_tails=tuple(split_tails))
    res = pl.pallas_call(
        kern,
        grid=(first,),
        in_specs=specs,
        out_specs=out_specs,
        out_shape=out_shape,
        scratch_shapes=[pltpu.VMEM((max(min(tm_cap, grp['x'].shape[0]) for grp in groups), D_MODEL), F32)],
        compiler_params=_cparams(("arbitrary",)),
        name="ffn",
    )(*args)
    out, k = [], 0
    for split in split_tails:
        width = 1 + (post is not None) + split
        out.append(tuple(res[k:k + width]))
        k += width
    return out


def _hgrn_kernel(q_ref, f_ref, i_ref, g_ref, lb_ref, og_ref, s0_ref, y_ref, st_ref, state_ref,
                 *, tt, blk, has_past):
    for si in range(q_ref.shape[0]):
        _hgrn_stream(q_ref.at[si], f_ref.at[si], i_ref.at[si], g_ref.at[si], lb_ref, og_ref, s0_ref.at[si],
                     y_ref.at[si], st_ref.at[si], state_ref.at[si], tt=tt, blk=blk, has_past=has_past)


def _hgrn_stream(q_ref, f_ref, i_ref, g_ref, lb_ref, og_ref, s0_ref, y_ref, st_ref, state_ref,
                 *, tt, blk, has_past):
    t_idx = pl.program_id(1)
    n_chunks = tt // blk
    n_sub = blk // HGRN_SUB

    @pl.when(t_idx == 0)
    def _():
        if has_past:
            for h in range(A_HEADS):
                state_ref[h] = s0_ref[h].T
        else:
            state_ref[...] = jnp.zeros_like(state_ref)

    row = lax.broadcasted_iota(jnp.int32, (tt, tt), 0)
    col = lax.broadcasted_iota(jnp.int32, (tt, tt), 1)
    same_chunk = _div_pow2(row, blk) == _div_pow2(col, blk)
    causal = same_chunk & (col <= row)
    tril = jnp.where(causal, 1.0, 0.0).astype(BF16)
    heads = [slice(h * LANES, (h + 1) * LANES) for h in range(A_HEADS)]

    q = q_ref[...].astype(F32)
    v_bf = i_ref[...].astype(BF16)
    lb = lb_ref[...]
    f = lb + (1.0 - lb) * jax.nn.sigmoid(f_ref[...].astype(F32))
    k = 1.0 - f
    logf = jnp.log(f)
    hi = logf.astype(BF16)
    mid = (logf - hi.astype(F32)).astype(BF16)
    b = _dot(tril, hi) + _dot(tril, mid)

    q_slab, k_slab, ends = [], [], []
    for c in range(n_chunks):
        rows = [slice(c * blk + i * HGRN_SUB, c * blk + (i + 1) * HGRN_SUB) for i in range(n_sub)]
        mids = [b[r.start + HGRN_SUB // 2 - 1:r.start + HGRN_SUB // 2, :] for r in rows]
        q_slab.append([(q[r] * jnp.exp(jnp.minimum(b[r] - mids[i], EXP_CLAMP))).astype(BF16)
                       for i, r in enumerate(rows)])
        k_slab.append([[None if j > i else
                        (k[r] * jnp.exp(jnp.minimum(mids[i] - b[r], EXP_CLAMP) if j == i
                                        else mids[i] - b[r])).astype(BF16)
                        for i in range(n_sub)] for j, r in enumerate(rows)])
        ends.append(b[(c + 1) * blk - 1:(c + 1) * blk, :])
    b_end_rows = (jnp.broadcast_to(ends[0], (blk, A_WIDTH)) if n_chunks == 1 else
                  jnp.concatenate([jnp.broadcast_to(e, (blk, A_WIDTH)) for e in ends], axis=0))
    q_e = (q * jnp.exp(b)).astype(BF16)
    k_d = (k * jnp.exp(b_end_rows - b)).astype(BF16)
    zero_slab = jnp.zeros((HGRN_SUB, LANES), BF16)
    zero_blk = jnp.zeros((blk, LANES), BF16)
    chunk_rows = [slice(c * blk, (c + 1) * blk) for c in range(n_chunks)]

    outs = []
    for h, sl in enumerate(heads):
        q_hat = jnp.concatenate(
            [jnp.concatenate([q_slab[c][j][:, sl] if i == j else zero_slab for i in range(n_sub)], axis=1)
             for c in range(n_chunks) for j in range(n_sub)], axis=0)
        k_hat = jnp.concatenate(
            [jnp.concatenate([k_slab[c][j][i][:, sl] if j <= i else zero_slab for i in range(n_sub)], axis=1)
             for c in range(n_chunks) for j in range(n_sub)], axis=0)
        att = _dot_nt(q_hat, k_hat)
        att = jnp.where(causal, att, 0.0).astype(BF16)
        v_spread = jnp.concatenate(
            [jnp.concatenate([v_bf[rs, sl] if c == d else zero_blk for d in range(n_chunks)], axis=1)
             for c, rs in enumerate(chunk_rows)], axis=0)
        kv = _dot_tn(v_spread, k_d[:, sl])
        s_t = state_ref[h]
        states = []
        for c in range(n_chunks):
            states.append(s_t.astype(BF16))
            s_t = s_t * jnp.exp(ends[c][:, sl]) + kv[c * A_VAL:(c + 1) * A_VAL]
        state_ref[h] = s_t
        o_inter = jnp.concatenate([_dot_nt(q_e[rs, sl], st) for rs, st in zip(chunk_rows, states)], axis=0)
        o = _dot(att, v_bf[:, sl]) + o_inter
        outs.append(_rms(o, A_VAL) * og_ref[...])
    y = jnp.concatenate(outs, axis=1) * jax.nn.sigmoid(g_ref[...].astype(F32))
    y_ref[...] = y.astype(y_ref.dtype)

    @pl.when(t_idx == pl.num_programs(1) - 1)
    def _():
        for h in range(A_HEADS):
            st_ref[h] = state_ref[h].T


def hgrn2(proj, lb, out_g, s0, tt_cap=256):
    bsz, t, _ = proj.shape
    blk = min(HGRN_BLOCK, t)
    tt = _row_tile(t, tt_cap)
    nb = _row_tile(bsz, max(1, HGRN_STEP_ROWS // tt))
    has_past = s0 is not None
    if s0 is None:
        s0 = jnp.zeros((bsz, A_HEADS, A_KEY, A_VAL), F32)
    seg = lambda s: pl.BlockSpec((nb, tt, A_WIDTH), lambda b, i, s=s: (b, i, s))
    kern = functools.partial(_hgrn_kernel, tt=tt, blk=blk, has_past=has_past)
    return pl.pallas_call(
        kern,
        grid=(bsz // nb, t // tt),
        in_specs=[
            seg(0), seg(1), seg(2), seg(3),
            pl.BlockSpec((1, A_QK), lambda b, i: (0, 0)),
            pl.BlockSpec((1, A_VAL), lambda b, i: (0, 0)),
            pl.BlockSpec((nb, A_HEADS, A_KEY, A_VAL), lambda b, i: (b, 0, 0, 0)),
        ],
        out_specs=[
            pl.BlockSpec((nb, tt, A_WIDTH), lambda b, i: (b, i, 0)),
            pl.BlockSpec((nb, A_HEADS, A_KEY, A_VAL), lambda b, i: (b, 0, 0, 0)),
        ],
        out_shape=[
            jax.ShapeDtypeStruct((bsz, t, A_WIDTH), BF16),
            jax.ShapeDtypeStruct((bsz, A_HEADS, A_KEY, A_VAL), F32),
        ],
        scratch_shapes=[pltpu.VMEM((nb, A_HEADS, A_VAL, A_KEY), F32)],
        compiler_params=_cparams(("parallel", "arbitrary")),
        name="hgrn2",
    )(proj, proj, proj, proj, lb.reshape(1, A_QK), out_g.reshape(1, A_VAL), s0)


S5_SUPER = 4
S5_SUPER_U = B_WIDTH // S5_SUPER
S5_SUPER_X = 2 * (B_GROUPS // S5_SUPER) * B_STATE
S5_HALF = S5_SUPER_X // 2
S5_STATE_W = S5_SUPER * S5_SUPER_X
S5_SCAN_GROUP = 2


def _s5_kernel(u_ref, wb_ref, wc_ref, lam_ref, d_ref, wglu_ref, bglu_ref, x0_ref,
               o_ref, xt_ref, xbuf_ref, state_ref, *, tt):
    t_idx = pl.program_id(1)
    rows = tt * SUBLANES

    @pl.when(t_idx == 0)
    def _():
        state_ref[...] = x0_ref[...]

    u_bf = u_ref[...].reshape(rows, B_WIDTH)
    u = u_bf.astype(F32)
    for j in range(S5_SUPER):
        xbuf_ref[:, j * S5_SUPER_X:(j + 1) * S5_SUPER_X] = _dot(
            u_bf[:, j * S5_SUPER_U:(j + 1) * S5_SUPER_U], wb_ref[j])

    for j0 in range(0, S5_SUPER, S5_SCAN_GROUP):
        cols, lams = [], []
        for j in range(j0, j0 + S5_SCAN_GROUP):
            re_cols = slice(j * S5_SUPER_X, j * S5_SUPER_X + S5_HALF)
            im_cols = slice(j * S5_SUPER_X + S5_HALF, (j + 1) * S5_SUPER_X)
            cols.append((re_cols, im_cols))
            lams.append((jnp.broadcast_to(lam_ref[0:1, re_cols], (SUBLANES, S5_HALF)),
                         jnp.broadcast_to(lam_ref[1:2, re_cols], (SUBLANES, S5_HALF))))

        def step(t, carry, cols=cols, lams=lams):
            rows_t = pl.ds(pl.multiple_of(t * SUBLANES, SUBLANES), SUBLANES)
            out = []
            for (re_cols, im_cols), (lam_re, lam_im), (x_re, x_im) in zip(cols, lams, carry):
                n_re = lam_re * x_re - lam_im * x_im + xbuf_ref[rows_t, re_cols]
                n_im = lam_re * x_im + lam_im * x_re + xbuf_ref[rows_t, im_cols]
                xbuf_ref[rows_t, re_cols] = n_re
                xbuf_ref[rows_t, im_cols] = n_im
                out.append((n_re, n_im))
            return tuple(out)

        init = tuple((state_ref[:, rc], state_ref[:, ic]) for rc, ic in cols)
        last = lax.fori_loop(0, tt, step, init, unroll=True)
        for (rc, ic), (x_re, x_im) in zip(cols, last):
            state_ref[:, rc] = x_re
            state_ref[:, ic] = x_im

    ys = []
    for j in range(S5_SUPER):
        xs = xbuf_ref[:, j * S5_SUPER_X:(j + 1) * S5_SUPER_X].astype(BF16)
        ys.append(_dot(xs, wc_ref[j]))
    y = jnp.concatenate(ys, axis=1) + d_ref[...] * u
    z = jax.nn.gelu(y)
    gate = jax.nn.sigmoid(_dot(z.astype(BF16), wglu_ref[...]) + bglu_ref[...])
    o_ref[...] = (z * gate).astype(o_ref.dtype).reshape(tt, SUBLANES, B_WIDTH)

    @pl.when(t_idx == pl.num_programs(1) - 1)
    def _():
        xt_ref[...] = state_ref[...]


def s5_mix(u_tb, wb, wc, lam, d, w_glu, b_glu, x0, tt_cap=64):
    t, bsz, _ = u_tb.shape
    tt = _row_tile(t, tt_cap)
    kern = functools.partial(_s5_kernel, tt=tt)
    return pl.pallas_call(
        kern,
        grid=(bsz // SUBLANES, t // tt),
        in_specs=[
            pl.BlockSpec((tt, SUBLANES, B_WIDTH), lambda b, i: (i, b, 0)),
            pl.BlockSpec((S5_SUPER, S5_SUPER_U, S5_SUPER_X), lambda b, i: (0, 0, 0)),
            pl.BlockSpec((S5_SUPER, S5_SUPER_X, S5_SUPER_U), lambda b, i: (0, 0, 0)),
            pl.BlockSpec((2, S5_STATE_W), lambda b, i: (0, 0)),
            pl.BlockSpec((1, B_WIDTH), lambda b, i: (0, 0)),
            pl.BlockSpec((B_WIDTH, B_WIDTH), lambda b, i: (0, 0)),
            pl.BlockSpec((1, B_WIDTH), lambda b, i: (0, 0)),
            pl.BlockSpec((SUBLANES, S5_STATE_W), lambda b, i: (b, 0)),
        ],
        out_specs=[
            pl.BlockSpec((tt, SUBLANES, B_WIDTH), lambda b, i: (i, b, 0)),
            pl.BlockSpec((SUBLANES, S5_STATE_W), lambda b, i: (b, 0)),
        ],
        out_shape=[
            jax.ShapeDtypeStruct((t, bsz, B_WIDTH), BF16),
            jax.ShapeDtypeStruct((bsz, S5_STATE_W), F32),
        ],
        scratch_shapes=[pltpu.VMEM((tt * SUBLANES, S5_STATE_W), F32),
                        pltpu.VMEM((SUBLANES, S5_STATE_W), F32)],
        compiler_params=_cparams(("parallel", "arbitrary")),
        name="s5_mix",
    )(u_tb, wb, wc, lam, d.reshape(1, B_WIDTH), w_glu, b_glu.reshape(1, B_WIDTH), x0)


def _s5_pack_cols(a):
    return a.reshape(a.shape[:-2] + (S5_SUPER, S5_HALF))


def s5_prepare(lam_re, lam_im, log_dt, b_re, b_im, c_re, c_im):
    lr = jnp.minimum(lam_re, LAMBDA_RE_CEIL)
    li = lam_im
    dt = jnp.exp(log_dt)[:, None]
    mag = jnp.exp(lr * dt)
    bar_re = mag * jnp.cos(li * dt)
    bar_im = mag * jnp.sin(li * dt)
    den = lr * lr + li * li
    coef_re = (((bar_re - 1.0) * lr + bar_im * li) / den)[:, :, None]
    coef_im = ((bar_im * lr - (bar_re - 1.0) * li) / den)[:, :, None]
    bb_re = coef_re * b_re - coef_im * b_im
    bb_im = coef_re * b_im + coef_im * b_re
    gps = B_GROUPS // S5_SUPER
    eye = jnp.eye(gps, dtype=F32)

    def in_block(bm):
        bm = bm.reshape(S5_SUPER, gps, B_STATE, B_GROUP_DIM)
        blk = jnp.einsum('jgph,gk->jghkp', bm, eye)
        return blk.reshape(S5_SUPER, gps * B_GROUP_DIM, gps * B_STATE)

    def out_block(cm):
        cm = cm.reshape(S5_SUPER, gps, B_GROUP_DIM, B_STATE)
        blk = jnp.einsum('jghp,gk->jkpgh', cm, eye)
        return blk.reshape(S5_SUPER, gps * B_STATE, gps * B_GROUP_DIM)

    wb = jnp.concatenate([in_block(bb_re), in_block(bb_im)], axis=2).astype(BF16)
    wc = jnp.concatenate([out_block(c_re), out_block(-c_im)], axis=1).astype(BF16)
    lam_rows = jnp.stack([
        jnp.concatenate([_s5_pack_cols(bar_re)] * 2, axis=-1).reshape(S5_STATE_W),
        jnp.concatenate([_s5_pack_cols(bar_im)] * 2, axis=-1).reshape(S5_STATE_W)])
    return wb, wc, lam_rows


def s5_pack_state(x_re, x_im):
    bsz = x_re.shape[0]
    return jnp.concatenate([_s5_pack_cols(x_re), _s5_pack_cols(x_im)], axis=-1).reshape(bsz, S5_STATE_W)


def s5_unpack_state(x):
    bsz = x.shape[0]
    x = x.reshape(bsz, S5_SUPER, 2, S5_HALF)
    return (x[:, :, 0].reshape(bsz, B_GROUPS, B_STATE), x[:, :, 1].reshape(bsz, B_GROUPS, B_STATE))


def _gmlp_kernel(u_ref, v_ref, gn_ref, ws_ref, bs_ref, o_ref, vr_ref, *, tt, chunk, all_rows):
    n_chunks = tt // chunk
    u = jax.nn.gelu(u_ref[...].astype(F32))
    v = jax.nn.gelu(v_ref[...].astype(F32))
    vn = _rms(v, C_WIDTH) * gn_ref[...]
    vn_bf = vn.astype(BF16)
    row = lax.broadcasted_iota(jnp.int32, (chunk, chunk), 0)
    col = lax.broadcasted_iota(jnp.int32, (chunk, chunk), 1)
    keep = col <= row
    for g in range(C_GROUPS):
        w = jnp.where(keep, ws_ref[g], 0.0).astype(BF16)
        bias = bs_ref[:, g:g + 1]
        cs = slice(g * C_GROUP_DIM, (g + 1) * C_GROUP_DIM)
        for c in range(n_chunks):
            rs = slice(c * chunk, (c + 1) * chunk)
            s = _dot(w, vn_bf[rs, cs]) + bias
            o_ref[rs, cs] = (u[rs, cs] * s).astype(o_ref.dtype)

    if all_rows:
        vr_ref[...] = vn
    else:
        @pl.when(pl.program_id(1) == pl.num_programs(1) - 1)
        def _():
            vr_ref[...] = vn[tt - chunk:, :]


def gmlp(proj, v_norm, w_s, b_s, tt_cap=1024):
    bsz, t, width = proj.shape
    chunk = min(t, C_CHUNK)
    all_rows = t == chunk
    if all_rows:
        c_out, v_rows = gmlp_call(proj.reshape(1, bsz * t, width), v_norm, w_s, b_s, chunk, True, tt_cap)
        return c_out.reshape(bsz, t, C_WIDTH), v_rows.reshape(bsz, t, C_WIDTH)
    return gmlp_call(proj, v_norm, w_s, b_s, chunk, False, tt_cap)


def gmlp_call(proj, v_norm, w_s, b_s, chunk, all_rows, tt_cap):
    bsz, t, _ = proj.shape
    tt = _row_tile(t, tt_cap)
    vr_rows = tt if all_rows else chunk
    ws = w_s[:, :chunk, :chunk]
    bs_t = b_s[:, :chunk].T
    kern = functools.partial(_gmlp_kernel, tt=tt, chunk=chunk, all_rows=all_rows)
    return pl.pallas_call(
        kern,
        grid=(bsz, t // tt),
        in_specs=[
            pl.BlockSpec((None, tt, C_WIDTH), lambda b, i: (b, i, 0)),
            pl.BlockSpec((None, tt, C_WIDTH), lambda b, i: (b, i, 1)),
            pl.BlockSpec((1, C_WIDTH), lambda b, i: (0, 0)),
            pl.BlockSpec((C_GROUPS, chunk, chunk), lambda b, i: (0, 0, 0)),
            pl.BlockSpec((chunk, C_GROUPS), lambda b, i: (0, 0)),
        ],
        out_specs=[
            pl.BlockSpec((None, tt, C_WIDTH), lambda b, i: (b, i, 0)),
            pl.BlockSpec((None, vr_rows, C_WIDTH), (lambda b, i: (b, i, 0)) if all_rows else (lambda b, i: (b, 0, 0))),
        ],
        out_shape=[
            jax.ShapeDtypeStruct((bsz, t, C_WIDTH), BF16),
            jax.ShapeDtypeStruct((bsz, t if all_rows else chunk, C_WIDTH), F32),
        ],
        compiler_params=_cparams(("parallel", "arbitrary")),
        name="gmlp",
    )(proj, proj, v_norm.reshape(1, C_WIDTH), ws, bs_t)


def _rope_lanes(x, cos_t, sin_lo, sin_hi):
    half = ROPE_DIM // 2
    return (x * cos_t + pltpu.roll(x, LANES - half, axis=1) * sin_lo
            + pltpu.roll(x, half, axis=1) * sin_hi)


def _mla_q_kernel(cq_ref, ckv_ref, kpe_ref, qn_ref, kvn_ref, wuq_ref, qg_ref,
                  qc_ref, qs_ref, kc_ref, ks1_ref, ks2_ref, *rest, with_kv):
    if with_kv:
        wk_ref, wvt_ref, kg_ref, q_ref, ckv_o_ref, kpe_o_ref, k_ref, vt_ref = rest
    else:
        q_ref, ckv_o_ref, kpe_o_ref = rest
    cqn = (_rms(cq_ref[...].astype(F32), Q_LORA) * qn_ref[...]).astype(BF16)
    width = D_HEADS * HEAD_PAD
    q_all = _dot(cqn, wuq_ref[...])
    qc, qs = qc_ref[...], qs_ref[...]
    scale = QK_DIM ** -0.5 * math.log2(math.e)
    for h in range(D_HEADS):
        sl = slice(h * HEAD_PAD, (h + 1) * HEAD_PAD)
        rot = slice(width + h * HEAD_PAD, width + (h + 1) * HEAD_PAD)
        qh = q_all[:, sl] * qc + q_all[:, rot] * qs
        qh = _rms(qh, QK_DIM) * (qg_ref[...] * scale)
        q_ref[:, sl] = qh.astype(q_ref.dtype)
    ckv = _rms(ckv_ref[...].astype(F32), KV_LORA) * kvn_ref[...]
    kpe = _rope_lanes(kpe_ref[...].astype(F32), kc_ref[...], ks1_ref[...], ks2_ref[...])
    ckv_o_ref[...] = ckv
    kpe_o_ref[...] = kpe[:, :ROPE_DIM]
    if with_kv:
        ckv_bf = ckv.astype(BF16)
        ckpe = jnp.concatenate([ckv_bf, kpe.astype(BF16)], axis=1)
        k_all = _dot(ckpe, wk_ref[...])
        for h in range(D_HEADS):
            sl = slice(h * HEAD_PAD, (h + 1) * HEAD_PAD)
            k_ref[:, sl] = (_rms(k_all[:, sl], QK_DIM) * kg_ref[...]).astype(k_ref.dtype)
        row = lax.broadcasted_iota(jnp.int32, (D_HEADS * HEAD_PAD, 1), 0)
        ones_row = jnp.where(_mod_pow2(row, HEAD_PAD) == V_ONE_ROW, 1.0, 0.0)
        for blk in range(vt_ref.shape[0]):
            rows = slice(blk * ATTN_TK, (blk + 1) * ATTN_TK)
            vt_ref[blk] = (_dot_nt(wvt_ref[...], ckv_bf[rows]) + ones_row).astype(vt_ref.dtype)


MLA_Q_ROWS = 1024


def mla_q(proj, tables, q_norm, kv_norm, wuq_p, q_gain_p, kv_weights=None, tm_cap=MLA_Q_ROWS):
    bsz, t, _ = proj.shape
    tm = _row_tile(t, tm_cap)
    width = D_HEADS * HEAD_PAD
    tab = pl.BlockSpec((tm, LANES), lambda b, i: (i, 0))
    row = lambda w: pl.BlockSpec((1, w), lambda b, i: (0, 0))
    full = lambda r, c: pl.BlockSpec((r, c), lambda b, i: (0, 0))
    with_kv = kv_weights is not None
    args = [proj, proj, proj, q_norm.reshape(1, Q_LORA), kv_norm.reshape(1, KV_LORA), wuq_p, q_gain_p, *tables]
    in_specs = [
        pl.BlockSpec((None, tm, Q_LORA), lambda b, i: (b, i, 2 * C_WIDTH // Q_LORA)),
        pl.BlockSpec((None, tm, KV_LORA), lambda b, i: (b, i, (2 * C_WIDTH + Q_LORA) // KV_LORA)),
        pl.BlockSpec((None, tm, LANES), lambda b, i: (b, i, (2 * C_WIDTH + Q_LORA + KV_LORA) // LANES)),
        row(Q_LORA), row(KV_LORA), full(Q_LORA, 2 * width), row(HEAD_PAD),
        tab, tab, tab, tab, tab,
    ]
    out_specs = [
        pl.BlockSpec((None, tm, width), lambda b, i: (b, i, 0)),
        pl.BlockSpec((None, tm, KV_LORA), lambda b, i: (b, i, 0)),
        pl.BlockSpec((None, tm, ROPE_DIM), lambda b, i: (b, i, 0)),
    ]
    out_shape = [
        jax.ShapeDtypeStruct((bsz, t, width), BF16),
        jax.ShapeDtypeStruct((bsz, t, KV_LORA), F32),
        jax.ShapeDtypeStruct((bsz, t, ROPE_DIM), F32),
    ]
    if with_kv:
        wk_full, wv_p, k_gain_p = kv_weights
        assert tm % ATTN_TK == 0
        args += [wk_full, wv_p.T, k_gain_p]
        in_specs += [full(*wk_full.shape), full(width, KV_LORA), row(HEAD_PAD)]
        out_specs += [pl.BlockSpec((None, tm, width), lambda b, i: (b, i, 0)),
                      pl.BlockSpec((None, tm // ATTN_TK, width, ATTN_TK), lambda b, i: (b, i, 0, 0))]
        out_shape += [jax.ShapeDtypeStruct((bsz, t, width), BF16),
                      jax.ShapeDtypeStruct((bsz, t // ATTN_TK, width, ATTN_TK), BF16)]
    return pl.pallas_call(
        functools.partial(_mla_q_kernel, with_kv=with_kv),
        grid=(bsz, t // tm),
        in_specs=in_specs,
        out_specs=out_specs,
        out_shape=out_shape,
        compiler_params=_cparams(("parallel", "parallel")),
        name="mla_q",
    )(*args)


def _for_blocks(lo, hi, block_fn):
    n_pairs = (hi - lo) // 2

    def pair(i, carry):
        block_fn(lo + 2 * i)
        block_fn(lo + 2 * i + 1)
        return carry

    lax.fori_loop(0, n_pairs, pair, 0)

    @pl.when(lo + 2 * n_pairs < hi)
    def _():
        block_fn(hi - 1)


def _attn_kernel(q_ref, k_ref, vt_ref, o_ref, sc_ref, mx_ref, acc_ref, *, tq, tk, s_len, q_pos0, diag_split):
    qi = pl.program_id(1)
    q_first = q_pos0 + qi * tq
    q_last = q_first + tq - 1
    vis_first = jnp.minimum(s_len, (q_first // CHUNK + 1) * CHUNK)
    vis_last = jnp.minimum(s_len, (q_last // CHUNK + 1) * CHUNK)
    n_full = vis_first // tk
    n_kv = (vis_last + tk - 1) // tk

    q_chunk = _div_pow2(q_first + lax.broadcasted_iota(jnp.int32, (tk, tq), 1), CHUNK)
    k_iota = lax.broadcasted_iota(jnp.int32, (tk, tq), 0)

    half = tk // 2
    q_chunk_lo = _div_pow2(q_first + lax.broadcasted_iota(jnp.int32, (half, tq), 1), CHUNK)
    q_chunk_hi = _div_pow2(q_first + half + lax.broadcasted_iota(jnp.int32, (half, tq - half), 1), CHUNK)
    k_iota_lo = lax.broadcasted_iota(jnp.int32, (half, tq), 0)
    k_iota_hi = lax.broadcasted_iota(jnp.int32, (half, tq - half), 0)

    for g0 in range(0, D_HEADS, ATTN_HEAD_GROUP):
        heads = [slice(h * HEAD_PAD, (h + 1) * HEAD_PAD) for h in range(g0, g0 + ATTN_HEAD_GROUP)]

        def track_max(h, s, cols=slice(None)):
            blk_max = jnp.max(s.reshape(s.shape[0] // SUBLANES, SUBLANES, s.shape[1]), axis=0)
            mx_ref[h, :, cols] = jnp.maximum(mx_ref[h, :, cols], blk_max)

        def score_block(j, masked, heads=heads):
            k0 = pl.multiple_of(j * tk, tk)
            visible = (_div_pow2(k0 + k_iota, CHUNK) <= q_chunk) if masked else None
            for h, sl in enumerate(heads):
                s = _dot_nt(k_ref[pl.ds(k0, tk), sl], q_ref[:, sl])
                if masked:
                    s = jnp.where(visible, s, NEG_BIG)
                sc_ref[h, j] = s
                track_max(h, s)

        def sum_block(j, heads=heads):
            for h, sl in enumerate(heads):
                p = jnp.exp2(sc_ref[h, j] - mx_ref[h][0:1, :])
                acc_ref[h] += _dot(vt_ref[j, sl, :], p.astype(BF16))

        def score_diag(j, heads=heads):
            k0 = pl.multiple_of(j * tk, tk)
            k1 = pl.multiple_of(k0 + half, half)
            vis0 = _div_pow2(k0 + k_iota_lo, CHUNK) <= q_chunk_lo
            vis1 = _div_pow2(k1 + k_iota_hi, CHUNK) <= q_chunk_hi
            for h, sl in enumerate(heads):
                s0 = jnp.where(vis0, _dot_nt(k_ref[pl.ds(k0, half), sl], q_ref[:, sl]), NEG_BIG)
                s1 = jnp.where(vis1, _dot_nt(k_ref[pl.ds(k1, half), sl], q_ref[half:, sl]), NEG_BIG)
                sc_ref[h, j, :half, :] = s0
                sc_ref[h, j, half:, half:] = s1
                track_max(h, s0)
                track_max(h, s1, slice(half, None))

        def sum_diag(j, heads=heads):
            for h, sl in enumerate(heads):
                m = mx_ref[h][0:1, :]
                p0 = jnp.exp2(sc_ref[h, j, :half, :] - m)
                p1 = jnp.exp2(sc_ref[h, j, half:, half:] - m[:, half:])
                acc_ref[h] += _dot(vt_ref[j, sl, :half], p0.astype(BF16))
                acc_ref[h, :, half:] += _dot(vt_ref[j, sl, half:], p1.astype(BF16))

        mx_ref[...] = jnp.full_like(mx_ref, NEG_BIG)
        _for_blocks(0, n_full, functools.partial(score_block, masked=False))
        if diag_split:
            score_diag(n_full)
        else:
            _for_blocks(n_full, n_kv, functools.partial(score_block, masked=True))
        for h in range(ATTN_HEAD_GROUP):
            m = jnp.max(mx_ref[h], axis=0, keepdims=True)
            mx_ref[h] = jnp.broadcast_to(m, (SUBLANES, tq))
        acc_ref[...] = jnp.zeros_like(acc_ref)
        if diag_split:
            _for_blocks(0, n_full, sum_block)
            sum_diag(n_full)
        else:
            _for_blocks(0, n_kv, sum_block)
        for hp in range(0, ATTN_HEAD_GROUP, 2):
            pair = [acc_ref[h][:V_DIM, :] / acc_ref[h][V_ONE_ROW:V_ONE_ROW + 1, :] for h in (hp, hp + 1)]
            lanes = slice((g0 + hp) * V_DIM, (g0 + hp + 2) * V_DIM)
            o_ref[:, lanes] = jnp.concatenate(pair, axis=0).T.astype(o_ref.dtype)


def attention(q, k, vt, q_pos0, tq_cap=512):
    bsz, tq_len, width = q.shape
    s_len = k.shape[1]
    n_blocks, _, tk = vt.shape[1:]
    assert n_blocks * tk == s_len and tk % LANES == 0
    tq = _row_tile(tq_len, tq_cap)
    diag_split = tq == tk and q_pos0 % tq == 0 and s_len >= q_pos0 + tq_len
    kern = functools.partial(_attn_kernel, tq=tq, tk=tk, s_len=s_len, q_pos0=q_pos0, diag_split=diag_split)
    return pl.pallas_call(
        kern,
        grid=(bsz, tq_len // tq),
        in_specs=[
            pl.BlockSpec((None, tq, width), lambda b, i: (b, i, 0)),
            pl.BlockSpec((None, s_len, width), lambda b, i: (b, 0, 0)),
            pl.BlockSpec((None, n_blocks, width, tk), lambda b, i: (b, 0, 0, 0)),
        ],
        out_specs=pl.BlockSpec((None, tq, D_HEADS * V_DIM), lambda b, i: (b, i, 0)),
        out_shape=jax.ShapeDtypeStruct((bsz, tq_len, D_HEADS * V_DIM), BF16),
        scratch_shapes=[pltpu.VMEM((ATTN_HEAD_GROUP, n_blocks, tk, tq), F32),
                        pltpu.VMEM((ATTN_HEAD_GROUP, SUBLANES, tq), F32),
                        pltpu.VMEM((ATTN_HEAD_GROUP, HEAD_PAD, tq), F32)],
        compiler_params=_cparams(("parallel", "arbitrary")),
        name="attention",
    )(q, k, vt)


def _mla_decode_kernel(q_ref, ckv_c_ref, kpe_c_ref, ckv_n_ref, kpe_n_ref, wk_ref, wkt_ref, wv_ref, kg_ref,
                       o_ref, *, tq, s_len, q_pos0):
    s_pad = ckv_c_ref.shape[0] + ckv_n_ref.shape[0]
    heads = [slice(h * HEAD_PAD, (h + 1) * HEAD_PAD) for h in range(D_HEADS)]
    ckv = jnp.concatenate([ckv_c_ref[...], ckv_n_ref[...]], axis=0).astype(BF16)
    kpe_raw = jnp.concatenate([kpe_c_ref[...], kpe_n_ref[...]], axis=0).astype(BF16)
    lane = lax.broadcasted_iota(jnp.int32, (ROPE_DIM, LANES), 1)
    src = lax.broadcasted_iota(jnp.int32, (ROPE_DIM, LANES), 0)
    place = jnp.where(lane == src + NOPE_DIM, 1.0, 0.0).astype(BF16)
    kpe = _dot(kpe_raw, place).astype(BF16)
    eye = (lax.broadcasted_iota(jnp.int32, (LANES, LANES), 0)
           == lax.broadcasted_iota(jnp.int32, (LANES, LANES), 1))
    kpe_t = _dot_nt(jnp.where(eye, 1.0, 0.0).astype(BF16), kpe)
    pe_ss = jnp.sum(kpe_t * kpe_t, axis=0, keepdims=True)
    kn_t = _dot_nt(wkt_ref[...], ckv)
    inv_rms, q_lat, q_g = [], [], []
    for h, sl in enumerate(heads):
        blk = kn_t[h * NOPE_DIM:(h + 1) * NOPE_DIM, :]
        ss = jnp.sum(blk * blk, axis=0, keepdims=True) + pe_ss
        inv_rms.append(jnp.broadcast_to(lax.rsqrt(ss * (1.0 / QK_DIM) + EPS), (tq, s_pad)))
        qg = (q_ref[:, sl].astype(F32) * kg_ref[...]).astype(BF16)
        q_g.append(qg)
        q_lat.append(_dot_nt(qg, wk_ref[:, sl]).astype(BF16))
    s = _dot_nt(jnp.concatenate(q_lat, axis=0), ckv) + _dot_nt(jnp.concatenate(q_g, axis=0), kpe)
    s = s * jnp.concatenate(inv_rms, axis=0)
    row = lax.broadcasted_iota(jnp.int32, s.shape, 0)
    col = lax.broadcasted_iota(jnp.int32, s.shape, 1)
    q_chunk = _div_pow2(q_pos0 + _mod_pow2(row, tq), CHUNK)
    visible = (_div_pow2(col, CHUNK) <= q_chunk) & (col < s_len)
    s = jnp.where(visible, s, NEG_BIG)
    p = jnp.exp2(s - jnp.max(s, axis=-1, keepdims=True))
    lat = _dot(p.astype(BF16), ckv) / jnp.sum(p, axis=-1, keepdims=True)
    lat = lat.astype(BF16)
    for hp in range(D_HEADS // 2):
        both = jnp.concatenate([lat[2 * hp * tq:(2 * hp + 1) * tq], lat[(2 * hp + 1) * tq:(2 * hp + 2) * tq]], axis=1)
        o_ref[:, hp * LANES:(hp + 1) * LANES] = _dot(both, wv_ref[hp]).astype(o_ref.dtype)


def mla_decode(q, cache_ckv, cache_kpe, ckv_new, kpe_new, wk_p, wv_pairs, k_gain_p, q_pos0):
    bsz, tq, width = q.shape
    n_cache, n_new = cache_ckv.shape[1], ckv_new.shape[1]
    assert n_cache % LANES == 0 and tq & (tq - 1) == 0
    s_len = n_cache + n_new
    new_pad = -(-n_new // LANES) * LANES
    ckv_new = jnp.pad(ckv_new, ((0, 0), (0, new_pad - n_new), (0, 0)))
    kpe_new = jnp.pad(kpe_new, ((0, 0), (0, new_pad - n_new), (0, 0)))
    wk_rows = wk_p.T.reshape(D_HEADS, HEAD_PAD, KV_LORA)[:, :NOPE_DIM].reshape(D_HEADS * NOPE_DIM, KV_LORA)
    full = lambda r, c: pl.BlockSpec((r, c), lambda b: (0, 0))
    per_b = lambda r, c: pl.BlockSpec((None, r, c), lambda b: (b, 0, 0))
    kern = functools.partial(_mla_decode_kernel, tq=tq, s_len=s_len, q_pos0=q_pos0)
    return pl.pallas_call(
        kern,
        grid=(bsz,),
        in_specs=[
            per_b(tq, width), per_b(n_cache, KV_LORA), per_b(n_cache, ROPE_DIM),
            per_b(new_pad, KV_LORA), per_b(new_pad, ROPE_DIM),
            full(KV_LORA, width), full(D_HEADS * NOPE_DIM, KV_LORA),
            pl.BlockSpec((D_HEADS // 2, 2 * KV_LORA, LANES), lambda b: (0, 0, 0)), full(1, HEAD_PAD),
        ],
        out_specs=per_b(tq, D_HEADS * V_DIM),
        out_shape=jax.ShapeDtypeStruct((bsz, tq, D_HEADS * V_DIM), BF16),
        compiler_params=_cparams(("parallel",)),
        name="mla_decode",
    )(q, cache_ckv, cache_kpe, ckv_new, kpe_new, wk_p, wk_rows, wv_pairs, k_gain_p)


def rope_tables(pos):
    half = ROPE_DIM // 2
    inv = ROPE_BASE ** (-jnp.arange(half, dtype=F32) / half)
    ang = pos.astype(F32)[:, None] * inv[None, :]
    cos, sin = jnp.cos(ang), jnp.sin(ang)
    n = pos.shape[0]
    z = lambda w: jnp.zeros((n, w), F32)
    o = lambda w: jnp.ones((n, w), F32)
    tail = LANES - NOPE_DIM - ROPE_DIM
    q_cos = jnp.concatenate([o(NOPE_DIM), cos, cos, z(tail)], axis=1)
    q_sin = jnp.concatenate([z(NOPE_DIM), sin, sin, z(tail)], axis=1)
    k_cos = jnp.concatenate([cos, cos, z(LANES - ROPE_DIM)], axis=1)
    k_s1 = jnp.concatenate([-sin, z(LANES - half)], axis=1)
    k_s2 = jnp.concatenate([z(half), sin, z(LANES - ROPE_DIM)], axis=1)
    return q_cos, q_sin, k_cos, k_s1, k_s2


def pack_mla_weights(w_uq, w_ukv, q_gain, k_gain, cd_w_out):
    pad_q = HEAD_PAD - QK_DIM
    half = ROPE_DIM // 2
    wq = w_uq.reshape(Q_LORA, D_HEADS, QK_DIM)
    wuq_p = jnp.pad(wq, ((0, 0), (0, 0), (0, pad_q))).reshape(Q_LORA, D_HEADS * HEAD_PAD)
    wq_rot = jnp.concatenate([jnp.zeros_like(wq[:, :, :NOPE_DIM]), -wq[:, :, NOPE_DIM + half:],
                              wq[:, :, NOPE_DIM:NOPE_DIM + half], jnp.zeros_like(wq[:, :, :pad_q])], axis=2)
    wuq_p = jnp.concatenate([wuq_p, wq_rot.reshape(Q_LORA, D_HEADS * HEAD_PAD)], axis=1).astype(BF16)
    wkv = w_ukv.reshape(KV_LORA, D_HEADS, NOPE_DIM + V_DIM)
    wk_p = jnp.pad(wkv[:, :, :NOPE_DIM], ((0, 0), (0, 0), (0, HEAD_PAD - NOPE_DIM)))
    wk_p = wk_p.reshape(KV_LORA, D_HEADS * HEAD_PAD).astype(BF16)
    wv_p = jnp.pad(wkv[:, :, NOPE_DIM:], ((0, 0), (0, 0), (0, HEAD_PAD - V_DIM)))
    wv_p = wv_p.reshape(KV_LORA, D_HEADS * HEAD_PAD).astype(BF16)
    place = jnp.pad(jnp.eye(ROPE_DIM, dtype=F32), ((0, LANES - ROPE_DIM), (NOPE_DIM, HEAD_PAD - QK_DIM)))
    wk_full = jnp.concatenate([wk_p, jnp.tile(place, (1, D_HEADS)).astype(BF16)], axis=0)
    q_gain_p = jnp.pad(q_gain, (0, pad_q)).reshape(1, HEAD_PAD)
    k_gain_p = jnp.pad(k_gain, (0, pad_q)).reshape(1, HEAD_PAD)
    wv = wkv[:, :, NOPE_DIM:].reshape(KV_LORA, D_HEADS // 2, 2, V_DIM)
    zero = jnp.zeros_like(wv[:, :, 0])
    wv_pairs = jnp.concatenate([jnp.concatenate([wv[:, :, 0], zero], axis=2),
                                jnp.concatenate([zero, wv[:, :, 1]], axis=2)], axis=0)
    wv_pairs = jnp.transpose(wv_pairs, (1, 0, 2)).astype(BF16)
    w_c = cd_w_out[:C_WIDTH].astype(BF16)
    w_d = cd_w_out[C_WIDTH:].astype(BF16)
    return wuq_p, wk_p, wk_full, wv_p, wv_pairs, q_gain_p, k_gain_p, w_c, w_d


def _mix_even(proj, tail, w, lb, past, tmaj):
    bsz, t, _ = proj.shape
    n = bsz * t
    s0 = None if past is None else past[0][0]
    a_out, s_t = hgrn2(proj, lb, w['hgrn_out_norm'][0], s0)
    x0 = jnp.zeros((bsz, S5_STATE_W), F32) if past is None else s5_pack_state(past[1][0], past[2][0])
    if tmaj is not None:
        u_tb = tail.reshape(t, bsz, B_WIDTH)
    else:
        u_tb = jnp.transpose(proj[:, :, AB_IN - B_WIDTH:], (1, 0, 2))
    b_tb, x_t = s5_mix(u_tb, w['s5_wb'], w['s5_wc'], w['s5_lam'], w['s5_d'], w['s5_w_glu'], w['s5_b_glu'], x0)
    if tmaj is not None:
        b_out = b_tb.reshape(t, bsz * B_WIDTH)
    else:
        b_out = jnp.transpose(b_tb, (1, 0, 2)).reshape(n, B_WIDTH)
    s5re, s5im = s5_unpack_state(x_t)
    return a_out.reshape(n, A_WIDTH), b_out, dict(hgrn=s_t, s5re=s5re, s5im=s5im)


def _mix_odd(proj, w, pos0, past):
    bsz, t, _ = proj.shape
    n = bsz * t
    c_out, v_rows = gmlp(proj, w['gmlp_v_norm'], w['gmlp_w_s'], w['gmlp_b_s'])
    tables = rope_tables(pos0 + jnp.arange(t, dtype=jnp.int32))
    q_args = (w['mla_q_norm'], w['mla_kv_norm'], w['mla_wuq_p'], w['mla_q_gain_p'])
    if past is None:
        q, ckv, kpe, k, vt = mla_q(proj, tables, *q_args,
                                   kv_weights=(w['mla_wk_full'], w['mla_wv_p'], w['mla_k_gain_p']))
        d_out = attention(q, k, vt, pos0)
    else:
        flat = mla_q(proj.reshape(1, n, -1), [jnp.tile(tb, (bsz, 1)) for tb in tables], *q_args)
        q, ckv, kpe = [a.reshape(bsz, t, -1) for a in flat]
        d_out = mla_decode(q, past[3][0], past[4][0], ckv, kpe, w['mla_wk_p'], w['mla_wv_pairs'],
                           w['mla_k_gain_p'], pos0)
    return c_out.reshape(n, C_WIDTH), d_out.reshape(n, D_HEADS * V_DIM), dict(gv=v_rows, ckv=ckv, kpe=kpe)


def _trunks(streams, w):
    lb_all = jnp.cumsum(jax.nn.softmax(w['hgrn_lb_logits'], axis=0), axis=0)
    ffn_w = lambda name, l: (w[name + '_norm'][l], w[name + '_w_gate'][l], w[name + '_w_up'][l],
                             w[name + '_w_down'][l])
    shapes = [x.shape[:2] for x, _, _ in streams]
    tmajs = [(bsz, t) if t % FFN_ROWS == 0 else None for bsz, t in shapes]
    xs = [x.reshape(-1, D_MODEL) for x, _, _ in streams]
    outs = [{} for _ in streams]
    for l in range(2):
        even = l % 2 == 0
        w_in = w['ab_w_in'] if even else w['cd_w_in_p']
        res = ffn([dict(x=x, time_major=tm if even else None) for x, tm in zip(xs, tmajs)],
                  *ffn_w('ffn1', l), post=(w['mix_norm'][l], w_in))
        groups = []
        for r, (bsz, t), tm, (_, pos0, past), o in zip(res, shapes, tmajs, streams, outs):
            proj = r[1].reshape(bsz, t, -1)
            if even:
                a1, a2, new = _mix_even(proj, r[2] if tm is not None else None, w, lb_all[l], past, tm)
            else:
                a1, a2, new = _mix_odd(proj, w, pos0, past)
            o.update(new)
            groups.append(dict(x=r[0], a1=a1, a2=a2, time_major=tm if even else None))
        pre_w = (w['ab_w_out_a'], w['ab_w_out_b']) if even else (w['cd_w_out_c'], w['cd_w_out_d'])
        xs = [r[0] for r in ffn(groups, *ffn_w('ffn2', l), pre_w=pre_w)]
    return [x.reshape(bsz, t, D_MODEL) for x, (bsz, t) in zip(xs, shapes)], outs


def kernel(x_prompt, x_sample, state_hgrn, state_s5_re, state_s5_im, cache_mla_ckv, cache_mla_kpe, ffn1_norm, ffn1_w_gate, ffn1_w_up, ffn1_w_down, mix_norm, ffn2_norm, ffn2_w_gate, ffn2_w_up, ffn2_w_down, ab_w_in, hgrn_lb_logits, hgrn_out_norm, s5_lambda_re, s5_lambda_im, s5_log_dt, s5_b_re, s5_b_im, s5_c_re, s5_c_im, s5_d, s5_w_glu, s5_b_glu, ab_w_out, cd_w_in, gmlp_v_norm, gmlp_w_s, gmlp_b_s, mla_q_norm, mla_w_uq, mla_kv_norm, mla_w_ukv, mla_q_gain, mla_k_gain, cd_w_out):
    bf = lambda a: a.astype(BF16)
    wb, wc, lam_rows = s5_prepare(s5_lambda_re[0], s5_lambda_im[0], s5_log_dt[0],
                                  s5_b_re[0], s5_b_im[0], s5_c_re[0], s5_c_im[0])
    wuq_p, wk_p, wk_full, wv_p, wv_pairs, q_gain_p, k_gain_p, w_c, w_d = pack_mla_weights(
        mla_w_uq[0], mla_w_ukv[0], mla_q_gain[0], mla_k_gain[0], cd_w_out[0])
    w = dict(
        ffn1_norm=ffn1_norm, ffn1_w_gate=bf(ffn1_w_gate), ffn1_w_up=bf(ffn1_w_up), ffn1_w_down=bf(ffn1_w_down),
        ffn2_norm=ffn2_norm, ffn2_w_gate=bf(ffn2_w_gate), ffn2_w_up=bf(ffn2_w_up), ffn2_w_down=bf(ffn2_w_down),
        mix_norm=mix_norm, ab_w_in=bf(ab_w_in[0]), hgrn_lb_logits=hgrn_lb_logits, hgrn_out_norm=hgrn_out_norm,
        s5_wb=wb, s5_wc=wc, s5_lam=lam_rows, s5_d=s5_d[0].reshape(B_WIDTH), s5_w_glu=bf(s5_w_glu[0]),
        s5_b_glu=s5_b_glu[0], ab_w_out_a=bf(ab_w_out[0, :A_WIDTH]), ab_w_out_b=bf(ab_w_out[0, A_WIDTH:]),
        cd_w_in_p=bf(jnp.pad(cd_w_in[0], ((0, 0), (0, CD_IN_PAD - CD_IN)))),
        gmlp_v_norm=gmlp_v_norm[0], gmlp_w_s=gmlp_w_s[0], gmlp_b_s=gmlp_b_s[0],
        mla_q_norm=mla_q_norm[0], mla_kv_norm=mla_kv_norm[0], mla_wuq_p=wuq_p, mla_q_gain_p=q_gain_p,
        mla_wk_p=wk_p, mla_wk_full=wk_full, mla_wv_p=wv_p, mla_wv_pairs=wv_pairs, mla_k_gain_p=k_gain_p,
        cd_w_out_c=w_c, cd_w_out_d=w_d,
    )
    past_len = cache_mla_ckv.shape[2]
    (y_p, y_s), (o_p, o_s) = _trunks(
        [(x_prompt, 0, None),
         (x_sample, past_len, (state_hgrn, state_s5_re, state_s5_im, cache_mla_ckv, cache_mla_kpe))], w)
    e = lambda a: a[None]
    return (y_p, y_s, e(o_p['hgrn']), e(o_s['hgrn']), e(o_p['s5re']), e(o_p['s5im']),
            e(o_s['s5re']), e(o_s['s5im']), e(o_p['gv']), e(o_s['gv']),
            e(o_p['ckv']), e(o_p['kpe']), e(o_s['ckv']), e(o_s['kpe']))
```

```python
import functools
import math

import jax
import jax.numpy as jnp
from jax import lax
from jax.experimental import pallas as pl
from jax.experimental.pallas import tpu as pltpu

D_MODEL = 1024
D_FF = 2816
EPS = 1e-6
A_HEADS = 4
A_KEY = 128
A_VAL = 128
A_WIDTH = A_HEADS * A_VAL
HGRN_BLOCK = 64
B_GROUP_DIM = 16
B_WIDTH = 512
B_GROUPS = 32
B_STATE = 64
LAMBDA_RE_CEIL = -1e-4
C_CHUNK = 128
C_WIDTH = 512
C_GROUPS = 4
C_GROUP_DIM = 128
D_HEADS = 8
Q_LORA = 256
KV_LORA = 128
NOPE_DIM = 64
ROPE_DIM = 32
V_DIM = 64
QK_DIM = NOPE_DIM + ROPE_DIM
ROPE_BASE = 10000.0
CHUNK = 64
A_QK = A_HEADS * A_KEY
AB_IN = 2 * A_QK + 2 * A_WIDTH + B_WIDTH
CD_IN = 2 * C_WIDTH + Q_LORA + KV_LORA + ROPE_DIM
CD_IN_PAD = 1536

LANES = 128
SUBLANES = 8
VMEM_LIMIT = 56 * 1024 * 1024
HGRN_SUB = 16
HGRN_STEP_ROWS = 128
HEAD_PAD = LANES
NEG_BIG = -1e30
EXP_CLAMP = 80.0
V_ONE_ROW = V_DIM
ATTN_TK = 512
ATTN_HEAD_GROUP = 4

BF16 = jnp.bfloat16
F32 = jnp.float32


def _cparams(sem):
    return pltpu.CompilerParams(dimension_semantics=sem, vmem_limit_bytes=VMEM_LIMIT)


def _rms(xf, width):
    return xf * lax.rsqrt(jnp.sum(xf * xf, axis=-1, keepdims=True) * (1.0 / width) + EPS)


def _dot(a, b):
    return jnp.dot(a, b, preferred_element_type=F32)


def _dot_nt(a, b):
    return lax.dot_general(a, b, (((1,), (1,)), ((), ())), preferred_element_type=F32)


def _dot_tn(a, b):
    return lax.dot_general(a, b, (((0,), (0,)), ((), ())), preferred_element_type=F32)


def _div_pow2(x, d):
    assert d & (d - 1) == 0
    return lax.shift_right_logical(x, jnp.int32(d.bit_length() - 1))


def _mod_pow2(x, d):
    assert d & (d - 1) == 0
    return x & jnp.int32(d - 1)


def _row_tile(n, cap):
    t = min(n, cap)
    assert n % t == 0, (n, t)
    return t


FFN_CHUNK = 256
FFN_CHUNKS = D_FF // FFN_CHUNK
POST_CHUNK = 512
FFN_ROWS = 512


def _ffn_rows(x_ref, pre_refs, w_refs, post_refs, o_ref, p_ref, tail_ref, acc_ref):
    g_ref, wg_ref, wu_ref, wd_ref = w_refs
    rows = x_ref.shape[0]
    x = x_ref[...]
    if pre_refs is not None:
        a1_ref, a2_ref, w1_ref, w2_ref = pre_refs
        x = x + _dot(a1_ref[...], w1_ref[...]) + _dot(a2_ref[...], w2_ref[...])
    h = (_rms(x, D_MODEL) * g_ref[...]).astype(BF16)
    for j in range(FFN_CHUNKS):
        gate = _dot(h, wg_ref[j])
        up = _dot(h, wu_ref[j])
        act = (gate * jax.nn.sigmoid(gate) * up).astype(BF16)
        part = _dot(act, wd_ref[j])
        if j == 0:
            acc_ref[:rows] = part
        else:
            acc_ref[:rows] += part
    out = x + 0.5 * acc_ref[:rows]
    o_ref[...] = out
    if post_refs is not None:
        g2_ref, wp_ref = post_refs
        h2 = (_rms(out, D_MODEL) * g2_ref[...]).astype(BF16)
        n_post = wp_ref.shape[1] // POST_CHUNK
        for c in range(n_post):
            cs = slice(c * POST_CHUNK, (c + 1) * POST_CHUNK)
            part = _dot(h2, wp_ref[:, cs]).astype(p_ref.dtype)
            if tail_ref is not None and c == n_post - 1:
                tail_ref[...] = part
            else:
                p_ref[:, cs] = part


def _ffn_kernel(*refs, tiles, has_pre, has_post, split_tails):
    refs = list(refs)
    take = lambda k: [refs.pop(0) for _ in range(k)]
    group_in = [take(3 if has_pre else 1) for _ in tiles]
    pre_w = take(2) if has_pre else None
    g_ref, wg_in, wu_in, wd_in = take(4)
    post_refs = take(2) if has_post else None
    group_out = [take(1 + has_post + split) for split in split_tails]
    wg_s, wu_s, wd_s, acc_ref = refs
    w_refs = (g_ref, wg_s, wu_s, wd_s)
    step = pl.program_id(0)

    @pl.when(step < FFN_CHUNKS)
    def _():
        wg_s[step] = wg_in[...].astype(BF16)
        wu_s[step] = wu_in[...].astype(BF16)
        wd_s[step] = wd_in[...].astype(BF16)

    first = FFN_CHUNKS
    for n_tiles, ins, outs, split in zip(tiles, group_in, group_out, split_tails):
        pre_refs = (ins[1], ins[2], *pre_w) if has_pre else None
        p_ref = outs[1] if has_post else None
        tail_ref = outs[2] if split else None

        @pl.when((step >= first) & (step < first + n_tiles))
        def _(ins=ins, outs=outs, pre_refs=pre_refs, p_ref=p_ref, tail_ref=tail_ref):
            _ffn_rows(ins[0], pre_refs, w_refs, post_refs, outs[0], p_ref, tail_ref, acc_ref)

        first += n_tiles


def ffn(groups, g, wg, wu, wd, layer, pre_w=None, post=None, tm_cap=FFN_ROWS):
    fixed = lambda r, c: pl.BlockSpec((r, c), lambda i: (0, 0), pipeline_mode=pl.Buffered(1))
    args, specs, out_specs, out_shape, tiles, split_tails = [], [], [], [], [], []
    first = FFN_CHUNKS
    for gi, grp in enumerate(groups):
        x = grp['x']
        n = x.shape[0]
        tm = _row_tile(n, tm_cap)
        n_tiles = n // tm
        local = lambda i, first=first, n_tiles=n_tiles: jnp.clip(i - first, 0, n_tiles - 1)
        mode = {} if gi == 0 else dict(pipeline_mode=pl.Buffered(1))
        rows = lambda w, local=local, tm=tm, mode=mode: pl.BlockSpec((tm, w), lambda i: (local(i), 0), **mode)
        tmaj = grp.get('time_major')
        if tmaj is not None:
            streams, steps = tmaj
            assert steps % tm == 0 and streams * steps == n
            per_stream = steps // tm
            by_time = lambda w, local=local, tm=tm, per_stream=per_stream: pl.BlockSpec(
                (tm, w), lambda i: (local(i) % per_stream, local(i) // per_stream))
        args.append(x)
        specs.append(rows(D_MODEL))
        if pre_w is not None:
            args += [grp['a1'], grp['a2']]
            specs += [rows(grp['a1'].shape[1]),
                      by_time(pre_w[1].shape[0]) if tmaj is not None else rows(grp['a2'].shape[1])]
        out_specs.append(rows(D_MODEL))
        out_shape.append(jax.ShapeDtypeStruct((n, D_MODEL), F32))
        split = post is not None and tmaj is not None
        if post is not None:
            head_w = post[1].shape[1] - (POST_CHUNK if split else 0)
            out_specs.append(rows(head_w))
            out_shape.append(jax.ShapeDtypeStruct((n, head_w), BF16))
            if split:
                out_specs.append(by_time(POST_CHUNK))
                out_shape.append(jax.ShapeDtypeStruct((steps, streams * POST_CHUNK), BF16))
        tiles.append(n_tiles)
        split_tails.append(split)
        first += n_tiles
    if pre_w is not None:
        args += list(pre_w)
        specs += [fixed(*pre_w[0].shape), fixed(*pre_w[1].shape)]
    chunk = lambda i: jnp.minimum(i, FFN_CHUNKS - 1)
    args += [g.reshape(1, D_MODEL), wg, wu, wd]
    specs += [fixed(1, D_MODEL),
              pl.BlockSpec((None, D_MODEL, FFN_CHUNK), lambda i: (layer, 0, chunk(i))),
              pl.BlockSpec((None, D_MODEL, FFN_CHUNK), lambda i: (layer, 0, chunk(i))),
              pl.BlockSpec((None, FFN_CHUNK, D_MODEL), lambda i: (layer, chunk(i), 0))]
    if post is not None:
        assert post[1].shape[1] % POST_CHUNK == 0
        args += [post[0].reshape(1, D_MODEL), post[1]]
        specs += [fixed(1, D_MODEL), fixed(*post[1].shape)]
    kern = functools.partial(_ffn_kernel, tiles=tuple(tiles), has_pre=pre_w is not None,
                             has_post=post is not None, split_tails=tuple(split_tails))
    res = pl.pallas_call(
        kern,
        grid=(first,),
        in_specs=specs,
        out_specs=out_specs,
        out_shape=out_shape,
        scratch_shapes=[pltpu.VMEM((FFN_CHUNKS, D_MODEL, FFN_CHUNK), BF16),
                        pltpu.VMEM((FFN_CHUNKS, D_MODEL, FFN_CHUNK), BF16),
                        pltpu.VMEM((FFN_CHUNKS, FFN_CHUNK, D_MODEL), BF16),
                        pltpu.VMEM((max(min(tm_cap, grp['x'].shape[0]) for grp in groups), D_MODEL), F32)],
        compiler_params=_cparams(("arbitrary",)),
        name="ffn",
    )(*args)
    out, k = [], 0
    for split in split_tails:
        width = 1 + (post is not None) + split
        out.append(tuple(res[k:k + width]))
        k += width
    return out


def _hgrn_kernel(q_ref, f_ref, i_ref, g_ref, lb_ref, og_ref, s0_ref, y_ref, st_ref, state_ref,
                 *, tt, blk, has_past):
    for si in range(q_ref.shape[0]):
        _hgrn_stream(q_ref.at[si], f_ref.at[si], i_ref.at[si], g_ref.at[si], lb_ref, og_ref, s0_ref.at[si],
                     y_ref.at[si], st_ref.at[si], state_ref.at[si], tt=tt, blk=blk, has_past=has_past)


def _hgrn_stream(q_ref, f_ref, i_ref, g_ref, lb_ref, og_ref, s0_ref, y_ref, st_ref, state_ref,
                 *, tt, blk, has_past):
    t_idx = pl.program_id(1)
    n_chunks = tt // blk
    n_sub = blk // HGRN_SUB

    @pl.when(t_idx == 0)
    def _():
        if has_past:
            for h in range(A_HEADS):
                state_ref[h] = s0_ref[h].T
        else:
            state_ref[...] = jnp.zeros_like(state_ref)

    row = lax.broadcasted_iota(jnp.int32, (tt, tt), 0)
    col = lax.broadcasted_iota(jnp.int32, (tt, tt), 1)
    same_chunk = _div_pow2(row, blk) == _div_pow2(col, blk)
    causal = same_chunk & (col <= row)
    tril = jnp.where(causal, 1.0, 0.0).astype(BF16)
    heads = [slice(h * LANES, (h + 1) * LANES) for h in range(A_HEADS)]

    q = q_ref[...].astype(F32)
    v_bf = i_ref[...].astype(BF16)
    lb = lb_ref[...]
    f = lb + (1.0 - lb) * jax.nn.sigmoid(f_ref[...].astype(F32))
    k = 1.0 - f
    logf = jnp.log(f)
    hi = logf.astype(BF16)
    mid = (logf - hi.astype(F32)).astype(BF16)
    b = _dot(tril, hi) + _dot(tril, mid)

    q_slab, k_slab, ends = [], [], []
    for c in range(n_chunks):
        rows = [slice(c * blk + i * HGRN_SUB, c * blk + (i + 1) * HGRN_SUB) for i in range(n_sub)]
        mids = [b[r.start + HGRN_SUB // 2 - 1:r.start + HGRN_SUB // 2, :] for r in rows]
        q_slab.append([(q[r] * jnp.exp(jnp.minimum(b[r] - mids[i], EXP_CLAMP))).astype(BF16)
                       for i, r in enumerate(rows)])
        k_slab.append([[None if j > i else
                        (k[r] * jnp.exp(jnp.minimum(mids[i] - b[r], EXP_CLAMP) if j == i
                                        else mids[i] - b[r])).astype(BF16)
                        for i in range(n_sub)] for j, r in enumerate(rows)])
        ends.append(b[(c + 1) * blk - 1:(c + 1) * blk, :])
    b_end_rows = (jnp.broadcast_to(ends[0], (blk, A_WIDTH)) if n_chunks == 1 else
                  jnp.concatenate([jnp.broadcast_to(e, (blk, A_WIDTH)) for e in ends], axis=0))
    q_e = (q * jnp.exp(b)).astype(BF16)
    k_d = (k * jnp.exp(b_end_rows - b)).astype(BF16)
    zero_slab = jnp.zeros((HGRN_SUB, LANES), BF16)
    zero_blk = jnp.zeros((blk, LANES), BF16)
    chunk_rows = [slice(c * blk, (c + 1) * blk) for c in range(n_chunks)]

    outs = []
    for h, sl in enumerate(heads):
        q_hat = jnp.concatenate(
            [jnp.concatenate([q_slab[c][j][:, sl] if i == j else zero_slab for i in range(n_sub)], axis=1)
             for c in range(n_chunks) for j in range(n_sub)], axis=0)
        k_hat = jnp.concatenate(
            [jnp.concatenate([k_slab[c][j][i][:, sl] if j <= i else zero_slab for i in range(n_sub)], axis=1)
             for c in range(n_chunks) for j in range(n_sub)], axis=0)
        att = _dot_nt(q_hat, k_hat)
        att = jnp.where(causal, att, 0.0).astype(BF16)
        v_spread = jnp.concatenate(
            [jnp.concatenate([v_bf[rs, sl] if c == d else zero_blk for d in range(n_chunks)], axis=1)
             for c, rs in enumerate(chunk_rows)], axis=0)
        kv = _dot_tn(v_spread, k_d[:, sl])
        s_t = state_ref[h]
        states = []
        for c in range(n_chunks):
            states.append(s_t.astype(BF16))
            s_t = s_t * jnp.exp(ends[c][:, sl]) + kv[c * A_VAL:(c + 1) * A_VAL]
        state_ref[h] = s_t
        o_inter = jnp.concatenate([_dot_nt(q_e[rs, sl], st) for rs, st in zip(chunk_rows, states)], axis=0)
        o = _dot(att, v_bf[:, sl]) + o_inter
        outs.append(_rms(o, A_VAL) * og_ref[...])
    y = jnp.concatenate(outs, axis=1) * jax.nn.sigmoid(g_ref[...].astype(F32))
    y_ref[...] = y.astype(y_ref.dtype)

    @pl.when(t_idx == pl.num_programs(1) - 1)
    def _():
        for h in range(A_HEADS):
            st_ref[h] = state_ref[h].T


def hgrn2(proj, lb, out_g, s0, tt_cap=256):
    bsz, t, _ = proj.shape
    blk = min(HGRN_BLOCK, t)
    tt = _row_tile(t, tt_cap)
    nb = _row_tile(bsz, max(1, HGRN_STEP_ROWS // tt))
    has_past = s0 is not None
    if s0 is None:
        s0 = jnp.zeros((bsz, A_HEADS, A_KEY, A_VAL), F32)
    seg = lambda s: pl.BlockSpec((nb, tt, A_WIDTH), lambda b, i, s=s: (b, i, s))
    kern = functools.partial(_hgrn_kernel, tt=tt, blk=blk, has_past=has_past)
    return pl.pallas_call(
        kern,
        grid=(bsz // nb, t // tt),
        in_specs=[
            seg(0), seg(1), seg(2), seg(3),
            pl.BlockSpec((1, A_QK), lambda b, i: (0, 0)),
            pl.BlockSpec((1, A_VAL), lambda b, i: (0, 0)),
            pl.BlockSpec((nb, A_HEADS, A_KEY, A_VAL), lambda b, i: (b, 0, 0, 0)),
        ],
        out_specs=[
            pl.BlockSpec((nb, tt, A_WIDTH), lambda b, i: (b, i, 0)),
            pl.BlockSpec((nb, A_HEADS, A_KEY, A_VAL), lambda b, i: (b, 0, 0, 0)),
        ],
        out_shape=[
            jax.ShapeDtypeStruct((bsz, t, A_WIDTH), BF16),
            jax.ShapeDtypeStruct((bsz, A_HEADS, A_KEY, A_VAL), F32),
        ],
        scratch_shapes=[pltpu.VMEM((nb, A_HEADS, A_VAL, A_KEY), F32)],
        compiler_params=_cparams(("parallel", "arbitrary")),
        name="hgrn2",
    )(proj, proj, proj, proj, lb.reshape(1, A_QK), out_g.reshape(1, A_VAL), s0)


S5_SUPER = 4
S5_SUPER_U = B_WIDTH // S5_SUPER
S5_SUPER_X = 2 * (B_GROUPS // S5_SUPER) * B_STATE
S5_HALF = S5_SUPER_X // 2
S5_STATE_W = S5_SUPER * S5_SUPER_X
S5_SCAN_GROUP = 2


def _s5_kernel(u_ref, wb_ref, wc_ref, lam_ref, d_ref, wglu_ref, bglu_ref, x0_ref,
               o_ref, xt_ref, xbuf_ref, state_ref, *, tt):
    t_idx = pl.program_id(1)
    rows = tt * SUBLANES

    @pl.when(t_idx == 0)
    def _():
        state_ref[...] = x0_ref[...]

    u_bf = u_ref[...].reshape(rows, B_WIDTH)
    u = u_bf.astype(F32)
    for j in range(S5_SUPER):
        xbuf_ref[:, j * S5_SUPER_X:(j + 1) * S5_SUPER_X] = _dot(
            u_bf[:, j * S5_SUPER_U:(j + 1) * S5_SUPER_U], wb_ref[j])

    for j0 in range(0, S5_SUPER, S5_SCAN_GROUP):
        cols, lams = [], []
        for j in range(j0, j0 + S5_SCAN_GROUP):
            re_cols = slice(j * S5_SUPER_X, j * S5_SUPER_X + S5_HALF)
            im_cols = slice(j * S5_SUPER_X + S5_HALF, (j + 1) * S5_SUPER_X)
            cols.append((re_cols, im_cols))
            lams.append((jnp.broadcast_to(lam_ref[0:1, re_cols], (SUBLANES, S5_HALF)),
                         jnp.broadcast_to(lam_ref[1:2, re_cols], (SUBLANES, S5_HALF))))

        def step(t, carry, cols=cols, lams=lams):
            rows_t = pl.ds(pl.multiple_of(t * SUBLANES, SUBLANES), SUBLANES)
            out = []
            for (re_cols, im_cols), (lam_re, lam_im), (x_re, x_im) in zip(cols, lams, carry):
                n_re = lam_re * x_re - lam_im * x_im + xbuf_ref[rows_t, re_cols]
                n_im = lam_re * x_im + lam_im * x_re + xbuf_ref[rows_t, im_cols]
                xbuf_ref[rows_t, re_cols] = n_re
                xbuf_ref[rows_t, im_cols] = n_im
                out.append((n_re, n_im))
            return tuple(out)

        init = tuple((state_ref[:, rc], state_ref[:, ic]) for rc, ic in cols)
        last = lax.fori_loop(0, tt, step, init, unroll=True)
        for (rc, ic), (x_re, x_im) in zip(cols, last):
            state_ref[:, rc] = x_re
            state_ref[:, ic] = x_im

    ys = []
    for j in range(S5_SUPER):
        xs = xbuf_ref[:, j * S5_SUPER_X:(j + 1) * S5_SUPER_X].astype(BF16)
        ys.append(_dot(xs, wc_ref[j]))
    y = jnp.concatenate(ys, axis=1) + d_ref[...] * u
    z = jax.nn.gelu(y)
    gate = jax.nn.sigmoid(_dot(z.astype(BF16), wglu_ref[...]) + bglu_ref[...])
    o_ref[...] = (z * gate).astype(o_ref.dtype).reshape(tt, SUBLANES, B_WIDTH)

    @pl.when(t_idx == pl.num_programs(1) - 1)
    def _():
        xt_ref[...] = state_ref[...]


def s5_mix(u_tb, wb, wc, lam, d, w_glu, b_glu, x0, tt_cap=64):
    t, bsz, _ = u_tb.shape
    tt = _row_tile(t, tt_cap)
    kern = functools.partial(_s5_kernel, tt=tt)
    return pl.pallas_call(
        kern,
        grid=(bsz // SUBLANES, t // tt),
        in_specs=[
            pl.BlockSpec((tt, SUBLANES, B_WIDTH), lambda b, i: (i, b, 0)),
            pl.BlockSpec((S5_SUPER, S5_SUPER_U, S5_SUPER_X), lambda b, i: (0, 0, 0)),
            pl.BlockSpec((S5_SUPER, S5_SUPER_X, S5_SUPER_U), lambda b, i: (0, 0, 0)),
            pl.BlockSpec((2, S5_STATE_W), lambda b, i: (0, 0)),
            pl.BlockSpec((1, B_WIDTH), lambda b, i: (0, 0)),
            pl.BlockSpec((B_WIDTH, B_WIDTH), lambda b, i: (0, 0)),
            pl.BlockSpec((1, B_WIDTH), lambda b, i: (0, 0)),
            pl.BlockSpec((SUBLANES, S5_STATE_W), lambda b, i: (b, 0)),
        ],
        out_specs=[
            pl.BlockSpec((tt, SUBLANES, B_WIDTH), lambda b, i: (i, b, 0)),
            pl.BlockSpec((SUBLANES, S5_STATE_W), lambda b, i: (b, 0)),
        ],
        out_shape=[
            jax.ShapeDtypeStruct((t, bsz, B_WIDTH), BF16),
            jax.ShapeDtypeStruct((bsz, S5_STATE_W), F32),
        ],
        scratch_shapes=[pltpu.VMEM((tt * SUBLANES, S5_STATE_W), F32),
                        pltpu.VMEM((SUBLANES, S5_STATE_W), F32)],
        compiler_params=_cparams(("parallel", "arbitrary")),
        name="s5_mix",
    )(u_tb, wb, wc, lam, d.reshape(1, B_WIDTH), w_glu, b_glu.reshape(1, B_WIDTH), x0)


def _s5_pack_cols(a):
    return a.reshape(a.shape[:-2] + (S5_SUPER, S5_HALF))


def s5_prepare(lam_re, lam_im, log_dt, b_re, b_im, c_re, c_im):
    lr = jnp.minimum(lam_re, LAMBDA_RE_CEIL)
    li = lam_im
    dt = jnp.exp(log_dt)[:, None]
    mag = jnp.exp(lr * dt)
    bar_re = mag * jnp.cos(li * dt)
    bar_im = mag * jnp.sin(li * dt)
    den = lr * lr + li * li
    coef_re = (((bar_re - 1.0) * lr + bar_im * li) / den)[:, :, None]
    coef_im = ((bar_im * lr - (bar_re - 1.0) * li) / den)[:, :, None]
    bb_re = coef_re * b_re - coef_im * b_im
    bb_im = coef_re * b_im + coef_im * b_re
    gps = B_GROUPS // S5_SUPER
    eye = jnp.eye(gps, dtype=F32)

    def in_block(bm):
        bm = bm.reshape(S5_SUPER, gps, B_STATE, B_GROUP_DIM)
        blk = jnp.einsum('jgph,gk->jghkp', bm, eye)
        return blk.reshape(S5_SUPER, gps * B_GROUP_DIM, gps * B_STATE)

    def out_block(cm):
        cm = cm.reshape(S5_SUPER, gps, B_GROUP_DIM, B_STATE)
        blk = jnp.einsum('jghp,gk->jkpgh', cm, eye)
        return blk.reshape(S5_SUPER, gps * B_STATE, gps * B_GROUP_DIM)

    wb = jnp.concatenate([in_block(bb_re), in_block(bb_im)], axis=2).astype(BF16)
    wc = jnp.concatenate([out_block(c_re), out_block(-c_im)], axis=1).astype(BF16)
    lam_rows = jnp.stack([
        jnp.concatenate([_s5_pack_cols(bar_re)] * 2, axis=-1).reshape(S5_STATE_W),
        jnp.concatenate([_s5_pack_cols(bar_im)] * 2, axis=-1).reshape(S5_STATE_W)])
    return wb, wc, lam_rows


def s5_pack_state(x_re, x_im):
    bsz = x_re.shape[0]
    return jnp.concatenate([_s5_pack_cols(x_re), _s5_pack_cols(x_im)], axis=-1).reshape(bsz, S5_STATE_W)


def s5_unpack_state(x):
    bsz = x.shape[0]
    x = x.reshape(bsz, S5_SUPER, 2, S5_HALF)
    return (x[:, :, 0].reshape(bsz, B_GROUPS, B_STATE), x[:, :, 1].reshape(bsz, B_GROUPS, B_STATE))


def _gmlp_kernel(u_ref, v_ref, gn_ref, ws_ref, bs_ref, o_ref, vr_ref, *, tt, chunk, all_rows):
    n_chunks = tt // chunk
    u = jax.nn.gelu(u_ref[...].astype(F32))
    v = jax.nn.gelu(v_ref[...].astype(F32))
    vn = _rms(v, C_WIDTH) * gn_ref[...]
    vn_bf = vn.astype(BF16)
    row = lax.broadcasted_iota(jnp.int32, (chunk, chunk), 0)
    col = lax.broadcasted_iota(jnp.int32, (chunk, chunk), 1)
    keep = col <= row
    for g in range(C_GROUPS):
        w = jnp.where(keep, ws_ref[g], 0.0).astype(BF16)
        bias = bs_ref[:, g:g + 1]
        cs = slice(g * C_GROUP_DIM, (g + 1) * C_GROUP_DIM)
        for c in range(n_chunks):
            rs = slice(c * chunk, (c + 1) * chunk)
            s = _dot(w, vn_bf[rs, cs]) + bias
            o_ref[rs, cs] = (u[rs, cs] * s).astype(o_ref.dtype)

    if all_rows:
        vr_ref[...] = vn
    else:
        @pl.when(pl.program_id(1) == pl.num_programs(1) - 1)
        def _():
            vr_ref[...] = vn[tt - chunk:, :]


def gmlp(proj, v_norm, w_s, b_s, tt_cap=1024):
    bsz, t, width = proj.shape
    chunk = min(t, C_CHUNK)
    all_rows = t == chunk
    if all_rows:
        c_out, v_rows = gmlp_call(proj.reshape(1, bsz * t, width), v_norm, w_s, b_s, chunk, True, tt_cap)
        return c_out.reshape(bsz, t, C_WIDTH), v_rows.reshape(bsz, t, C_WIDTH)
    return gmlp_call(proj, v_norm, w_s, b_s, chunk, False, tt_cap)


def gmlp_call(proj, v_norm, w_s, b_s, chunk, all_rows, tt_cap):
    bsz, t, _ = proj.shape
    tt = _row_tile(t, tt_cap)
    vr_rows = tt if all_rows else chunk
    ws = w_s[:, :chunk, :chunk]
    bs_t = b_s[:, :chunk].T
    kern = functools.partial(_gmlp_kernel, tt=tt, chunk=chunk, all_rows=all_rows)
    return pl.pallas_call(
        kern,
        grid=(bsz, t // tt),
        in_specs=[
            pl.BlockSpec((None, tt, C_WIDTH), lambda b, i: (b, i, 0)),
            pl.BlockSpec((None, tt, C_WIDTH), lambda b, i: (b, i, 1)),
            pl.BlockSpec((1, C_WIDTH), lambda b, i: (0, 0)),
            pl.BlockSpec((C_GROUPS, chunk, chunk), lambda b, i: (0, 0, 0)),
            pl.BlockSpec((chunk, C_GROUPS), lambda b, i: (0, 0)),
        ],
        out_specs=[
            pl.BlockSpec((None, tt, C_WIDTH), lambda b, i: (b, i, 0)),
            pl.BlockSpec((None, vr_rows, C_WIDTH), (lambda b, i: (b, i, 0)) if all_rows else (lambda b, i: (b, 0, 0))),
        ],
        out_shape=[
            jax.ShapeDtypeStruct((bsz, t, C_WIDTH), BF16),
            jax.ShapeDtypeStruct((bsz, t if all_rows else chunk, C_WIDTH), F32),
        ],
        compiler_params=_cparams(("parallel", "arbitrary")),
        name="gmlp",
    )(proj, proj, v_norm.reshape(1, C_WIDTH), ws, bs_t)


def _rope_lanes(x, cos_t, sin_lo, sin_hi):
    half = ROPE_DIM // 2
    return (x * cos_t + pltpu.roll(x, LANES - half, axis=1) * sin_lo
            + pltpu.roll(x, half, axis=1) * sin_hi)


def _mla_q_kernel(cq_ref, ckv_ref, kpe_ref, qn_ref, kvn_ref, wuq_ref, qg_ref,
                  qc_ref, qs_ref, kc_ref, ks1_ref, ks2_ref, *rest, with_kv):
    if with_kv:
        wk_ref, wvt_ref, kg_ref, q_ref, ckv_o_ref, kpe_o_ref, k_ref, vt_ref = rest
    else:
        q_ref, ckv_o_ref, kpe_o_ref = rest
    cqn = (_rms(cq_ref[...].astype(F32), Q_LORA) * qn_ref[...]).astype(BF16)
    width = D_HEADS * HEAD_PAD
    q_all = _dot(cqn, wuq_ref[...])
    qc, qs = qc_ref[...], qs_ref[...]
    scale = QK_DIM ** -0.5 * math.log2(math.e)
    for h in range(D_HEADS):
        sl = slice(h * HEAD_PAD, (h + 1) * HEAD_PAD)
        rot = slice(width + h * HEAD_PAD, width + (h + 1) * HEAD_PAD)
        qh = q_all[:, sl] * qc + q_all[:, rot] * qs
        qh = _rms(qh, QK_DIM) * (qg_ref[...] * scale)
        q_ref[:, sl] = qh.astype(q_ref.dtype)
    ckv = _rms(ckv_ref[...].astype(F32), KV_LORA) * kvn_ref[...]
    kpe = _rope_lanes(kpe_ref[...].astype(F32), kc_ref[...], ks1_ref[...], ks2_ref[...])
    ckv_o_ref[...] = ckv
    kpe_o_ref[...] = kpe[:, :ROPE_DIM]
    if with_kv:
        ckv_bf = ckv.astype(BF16)
        ckpe = jnp.concatenate([ckv_bf, kpe.astype(BF16)], axis=1)
        k_all = _dot(ckpe, wk_ref[...])
        for h in range(D_HEADS):
            sl = slice(h * HEAD_PAD, (h + 1) * HEAD_PAD)
            k_ref[:, sl] = (_rms(k_all[:, sl], QK_DIM) * kg_ref[...]).astype(k_ref.dtype)
        row = lax.broadcasted_iota(jnp.int32, (D_HEADS * HEAD_PAD, 1), 0)
        ones_row = jnp.where(_mod_pow2(row, HEAD_PAD) == V_ONE_ROW, 1.0, 0.0)
        for blk in range(vt_ref.shape[0]):
            rows = slice(blk * ATTN_TK, (blk + 1) * ATTN_TK)
            vt_ref[blk] = (_dot_nt(wvt_ref[...], ckv_bf[rows]) + ones_row).astype(vt_ref.dtype)


MLA_Q_ROWS = 1024


def mla_q(proj, tables, q_norm, kv_norm, wuq_p, q_gain_p, kv_weights=None, tm_cap=MLA_Q_ROWS):
    bsz, t, _ = proj.shape
    tm = _row_tile(t, tm_cap)
    width = D_HEADS * HEAD_PAD
    tab = pl.BlockSpec((tm, LANES), lambda b, i: (i, 0))
    row = lambda w: pl.BlockSpec((1, w), lambda b, i: (0, 0))
    full = lambda r, c: pl.BlockSpec((r, c), lambda b, i: (0, 0))
    with_kv = kv_weights is not None
    args = [proj, proj, proj, q_norm.reshape(1, Q_LORA), kv_norm.reshape(1, KV_LORA), wuq_p, q_gain_p, *tables]
    in_specs = [
        pl.BlockSpec((None, tm, Q_LORA), lambda b, i: (b, i, 2 * C_WIDTH // Q_LORA)),
        pl.BlockSpec((None, tm, KV_LORA), lambda b, i: (b, i, (2 * C_WIDTH + Q_LORA) // KV_LORA)),
        pl.BlockSpec((None, tm, LANES), lambda b, i: (b, i, (2 * C_WIDTH + Q_LORA + KV_LORA) // LANES)),
        row(Q_LORA), row(KV_LORA), full(Q_LORA, 2 * width), row(HEAD_PAD),
        tab, tab, tab, tab, tab,
    ]
    out_specs = [
        pl.BlockSpec((None, tm, width), lambda b, i: (b, i, 0)),
        pl.BlockSpec((None, tm, KV_LORA), lambda b, i: (b, i, 0)),
        pl.BlockSpec((None, tm, ROPE_DIM), lambda b, i: (b, i, 0)),
    ]
    out_shape = [
        jax.ShapeDtypeStruct((bsz, t, width), BF16),
        jax.ShapeDtypeStruct((bsz, t, KV_LORA), F32),
        jax.ShapeDtypeStruct((bsz, t, ROPE_DIM), F32),
    ]
    if with_kv:
        wk_full, wv_p, k_gain_p = kv_weights
        assert tm % ATTN_TK == 0
        args += [wk_full, wv_p.T, k_gain_p]
        in_specs += [full(*wk_full.shape), full(width, KV_LORA), row(HEAD_PAD)]
        out_specs += [pl.BlockSpec((None, tm, width), lambda b, i: (b, i, 0)),
                      pl.BlockSpec((None, tm // ATTN_TK, width, ATTN_TK), lambda b, i: (b, i, 0, 0))]
        out_shape += [jax.ShapeDtypeStruct((bsz, t, width), BF16),
                      jax.ShapeDtypeStruct((bsz, t // ATTN_TK, width, ATTN_TK), BF16)]
    return pl.pallas_call(
        functools.partial(_mla_q_kernel, with_kv=with_kv),
        grid=(bsz, t // tm),
        in_specs=in_specs,
        out_specs=out_specs,
        out_shape=out_shape,
        compiler_params=_cparams(("parallel", "parallel")),
        name="mla_q",
    )(*args)


def _for_blocks(lo, hi, block_fn):
    n_pairs = (hi - lo) // 2

    def pair(i, carry):
        block_fn(lo + 2 * i)
        block_fn(lo + 2 * i + 1)
        return carry

    lax.fori_loop(0, n_pairs, pair, 0)

    @pl.when(lo + 2 * n_pairs < hi)
    def _():
        block_fn(hi - 1)


def _attn_kernel(q_ref, k_ref, vt_ref, o_ref, sc_ref, mx_ref, acc_ref, *, tq, tk, s_len, q_pos0, diag_split):
    qi = pl.program_id(1)
    q_first = q_pos0 + qi * tq
    q_last = q_first + tq - 1
    vis_first = jnp.minimum(s_len, (q_first // CHUNK + 1) * CHUNK)
    vis_last = jnp.minimum(s_len, (q_last // CHUNK + 1) * CHUNK)
    n_full = vis_first // tk
    n_kv = (vis_last + tk - 1) // tk

    q_chunk = _div_pow2(q_first + lax.broadcasted_iota(jnp.int32, (tk, tq), 1), CHUNK)
    k_iota = lax.broadcasted_iota(jnp.int32, (tk, tq), 0)

    half = tk // 2
    q_chunk_lo = _div_pow2(q_first + lax.broadcasted_iota(jnp.int32, (half, tq), 1), CHUNK)
    q_chunk_hi = _div_pow2(q_first + half + lax.broadcasted_iota(jnp.int32, (half, tq - half), 1), CHUNK)
    k_iota_lo = lax.broadcasted_iota(jnp.int32, (half, tq), 0)
    k_iota_hi = lax.broadcasted_iota(jnp.int32, (half, tq - half), 0)

    for g0 in range(0, D_HEADS, ATTN_HEAD_GROUP):
        heads = [slice(h * HEAD_PAD, (h + 1) * HEAD_PAD) for h in range(g0, g0 + ATTN_HEAD_GROUP)]

        def track_max(h, s, cols=slice(None)):
            blk_max = jnp.max(s.reshape(s.shape[0] // SUBLANES, SUBLANES, s.shape[1]), axis=0)
            mx_ref[h, :, cols] = jnp.maximum(mx_ref[h, :, cols], blk_max)

        def score_block(j, masked, heads=heads):
            k0 = pl.multiple_of(j * tk, tk)
            visible = (_div_pow2(k0 + k_iota, CHUNK) <= q_chunk) if masked else None
            for h, sl in enumerate(heads):
                s = _dot_nt(k_ref[pl.ds(k0, tk), sl], q_ref[:, sl])
                if masked:
                    s = jnp.where(visible, s, NEG_BIG)
                sc_ref[h, j] = s
                track_max(h, s)

        def sum_block(j, heads=heads):
            for h, sl in enumerate(heads):
                p = jnp.exp2(sc_ref[h, j] - mx_ref[h][0:1, :])
                acc_ref[h] += _dot(vt_ref[j, sl, :], p.astype(BF16))

        def score_diag(j, heads=heads):
            k0 = pl.multiple_of(j * tk, tk)
            k1 = pl.multiple_of(k0 + half, half)
            vis0 = _div_pow2(k0 + k_iota_lo, CHUNK) <= q_chunk_lo
            vis1 = _div_pow2(k1 + k_iota_hi, CHUNK) <= q_chunk_hi
            for h, sl in enumerate(heads):
                s0 = jnp.where(vis0, _dot_nt(k_ref[pl.ds(k0, half), sl], q_ref[:, sl]), NEG_BIG)
                s1 = jnp.where(vis1, _dot_nt(k_ref[pl.ds(k1, half), sl], q_ref[half:, sl]), NEG_BIG)
                sc_ref[h, j, :half, :] = s0
                sc_ref[h, j, half:, half:] = s1
                track_max(h, s0)
                track_max(h, s1, slice(half, None))

        def sum_diag(j, heads=heads):
            for h, sl in enumerate(heads):
                m = mx_ref[h][0:1, :]
                p0 = jnp.exp2(sc_ref[h, j, :half, :] - m)
                p1 = jnp.exp2(sc_ref[h, j, half:, half:] - m[:, half:])
                acc_ref[h] += _dot(vt_ref[j, sl, :half], p0.astype(BF16))
                acc_ref[h, :, half:] += _dot(vt_ref[j, sl, half:], p1.astype(BF16))

        mx_ref[...] = jnp.full_like(mx_ref, NEG_BIG)
        _for_blocks(0, n_full, functools.partial(score_block, masked=False))
        if diag_split:
            score_diag(n_full)
        else:
            _for_blocks(n_full, n_kv, functools.partial(score_block, masked=True))
        for h in range(ATTN_HEAD_GROUP):
            m = jnp.max(mx_ref[h], axis=0, keepdims=True)
            mx_ref[h] = jnp.broadcast_to(m, (SUBLANES, tq))
        acc_ref[...] = jnp.zeros_like(acc_ref)
        if diag_split:
            _for_blocks(0, n_full, sum_block)
            sum_diag(n_full)
        else:
            _for_blocks(0, n_kv, sum_block)
        for hp in range(0, ATTN_HEAD_GROUP, 2):
            pair = [acc_ref[h][:V_DIM, :] / acc_ref[h][V_ONE_ROW:V_ONE_ROW + 1, :] for h in (hp, hp + 1)]
            lanes = slice((g0 + hp) * V_DIM, (g0 + hp + 2) * V_DIM)
            o_ref[:, lanes] = jnp.concatenate(pair, axis=0).T.astype(o_ref.dtype)


def attention(q, k, vt, q_pos0, tq_cap=512):
    bsz, tq_len, width = q.shape
    s_len = k.shape[1]
    n_blocks, _, tk = vt.shape[1:]
    assert n_blocks * tk == s_len and tk % LANES == 0
    tq = _row_tile(tq_len, tq_cap)
    diag_split = tq == tk and q_pos0 % tq == 0 and s_len >= q_pos0 + tq_len
    kern = functools.partial(_attn_kernel, tq=tq, tk=tk, s_len=s_len, q_pos0=q_pos0, diag_split=diag_split)
    return pl.pallas_call(
        kern,
        grid=(bsz, tq_len // tq),
        in_specs=[
            pl.BlockSpec((None, tq, width), lambda b, i: (b, i, 0)),
            pl.BlockSpec((None, s_len, width), lambda b, i: (b, 0, 0)),
            pl.BlockSpec((None, n_blocks, width, tk), lambda b, i: (b, 0, 0, 0)),
        ],
        out_specs=pl.BlockSpec((None, tq, D_HEADS * V_DIM), lambda b, i: (b, i, 0)),
        out_shape=jax.ShapeDtypeStruct((bsz, tq_len, D_HEADS * V_DIM), BF16),
        scratch_shapes=[pltpu.VMEM((ATTN_HEAD_GROUP, n_blocks, tk, tq), F32),
                        pltpu.VMEM((ATTN_HEAD_GROUP, SUBLANES, tq), F32),
                        pltpu.VMEM((ATTN_HEAD_GROUP, HEAD_PAD, tq), F32)],
        compiler_params=_cparams(("parallel", "arbitrary")),
        name="attention",
    )(q, k, vt)


def _mla_decode_kernel(q_ref, ckv_c_ref, kpe_c_ref, ckv_n_ref, kpe_n_ref, wk_ref, wkt_ref, wv_ref, kg_ref,
                       o_ref, *, tq, s_len, q_pos0):
    s_pad = ckv_c_ref.shape[0] + ckv_n_ref.shape[0]
    heads = [slice(h * HEAD_PAD, (h + 1) * HEAD_PAD) for h in range(D_HEADS)]
    ckv = jnp.concatenate([ckv_c_ref[...], ckv_n_ref[...]], axis=0).astype(BF16)
    kpe_raw = jnp.concatenate([kpe_c_ref[...], kpe_n_ref[...]], axis=0).astype(BF16)
    lane = lax.broadcasted_iota(jnp.int32, (ROPE_DIM, LANES), 1)
    src = lax.broadcasted_iota(jnp.int32, (ROPE_DIM, LANES), 0)
    place = jnp.where(lane == src + NOPE_DIM, 1.0, 0.0).astype(BF16)
    kpe = _dot(kpe_raw, place).astype(BF16)
    eye = (lax.broadcasted_iota(jnp.int32, (LANES, LANES), 0)
           == lax.broadcasted_iota(jnp.int32, (LANES, LANES), 1))
    kpe_t = _dot_nt(jnp.where(eye, 1.0, 0.0).astype(BF16), kpe)
    pe_ss = jnp.sum(kpe_t * kpe_t, axis=0, keepdims=True)
    kn_t = _dot_nt(wkt_ref[...], ckv)
    inv_rms, q_lat, q_g = [], [], []
    for h, sl in enumerate(heads):
        blk = kn_t[h * NOPE_DIM:(h + 1) * NOPE_DIM, :]
        ss = jnp.sum(blk * blk, axis=0, keepdims=True) + pe_ss
        inv_rms.append(jnp.broadcast_to(lax.rsqrt(ss * (1.0 / QK_DIM) + EPS), (tq, s_pad)))
        qg = (q_ref[:, sl].astype(F32) * kg_ref[...]).astype(BF16)
        q_g.append(qg)
        q_lat.append(_dot_nt(qg, wk_ref[:, sl]).astype(BF16))
    s = _dot_nt(jnp.concatenate(q_lat, axis=0), ckv) + _dot_nt(jnp.concatenate(q_g, axis=0), kpe)
    s = s * jnp.concatenate(inv_rms, axis=0)
    row = lax.broadcasted_iota(jnp.int32, s.shape, 0)
    col = lax.broadcasted_iota(jnp.int32, s.shape, 1)
    q_chunk = _div_pow2(q_pos0 + _mod_pow2(row, tq), CHUNK)
    visible = (_div_pow2(col, CHUNK) <= q_chunk) & (col < s_len)
    s = jnp.where(visible, s, NEG_BIG)
    p = jnp.exp2(s - jnp.max(s, axis=-1, keepdims=True))
    lat = _dot(p.astype(BF16), ckv) / jnp.sum(p, axis=-1, keepdims=True)
    lat = lat.astype(BF16)
    for hp in range(D_HEADS // 2):
        both = jnp.concatenate([lat[2 * hp * tq:(2 * hp + 1) * tq], lat[(2 * hp + 1) * tq:(2 * hp + 2) * tq]], axis=1)
        o_ref[:, hp * LANES:(hp + 1) * LANES] = _dot(both, wv_ref[hp]).astype(o_ref.dtype)


def mla_decode(q, cache_ckv, cache_kpe, ckv_new, kpe_new, wk_p, wv_pairs, k_gain_p, q_pos0):
    bsz, tq, width = q.shape
    n_cache, n_new = cache_ckv.shape[1], ckv_new.shape[1]
    assert n_cache % LANES == 0 and tq & (tq - 1) == 0
    s_len = n_cache + n_new
    new_pad = -(-n_new // LANES) * LANES
    ckv_new = jnp.pad(ckv_new, ((0, 0), (0, new_pad - n_new), (0, 0)))
    kpe_new = jnp.pad(kpe_new, ((0, 0), (0, new_pad - n_new), (0, 0)))
    wk_rows = wk_p.T.reshape(D_HEADS, HEAD_PAD, KV_LORA)[:, :NOPE_DIM].reshape(D_HEADS * NOPE_DIM, KV_LORA)
    full = lambda r, c: pl.BlockSpec((r, c), lambda b: (0, 0))
    per_b = lambda r, c: pl.BlockSpec((None, r, c), lambda b: (b, 0, 0))
    kern = functools.partial(_mla_decode_kernel, tq=tq, s_len=s_len, q_pos0=q_pos0)
    return pl.pallas_call(
        kern,
        grid=(bsz,),
        in_specs=[
            per_b(tq, width), per_b(n_cache, KV_LORA), per_b(n_cache, ROPE_DIM),
            per_b(new_pad, KV_LORA), per_b(new_pad, ROPE_DIM),
            full(KV_LORA, width), full(D_HEADS * NOPE_DIM, KV_LORA),
            pl.BlockSpec((D_HEADS // 2, 2 * KV_LORA, LANES), lambda b: (0, 0, 0)), full(1, HEAD_PAD),
        ],
        out_specs=per_b(tq, D_HEADS * V_DIM),
        out_shape=jax.ShapeDtypeStruct((bsz, tq, D_HEADS * V_DIM), BF16),
        compiler_params=_cparams(("parallel",)),
        name="mla_decode",
    )(q, cache_ckv, cache_kpe, ckv_new, kpe_new, wk_p, wk_rows, wv_pairs, k_gain_p)


def rope_tables(pos):
    half = ROPE_DIM // 2
    inv = ROPE_BASE ** (-jnp.arange(half, dtype=F32) / half)
    ang = pos.astype(F32)[:, None] * inv[None, :]
    cos, sin = jnp.cos(ang), jnp.sin(ang)
    n = pos.shape[0]
    z = lambda w: jnp.zeros((n, w), F32)
    o = lambda w: jnp.ones((n, w), F32)
    tail = LANES - NOPE_DIM - ROPE_DIM
    q_cos = jnp.concatenate([o(NOPE_DIM), cos, cos, z(tail)], axis=1)
    q_sin = jnp.concatenate([z(NOPE_DIM), sin, sin, z(tail)], axis=1)
    k_cos = jnp.concatenate([cos, cos, z(LANES - ROPE_DIM)], axis=1)
    k_s1 = jnp.concatenate([-sin, z(LANES - half)], axis=1)
    k_s2 = jnp.concatenate([z(half), sin, z(LANES - ROPE_DIM)], axis=1)
    return q_cos, q_sin, k_cos, k_s1, k_s2


def pack_mla_weights(w_uq, w_ukv, q_gain, k_gain, cd_w_out):
    pad_q = HEAD_PAD - QK_DIM
    half = ROPE_DIM // 2
    wq = w_uq.reshape(Q_LORA, D_HEADS, QK_DIM)
    wuq_p = jnp.pad(wq, ((0, 0), (0, 0), (0, pad_q))).reshape(Q_LORA, D_HEADS * HEAD_PAD)
    wq_rot = jnp.concatenate([jnp.zeros_like(wq[:, :, :NOPE_DIM]), -wq[:, :, NOPE_DIM + half:],
                              wq[:, :, NOPE_DIM:NOPE_DIM + half], jnp.zeros_like(wq[:, :, :pad_q])], axis=2)
    wuq_p = jnp.concatenate([wuq_p, wq_rot.reshape(Q_LORA, D_HEADS * HEAD_PAD)], axis=1).astype(BF16)
    wkv = w_ukv.reshape(KV_LORA, D_HEADS, NOPE_DIM + V_DIM)
    wk_p = jnp.pad(wkv[:, :, :NOPE_DIM], ((0, 0), (0, 0), (0, HEAD_PAD - NOPE_DIM)))
    wk_p = wk_p.reshape(KV_LORA, D_HEADS * HEAD_PAD).astype(BF16)
    wv_p = jnp.pad(wkv[:, :, NOPE_DIM:], ((0, 0), (0, 0), (0, HEAD_PAD - V_DIM)))
    wv_p = wv_p.reshape(KV_LORA, D_HEADS * HEAD_PAD).astype(BF16)
    place = jnp.pad(jnp.eye(ROPE_DIM, dtype=F32), ((0, LANES - ROPE_DIM), (NOPE_DIM, HEAD_PAD - QK_DIM)))
    wk_full = jnp.concatenate([wk_p, jnp.tile(place, (1, D_HEADS)).astype(BF16)], axis=0)
    q_gain_p = jnp.pad(q_gain, (0, pad_q)).reshape(1, HEAD_PAD)
    k_gain_p = jnp.pad(k_gain, (0, pad_q)).reshape(1, HEAD_PAD)
    wv = wkv[:, :, NOPE_DIM:].reshape(KV_LORA, D_HEADS // 2, 2, V_DIM)
    zero = jnp.zeros_like(wv[:, :, 0])
    wv_pairs = jnp.concatenate([jnp.concatenate([wv[:, :, 0], zero], axis=2),
                                jnp.concatenate([zero, wv[:, :, 1]], axis=2)], axis=0)
    wv_pairs = jnp.transpose(wv_pairs, (1, 0, 2)).astype(BF16)
    w_c = cd_w_out[:C_WIDTH].astype(BF16)
    w_d = cd_w_out[C_WIDTH:].astype(BF16)
    return wuq_p, wk_p, wk_full, wv_p, wv_pairs, q_gain_p, k_gain_p, w_c, w_d


def _mix_even(proj, tail, w, lb, past, tmaj):
    bsz, t, _ = proj.shape
    n = bsz * t
    s0 = None if past is None else past[0][0]
    a_out, s_t = hgrn2(proj, lb, w['hgrn_out_norm'][0], s0)
    x0 = jnp.zeros((bsz, S5_STATE_W), F32) if past is None else s5_pack_state(past[1][0], past[2][0])
    if tmaj is not None:
        u_tb = tail.reshape(t, bsz, B_WIDTH)
    else:
        u_tb = jnp.transpose(proj[:, :, AB_IN - B_WIDTH:], (1, 0, 2))
    b_tb, x_t = s5_mix(u_tb, w['s5_wb'], w['s5_wc'], w['s5_lam'], w['s5_d'], w['s5_w_glu'], w['s5_b_glu'], x0)
    if tmaj is not None:
        b_out = b_tb.reshape(t, bsz * B_WIDTH)
    else:
        b_out = jnp.transpose(b_tb, (1, 0, 2)).reshape(n, B_WIDTH)
    s5re, s5im = s5_unpack_state(x_t)
    return a_out.reshape(n, A_WIDTH), b_out, dict(hgrn=s_t, s5re=s5re, s5im=s5im)


def _mix_odd(proj, w, pos0, past):
    bsz, t, _ = proj.shape
    n = bsz * t
    c_out, v_rows = gmlp(proj, w['gmlp_v_norm'], w['gmlp_w_s'], w['gmlp_b_s'])
    tables = rope_tables(pos0 + jnp.arange(t, dtype=jnp.int32))
    q_args = (w['mla_q_norm'], w['mla_kv_norm'], w['mla_wuq_p'], w['mla_q_gain_p'])
    if past is None:
        q, ckv, kpe, k, vt = mla_q(proj, tables, *q_args,
                                   kv_weights=(w['mla_wk_full'], w['mla_wv_p'], w['mla_k_gain_p']))
        d_out = attention(q, k, vt, pos0)
    else:
        flat = mla_q(proj.reshape(1, n, -1), [jnp.tile(tb, (bsz, 1)) for tb in tables], *q_args)
        q, ckv, kpe = [a.reshape(bsz, t, -1) for a in flat]
        d_out = mla_decode(q, past[3][0], past[4][0], ckv, kpe, w['mla_wk_p'], w['mla_wv_pairs'],
                           w['mla_k_gain_p'], pos0)
    return c_out.reshape(n, C_WIDTH), d_out.reshape(n, D_HEADS * V_DIM), dict(gv=v_rows, ckv=ckv, kpe=kpe)


def _trunks(streams, w):
    lb_all = jnp.cumsum(jax.nn.softmax(w['hgrn_lb_logits'], axis=0), axis=0)
    ffn_w = lambda name, l: (w[name + '_norm'][l], w[name + '_w_gate'], w[name + '_w_up'], w[name + '_w_down'], l)
    shapes = [x.shape[:2] for x, _, _ in streams]
    tmajs = [(bsz, t) if t % FFN_ROWS == 0 else None for bsz, t in shapes]
    xs = [x.reshape(-1, D_MODEL) for x, _, _ in streams]
    outs = [{} for _ in streams]
    for l in range(2):
        even = l % 2 == 0
        w_in = w['ab_w_in'] if even else w['cd_w_in_p']
        res = ffn([dict(x=x, time_major=tm if even else None) for x, tm in zip(xs, tmajs)],
                  *ffn_w('ffn1', l), post=(w['mix_norm'][l], w_in))
        groups = []
        for r, (bsz, t), tm, (_, pos0, past), o in zip(res, shapes, tmajs, streams, outs):
            proj = r[1].reshape(bsz, t, -1)
            if even:
                a1, a2, new = _mix_even(proj, r[2] if tm is not None else None, w, lb_all[l], past, tm)
            else:
                a1, a2, new = _mix_odd(proj, w, pos0, past)
            o.update(new)
            groups.append(dict(x=r[0], a1=a1, a2=a2, time_major=tm if even else None))
        pre_w = (w['ab_w_out_a'], w['ab_w_out_b']) if even else (w['cd_w_out_c'], w['cd_w_out_d'])
        xs = [r[0] for r in ffn(groups, *ffn_w('ffn2', l), pre_w=pre_w)]
    return [x.reshape(bsz, t, D_MODEL) for x, (bsz, t) in zip(xs, shapes)], outs


def kernel(x_prompt, x_sample, state_hgrn, state_s5_re, state_s5_im, cache_mla_ckv, cache_mla_kpe, ffn1_norm, ffn1_w_gate, ffn1_w_up, ffn1_w_down, mix_norm, ffn2_norm, ffn2_w_gate, ffn2_w_up, ffn2_w_down, ab_w_in, hgrn_lb_logits, hgrn_out_norm, s5_lambda_re, s5_lambda_im, s5_log_dt, s5_b_re, s5_b_im, s5_c_re, s5_c_im, s5_d, s5_w_glu, s5_b_glu, ab_w_out, cd_w_in, gmlp_v_norm, gmlp_w_s, gmlp_b_s, mla_q_norm, mla_w_uq, mla_kv_norm, mla_w_ukv, mla_q_gain, mla_k_gain, cd_w_out):
    bf = lambda a: a.astype(BF16)
    wb, wc, lam_rows = s5_prepare(s5_lambda_re[0], s5_lambda_im[0], s5_log_dt[0],
                                  s5_b_re[0], s5_b_im[0], s5_c_re[0], s5_c_im[0])
    wuq_p, wk_p, wk_full, wv_p, wv_pairs, q_gain_p, k_gain_p, w_c, w_d = pack_mla_weights(
        mla_w_uq[0], mla_w_ukv[0], mla_q_gain[0], mla_k_gain[0], cd_w_out[0])
    w = dict(
        ffn1_norm=ffn1_norm, ffn1_w_gate=ffn1_w_gate, ffn1_w_up=ffn1_w_up, ffn1_w_down=ffn1_w_down,
        ffn2_norm=ffn2_norm, ffn2_w_gate=ffn2_w_gate, ffn2_w_up=ffn2_w_up, ffn2_w_down=ffn2_w_down,
        mix_norm=mix_norm, ab_w_in=bf(ab_w_in[0]), hgrn_lb_logits=hgrn_lb_logits, hgrn_out_norm=hgrn_out_norm,
        s5_wb=wb, s5_wc=wc, s5_lam=lam_rows, s5_d=s5_d[0].reshape(B_WIDTH), s5_w_glu=bf(s5_w_glu[0]),
        s5_b_glu=s5_b_glu[0], ab_w_out_a=bf(ab_w_out[0, :A_WIDTH]), ab_w_out_b=bf(ab_w_out[0, A_WIDTH:]),
        cd_w_in_p=bf(jnp.pad(cd_w_in[0], ((0, 0), (0, CD_IN_PAD - CD_IN)))),
        gmlp_v_norm=gmlp_v_norm[0], gmlp_w_s=gmlp_w_s[0], gmlp_b_s=gmlp_b_s[0],
        mla_q_norm=mla_q_norm[0], mla_kv_norm=mla_kv_norm[0], mla_wuq_p=wuq_p, mla_q_gain_p=q_gain_p,
        mla_wk_p=wk_p, mla_wk_full=wk_full, mla_wv_p=wv_p, mla_wv_pairs=wv_pairs, mla_k_gain_p=k_gain_p,
        cd_w_out_c=w_c, cd_w_out_d=w_d,
    )
    past_len = cache_mla_ckv.shape[2]
    (y_p, y_s), (o_p, o_s) = _trunks(
        [(x_prompt, 0, None),
         (x_sample, past_len, (state_hgrn, state_s5_re, state_s5_im, cache_mla_ckv, cache_mla_kpe))], w)
    e = lambda a: a[None]
    return (y_p, y_s, e(o_p['hgrn']), e(o_s['hgrn']), e(o_p['s5re']), e(o_p['s5im']),
            e(o_s['s5re']), e(o_s['s5im']), e(o_p['gv']), e(o_s['gv']),
            e(o_p['ckv']), e(o_p['kpe']), e(o_s['ckv']), e(o_s['kpe']))
```

```python
import functools
import math

import jax
import jax.numpy as jnp
from jax import lax
from jax.experimental import pallas as pl
from jax.experimental.pallas import tpu as pltpu

D_MODEL = 1024
D_FF = 2816
EPS = 1e-6
A_HEADS = 4
A_KEY = 128
A_VAL = 128
A_WIDTH = A_HEADS * A_VAL
HGRN_BLOCK = 64
B_GROUP_DIM = 16
B_WIDTH = 512
B_GROUPS = 32
B_STATE = 64
LAMBDA_RE_CEIL = -1e-4
C_CHUNK = 128
C_WIDTH = 512
C_GROUPS = 4
C_GROUP_DIM = 128
D_HEADS = 8
Q_LORA = 256
KV_LORA = 128
NOPE_DIM = 64
ROPE_DIM = 32
V_DIM = 64
QK_DIM = NOPE_DIM + ROPE_DIM
ROPE_BASE = 10000.0
CHUNK = 64
A_QK = A_HEADS * A_KEY
AB_IN = 2 * A_QK + 2 * A_WIDTH + B_WIDTH
CD_IN = 2 * C_WIDTH + Q_LORA + KV_LORA + ROPE_DIM
CD_IN_PAD = 1536

LANES = 128
SUBLANES = 8
VMEM_LIMIT = 56 * 1024 * 1024
HGRN_SUB = 16
HGRN_STEP_ROWS = 128
HEAD_PAD = LANES
NEG_BIG = -1e30
EXP_CLAMP = 80.0
V_ONE_ROW = V_DIM
ATTN_TK = 512
ATTN_HEAD_GROUP = 4

BF16 = jnp.bfloat16
F32 = jnp.float32


def _cparams(sem):
    return pltpu.CompilerParams(dimension_semantics=sem, vmem_limit_bytes=VMEM_LIMIT)


def _rms(xf, width):
    return xf * lax.rsqrt(jnp.sum(xf * xf, axis=-1, keepdims=True) * (1.0 / width) + EPS)


def _dot(a, b):
    return jnp.dot(a, b, preferred_element_type=F32)


def _dot_nt(a, b):
    return lax.dot_general(a, b, (((1,), (1,)), ((), ())), preferred_element_type=F32)


def _dot_tn(a, b):
    return lax.dot_general(a, b, (((0,), (0,)), ((), ())), preferred_element_type=F32)


def _div_pow2(x, d):
    assert d & (d - 1) == 0
    return lax.shift_right_logical(x, jnp.int32(d.bit_length() - 1))


def _mod_pow2(x, d):
    assert d & (d - 1) == 0
    return x & jnp.int32(d - 1)


def _row_tile(n, cap):
    t = min(n, cap)
    assert n % t == 0, (n, t)
    return t


FFN_CHUNK = 256
FFN_CHUNKS = D_FF // FFN_CHUNK
POST_CHUNK = 512
FFN_ROWS = 512


def _ffn_rows(x_ref, pre_refs, w_refs, post_refs, o_ref, p_ref, tail_ref, acc_ref):
    g_ref, wg_ref, wu_ref, wd_ref = w_refs
    rows = x_ref.shape[0]
    x = x_ref[...]
    if pre_refs is not None:
        a1_ref, a2_ref, w1_ref, w2_ref = pre_refs
        x = x + _dot(a1_ref[...], w1_ref[...]) + _dot(a2_ref[...], w2_ref[...])
    h = (_rms(x, D_MODEL) * g_ref[...]).astype(BF16)
    for j in range(FFN_CHUNKS):
        gate = _dot(h, wg_ref[j])
        up = _dot(h, wu_ref[j])
        act = (gate * jax.nn.sigmoid(gate) * up).astype(BF16)
        part = _dot(act, wd_ref[j])
        if j == 0:
            acc_ref[:rows] = part
        else:
            acc_ref[:rows] += part
    out = x + 0.5 * acc_ref[:rows]
    o_ref[...] = out
    if post_refs is not None:
        g2_ref, wp_ref = post_refs
        h2 = (_rms(out, D_MODEL) * g2_ref[...]).astype(BF16)
        n_post = wp_ref.shape[0]
        for c in range(n_post):
            cs = slice(c * POST_CHUNK, (c + 1) * POST_CHUNK)
            part = _dot(h2, wp_ref[c]).astype(p_ref.dtype)
            if tail_ref is not None and c == n_post - 1:
                tail_ref[...] = part
            else:
                p_ref[:, cs] = part


def _ffn_kernel(*refs, tiles, has_pre, has_post, split_tails):
    refs = list(refs)
    take = lambda k: [refs.pop(0) for _ in range(k)]
    group_in = [take(3 if has_pre else 1) for _ in tiles]
    pre_w = take(2) if has_pre else None
    g_ref, wg_in, wu_in, wd_in = take(4)
    post_in = take(2) if has_post else None
    group_out = [take(1 + has_post + split) for split in split_tails]
    wg_s, wu_s, wd_s = take(3)
    pre_s = take(2) if has_pre else None
    wp_s = take(1)[0] if has_post else None
    acc_ref, = refs
    w_refs = (g_ref, wg_s, wu_s, wd_s)
    step = pl.program_id(0)

    @pl.when(step < FFN_CHUNKS)
    def _():
        wg_s[step] = wg_in[...].astype(BF16)
        wu_s[step] = wu_in[...].astype(BF16)
        wd_s[step] = wd_in[...].astype(BF16)

    if has_pre:
        @pl.when(step == 0)
        def _():
            pre_s[0][...] = pre_w[0][...].astype(BF16)
            pre_s[1][...] = pre_w[1][...].astype(BF16)
        pre_w = pre_s

    post_refs = None
    if has_post:
        n_post = wp_s.shape[0]
        assert n_post <= FFN_CHUNKS

        @pl.when(step < n_post)
        def _():
            wp_s[step] = post_in[1][...].astype(BF16)
        post_refs = (post_in[0], wp_s)

    first = FFN_CHUNKS
    for n_tiles, ins, outs, split in zip(tiles, group_in, group_out, split_tails):
        pre_refs = (ins[1], ins[2], *pre_w) if has_pre else None
        p_ref = outs[1] if has_post else None
        tail_ref = outs[2] if split else None

        @pl.when((step >= first) & (step < first + n_tiles))
        def _(ins=ins, outs=outs, pre_refs=pre_refs, p_ref=p_ref, tail_ref=tail_ref):
            _ffn_rows(ins[0], pre_refs, w_refs, post_refs, outs[0], p_ref, tail_ref, acc_ref)

        first += n_tiles


def ffn(groups, g, wg, wu, wd, layer, pre_w=None, post=None, tm_cap=FFN_ROWS):
    fixed = lambda r, c: pl.BlockSpec((r, c), lambda i: (0, 0), pipeline_mode=pl.Buffered(1))
    args, specs, out_specs, out_shape, tiles, split_tails = [], [], [], [], [], []
    if pre_w is not None:
        w_out, k1 = pre_w
        pre_k = (k1, w_out.shape[1] - k1)
        assert pre_k[0] == pre_k[1], "the two out-projection operands are addressed as equal row blocks"
    if post is not None:
        post_w = post[1].shape[2]
        assert post_w % POST_CHUNK == 0
    first = FFN_CHUNKS
    for gi, grp in enumerate(groups):
        x = grp['x']
        n = x.shape[0]
        tm = _row_tile(n, tm_cap)
        n_tiles = n // tm
        local = lambda i, first=first, n_tiles=n_tiles: jnp.clip(i - first, 0, n_tiles - 1)
        mode = {} if gi == 0 else dict(pipeline_mode=pl.Buffered(1))
        rows = lambda w, local=local, tm=tm, mode=mode: pl.BlockSpec((tm, w), lambda i: (local(i), 0), **mode)
        tmaj = grp.get('time_major')
        if tmaj is not None:
            streams, steps = tmaj
            assert steps % tm == 0 and streams * steps == n
            per_stream = steps // tm
            by_time = lambda w, local=local, tm=tm, per_stream=per_stream: pl.BlockSpec(
                (tm, w), lambda i: (local(i) % per_stream, local(i) // per_stream))
        args.append(x)
        specs.append(rows(D_MODEL))
        if pre_w is not None:
            args += [grp['a1'], grp['a2']]
            specs += [rows(grp['a1'].shape[1]),
                      by_time(pre_k[1]) if tmaj is not None else rows(grp['a2'].shape[1])]
        out_specs.append(rows(D_MODEL))
        out_shape.append(jax.ShapeDtypeStruct((n, D_MODEL), F32))
        split = post is not None and tmaj is not None
        if post is not None:
            head_w = post_w - (POST_CHUNK if split else 0)
            out_specs.append(rows(head_w))
            out_shape.append(jax.ShapeDtypeStruct((n, head_w), BF16))
            if split:
                out_specs.append(by_time(POST_CHUNK))
                out_shape.append(jax.ShapeDtypeStruct((steps, streams * POST_CHUNK), BF16))
        tiles.append(n_tiles)
        split_tails.append(split)
        first += n_tiles
    if pre_w is not None:
        half_rows = lambda blk: pl.BlockSpec((None, pre_k[0], D_MODEL), lambda i: (0, blk, 0),
                                             pipeline_mode=pl.Buffered(1))
        args += [w_out, w_out]
        specs += [half_rows(0), half_rows(1)]
    chunk = lambda i: jnp.minimum(i, FFN_CHUNKS - 1)
    args += [g.reshape(1, D_MODEL), wg, wu, wd]
    specs += [fixed(1, D_MODEL),
              pl.BlockSpec((None, D_MODEL, FFN_CHUNK), lambda i: (layer, 0, chunk(i))),
              pl.BlockSpec((None, D_MODEL, FFN_CHUNK), lambda i: (layer, 0, chunk(i))),
              pl.BlockSpec((None, FFN_CHUNK, D_MODEL), lambda i: (layer, chunk(i), 0))]
    if post is not None:
        n_post = post_w // POST_CHUNK
        args += [post[0].reshape(1, D_MODEL), post[1]]
        specs += [fixed(1, D_MODEL),
                  pl.BlockSpec((None, D_MODEL, POST_CHUNK), lambda i: (0, 0, jnp.minimum(i, n_post - 1)))]
    kern = functools.partial(_ffn_kernel, tiles=tuple(tiles), has_pre=pre_w is not None,
                             has_post=post is not None, split_tails=tuple(split_tails))
    res = pl.pallas_call(
        kern,
        grid=(first,),
        in_specs=specs,
        out_specs=out_specs,
        out_shape=out_shape,
        scratch_shapes=(
            [pltpu.VMEM((FFN_CHUNKS, D_MODEL, FFN_CHUNK), BF16),
             pltpu.VMEM((FFN_CHUNKS, D_MODEL, FFN_CHUNK), BF16),
             pltpu.VMEM((FFN_CHUNKS, FFN_CHUNK, D_MODEL), BF16)]
            + ([pltpu.VMEM((k, D_MODEL), BF16) for k in pre_k] if pre_w is not None else [])
            + ([pltpu.VMEM((post_w // POST_CHUNK, D_MODEL, POST_CHUNK), BF16)] if post is not None else [])
            + [pltpu.VMEM((max(min(tm_cap, grp['x'].shape[0]) for grp in groups), D_MODEL), F32)]),
        compiler_params=_cparams(("arbitrary",)),
        name="ffn",
    )(*args)
    out, k = [], 0
    for split in split_tails:
        width = 1 + (post is not None) + split
        out.append(tuple(res[k:k + width]))
        k += width
    return out


def _hgrn_kernel(q_ref, f_ref, i_ref, g_ref, lb_ref, og_ref, s0_ref, y_ref, st_ref, state_ref,
                 *, tt, blk, has_past):
    for si in range(q_ref.shape[0]):
        _hgrn_stream(q_ref.at[si], f_ref.at[si], i_ref.at[si], g_ref.at[si], lb_ref, og_ref, s0_ref.at[si],
                     y_ref.at[si], st_ref.at[si], state_ref.at[si], tt=tt, blk=blk, has_past=has_past)


def _hgrn_stream(q_ref, f_ref, i_ref, g_ref, lb_ref, og_ref, s0_ref, y_ref, st_ref, state_ref,
                 *, tt, blk, has_past):
    t_idx = pl.program_id(1)
    n_chunks = tt // blk
    n_sub = blk // HGRN_SUB

    @pl.when(t_idx == 0)
    def _():
        if has_past:
            for h in range(A_HEADS):
                state_ref[h] = s0_ref[h].T
        else:
            state_ref[...] = jnp.zeros_like(state_ref)

    row = lax.broadcasted_iota(jnp.int32, (tt, tt), 0)
    col = lax.broadcasted_iota(jnp.int32, (tt, tt), 1)
    same_chunk = _div_pow2(row, blk) == _div_pow2(col, blk)
    causal = same_chunk & (col <= row)
    tril = jnp.where(causal, 1.0, 0.0).astype(BF16)
    heads = [slice(h * LANES, (h + 1) * LANES) for h in range(A_HEADS)]

    q = q_ref[...].astype(F32)
    v_bf = i_ref[...].astype(BF16)
    lb = lb_ref[...]
    f = lb + (1.0 - lb) * jax.nn.sigmoid(f_ref[...].astype(F32))
    k = 1.0 - f
    logf = jnp.log(f)
    hi = logf.astype(BF16)
    mid = (logf - hi.astype(F32)).astype(BF16)
    b = _dot(tril, hi) + _dot(tril, mid)

    q_slab, k_slab, ends = [], [], []
    for c in range(n_chunks):
        rows = [slice(c * blk + i * HGRN_SUB, c * blk + (i + 1) * HGRN_SUB) for i in range(n_sub)]
        mids = [b[r.start + HGRN_SUB // 2 - 1:r.start + HGRN_SUB // 2, :] for r in rows]
        q_slab.append([(q[r] * jnp.exp(jnp.minimum(b[r] - mids[i], EXP_CLAMP))).astype(BF16)
                       for i, r in enumerate(rows)])
        k_slab.append([[None if j > i else
                        (k[r] * jnp.exp(jnp.minimum(mids[i] - b[r], EXP_CLAMP) if j == i
                                        else mids[i] - b[r])).astype(BF16)
                        for i in range(n_sub)] for j, r in enumerate(rows)])
        ends.append(b[(c + 1) * blk - 1:(c + 1) * blk, :])
    b_end_rows = (jnp.broadcast_to(ends[0], (blk, A_WIDTH)) if n_chunks == 1 else
                  jnp.concatenate([jnp.broadcast_to(e, (blk, A_WIDTH)) for e in ends], axis=0))
    q_e = (q * jnp.exp(b)).astype(BF16)
    k_d = (k * jnp.exp(b_end_rows - b)).astype(BF16)
    zero_slab = jnp.zeros((HGRN_SUB, LANES), BF16)
    zero_blk = jnp.zeros((blk, LANES), BF16)
    chunk_rows = [slice(c * blk, (c + 1) * blk) for c in range(n_chunks)]

    outs = []
    for h, sl in enumerate(heads):
        q_hat = jnp.concatenate(
            [jnp.concatenate([q_slab[c][j][:, sl] if i == j else zero_slab for i in range(n_sub)], axis=1)
             for c in range(n_chunks) for j in range(n_sub)], axis=0)
        k_hat = jnp.concatenate(
            [jnp.concatenate([k_slab[c][j][i][:, sl] if j <= i else zero_slab for i in range(n_sub)], axis=1)
             for c in range(n_chunks) for j in range(n_sub)], axis=0)
        att = _dot_nt(q_hat, k_hat)
        att = jnp.where(causal, att, 0.0).astype(BF16)
        v_spread = jnp.concatenate(
            [jnp.concatenate([v_bf[rs, sl] if c == d else zero_blk for d in range(n_chunks)], axis=1)
             for c, rs in enumerate(chunk_rows)], axis=0)
        kv = _dot_tn(v_spread, k_d[:, sl])
        s_t = state_ref[h]
        states = []
        for c in range(n_chunks):
            states.append(s_t.astype(BF16))
            s_t = s_t * jnp.exp(ends[c][:, sl]) + kv[c * A_VAL:(c + 1) * A_VAL]
        state_ref[h] = s_t
        o_inter = jnp.concatenate([_dot_nt(q_e[rs, sl], st) for rs, st in zip(chunk_rows, states)], axis=0)
        o = _dot(att, v_bf[:, sl]) + o_inter
        outs.append(_rms(o, A_VAL) * og_ref[...])
    y = jnp.concatenate(outs, axis=1) * jax.nn.sigmoid(g_ref[...].astype(F32))
    y_ref[...] = y.astype(y_ref.dtype)

    @pl.when(t_idx == pl.num_programs(1) - 1)
    def _():
        for h in range(A_HEADS):
            st_ref[h] = state_ref[h].T


def hgrn2(proj, lb, out_g, s0, tt_cap=256):
    bsz, t, _ = proj.shape
    blk = min(HGRN_BLOCK, t)
    tt = _row_tile(t, tt_cap)
    nb = _row_tile(bsz, max(1, HGRN_STEP_ROWS // tt))
    has_past = s0 is not None
    if s0 is None:
        s0 = jnp.zeros((bsz, A_HEADS, A_KEY, A_VAL), F32)
    seg = lambda s: pl.BlockSpec((nb, tt, A_WIDTH), lambda b, i, s=s: (b, i, s))
    kern = functools.partial(_hgrn_kernel, tt=tt, blk=blk, has_past=has_past)
    return pl.pallas_call(
        kern,
        grid=(bsz // nb, t // tt),
        in_specs=[
            seg(0), seg(1), seg(2), seg(3),
            pl.BlockSpec((1, A_QK), lambda b, i: (0, 0)),
            pl.BlockSpec((1, A_VAL), lambda b, i: (0, 0)),
            pl.BlockSpec((nb, A_HEADS, A_KEY, A_VAL), lambda b, i: (b, 0, 0, 0)),
        ],
        out_specs=[
            pl.BlockSpec((nb, tt, A_WIDTH), lambda b, i: (b, i, 0)),
            pl.BlockSpec((nb, A_HEADS, A_KEY, A_VAL), lambda b, i: (b, 0, 0, 0)),
        ],
        out_shape=[
            jax.ShapeDtypeStruct((bsz, t, A_WIDTH), BF16),
            jax.ShapeDtypeStruct((bsz, A_HEADS, A_KEY, A_VAL), F32),
        ],
        scratch_shapes=[pltpu.VMEM((nb, A_HEADS, A_VAL, A_KEY), F32)],
        compiler_params=_cparams(("parallel", "arbitrary")),
        name="hgrn2",
    )(proj, proj, proj, proj, lb.reshape(1, A_QK), out_g.reshape(1, A_VAL), s0)


S5_SUPER = 4
S5_SUPER_U = B_WIDTH // S5_SUPER
S5_SUPER_X = 2 * (B_GROUPS // S5_SUPER) * B_STATE
S5_HALF = S5_SUPER_X // 2
S5_STATE_W = S5_SUPER * S5_SUPER_X
S5_SCAN_GROUP = 2


def _s5_kernel(u_ref, wb_ref, wc_ref, lam_ref, d_ref, wglu_ref, bglu_ref, x0_ref,
               o_ref, xt_ref, xbuf_ref, state_ref, *, tt):
    t_idx = pl.program_id(1)
    rows = tt * SUBLANES

    @pl.when(t_idx == 0)
    def _():
        state_ref[...] = x0_ref[...]

    u_bf = u_ref[...].reshape(rows, B_WIDTH)
    u = u_bf.astype(F32)
    for j in range(S5_SUPER):
        xbuf_ref[:, j * S5_SUPER_X:(j + 1) * S5_SUPER_X] = _dot(
            u_bf[:, j * S5_SUPER_U:(j + 1) * S5_SUPER_U], wb_ref[j])

    for j0 in range(0, S5_SUPER, S5_SCAN_GROUP):
        cols, lams = [], []
        for j in range(j0, j0 + S5_SCAN_GROUP):
            re_cols = slice(j * S5_SUPER_X, j * S5_SUPER_X + S5_HALF)
            im_cols = slice(j * S5_SUPER_X + S5_HALF, (j + 1) * S5_SUPER_X)
            cols.append((re_cols, im_cols))
            lams.append((jnp.broadcast_to(lam_ref[0:1, re_cols], (SUBLANES, S5_HALF)),
                         jnp.broadcast_to(lam_ref[1:2, re_cols], (SUBLANES, S5_HALF))))

        def step(t, carry, cols=cols, lams=lams):
            rows_t = pl.ds(pl.multiple_of(t * SUBLANES, SUBLANES), SUBLANES)
            out = []
            for (re_cols, im_cols), (lam_re, lam_im), (x_re, x_im) in zip(cols, lams, carry):
                n_re = lam_re * x_re - lam_im * x_im + xbuf_ref[rows_t, re_cols]
                n_im = lam_re * x_im + lam_im * x_re + xbuf_ref[rows_t, im_cols]
                xbuf_ref[rows_t, re_cols] = n_re
                xbuf_ref[rows_t, im_cols] = n_im
                out.append((n_re, n_im))
            return tuple(out)

        init = tuple((state_ref[:, rc], state_ref[:, ic]) for rc, ic in cols)
        last = lax.fori_loop(0, tt, step, init, unroll=True)
        for (rc, ic), (x_re, x_im) in zip(cols, last):
            state_ref[:, rc] = x_re
            state_ref[:, ic] = x_im

    ys = []
    for j in range(S5_SUPER):
        xs = xbuf_ref[:, j * S5_SUPER_X:(j + 1) * S5_SUPER_X].astype(BF16)
        ys.append(_dot(xs, wc_ref[j]))
    y = jnp.concatenate(ys, axis=1) + d_ref[...] * u
    z = jax.nn.gelu(y)
    gate = jax.nn.sigmoid(_dot(z.astype(BF16), wglu_ref[...]) + bglu_ref[...])
    o_ref[...] = (z * gate).astype(o_ref.dtype).reshape(tt, SUBLANES, B_WIDTH)

    @pl.when(t_idx == pl.num_programs(1) - 1)
    def _():
        xt_ref[...] = state_ref[...]


def s5_mix(u_tb, wb, wc, lam, d, w_glu, b_glu, x0, tt_cap=128):
    t, bsz, _ = u_tb.shape
    tt = _row_tile(t, tt_cap)
    kern = functools.partial(_s5_kernel, tt=tt)
    return pl.pallas_call(
        kern,
        grid=(bsz // SUBLANES, t // tt),
        in_specs=[
            pl.BlockSpec((tt, SUBLANES, B_WIDTH), lambda b, i: (i, b, 0)),
            pl.BlockSpec((S5_SUPER, S5_SUPER_U, S5_SUPER_X), lambda b, i: (0, 0, 0)),
            pl.BlockSpec((S5_SUPER, S5_SUPER_X, S5_SUPER_U), lambda b, i: (0, 0, 0)),
            pl.BlockSpec((2, S5_STATE_W), lambda b, i: (0, 0)),
            pl.BlockSpec((1, B_WIDTH), lambda b, i: (0, 0)),
            pl.BlockSpec((B_WIDTH, B_WIDTH), lambda b, i: (0, 0)),
            pl.BlockSpec((1, B_WIDTH), lambda b, i: (0, 0)),
            pl.BlockSpec((SUBLANES, S5_STATE_W), lambda b, i: (b, 0)),
        ],
        out_specs=[
            pl.BlockSpec((tt, SUBLANES, B_WIDTH), lambda b, i: (i, b, 0)),
            pl.BlockSpec((SUBLANES, S5_STATE_W), lambda b, i: (b, 0)),
        ],
        out_shape=[
            jax.ShapeDtypeStruct((t, bsz, B_WIDTH), BF16),
            jax.ShapeDtypeStruct((bsz, S5_STATE_W), F32),
        ],
        scratch_shapes=[pltpu.VMEM((tt * SUBLANES, S5_STATE_W), F32),
                        pltpu.VMEM((SUBLANES, S5_STATE_W), F32)],
        compiler_params=_cparams(("parallel", "arbitrary")),
        name="s5_mix",
    )(u_tb, wb, wc, lam, d.reshape(1, B_WIDTH), w_glu, b_glu.reshape(1, B_WIDTH), x0)


def _s5_pack_cols(a):
    return a.reshape(a.shape[:-2] + (S5_SUPER, S5_HALF))


def s5_prepare(lam_re, lam_im, log_dt, b_re, b_im, c_re, c_im):
    lr = jnp.minimum(lam_re, LAMBDA_RE_CEIL)
    li = lam_im
    dt = jnp.exp(log_dt)[:, None]
    mag = jnp.exp(lr * dt)
    bar_re = mag * jnp.cos(li * dt)
    bar_im = mag * jnp.sin(li * dt)
    den = lr * lr + li * li
    coef_re = (((bar_re - 1.0) * lr + bar_im * li) / den)[:, :, None]
    coef_im = ((bar_im * lr - (bar_re - 1.0) * li) / den)[:, :, None]
    bb_re = coef_re * b_re - coef_im * b_im
    bb_im = coef_re * b_im + coef_im * b_re
    gps = B_GROUPS // S5_SUPER
    eye = jnp.eye(gps, dtype=F32)

    def in_block(bm):
        bm = bm.reshape(S5_SUPER, gps, B_STATE, B_GROUP_DIM)
        blk = jnp.einsum('jgph,gk->jghkp', bm, eye)
        return blk.reshape(S5_SUPER, gps * B_GROUP_DIM, gps * B_STATE)

    def out_block(cm):
        cm = cm.reshape(S5_SUPER, gps, B_GROUP_DIM, B_STATE)
        blk = jnp.einsum('jghp,gk->jkpgh', cm, eye)
        return blk.reshape(S5_SUPER, gps * B_STATE, gps * B_GROUP_DIM)

    wb = jnp.concatenate([in_block(bb_re), in_block(bb_im)], axis=2).astype(BF16)
    wc = jnp.concatenate([out_block(c_re), out_block(-c_im)], axis=1).astype(BF16)
    lam_rows = jnp.stack([
        jnp.concatenate([_s5_pack_cols(bar_re)] * 2, axis=-1).reshape(S5_STATE_W),
        jnp.concatenate([_s5_pack_cols(bar_im)] * 2, axis=-1).reshape(S5_STATE_W)])
    return wb, wc, lam_rows


def s5_pack_state(x_re, x_im):
    bsz = x_re.shape[0]
    return jnp.concatenate([_s5_pack_cols(x_re), _s5_pack_cols(x_im)], axis=-1).reshape(bsz, S5_STATE_W)


def s5_unpack_state(x):
    bsz = x.shape[0]
    x = x.reshape(bsz, S5_SUPER, 2, S5_HALF)
    return (x[:, :, 0].reshape(bsz, B_GROUPS, B_STATE), x[:, :, 1].reshape(bsz, B_GROUPS, B_STATE))


def _gmlp_kernel(u_ref, v_ref, gn_ref, ws_ref, bs_ref, o_ref, vr_ref, *, tt, chunk, all_rows):
    n_chunks = tt // chunk
    u = jax.nn.gelu(u_ref[...].astype(F32))
    v = jax.nn.gelu(v_ref[...].astype(F32))
    vn = _rms(v, C_WIDTH) * gn_ref[...]
    vn_bf = vn.astype(BF16)
    row = lax.broadcasted_iota(jnp.int32, (chunk, chunk), 0)
    col = lax.broadcasted_iota(jnp.int32, (chunk, chunk), 1)
    keep = col <= row
    for g in range(C_GROUPS):
        w = jnp.where(keep, ws_ref[g], 0.0).astype(BF16)
        bias = bs_ref[:, g:g + 1]
        cs = slice(g * C_GROUP_DIM, (g + 1) * C_GROUP_DIM)
        for c in range(n_chunks):
            rs = slice(c * chunk, (c + 1) * chunk)
            s = _dot(w, vn_bf[rs, cs]) + bias
            o_ref[rs, cs] = (u[rs, cs] * s).astype(o_ref.dtype)

    if all_rows:
        vr_ref[...] = vn
    else:
        @pl.when(pl.program_id(1) == pl.num_programs(1) - 1)
        def _():
            vr_ref[...] = vn[tt - chunk:, :]


def gmlp(proj, v_norm, w_s, b_s, tt_cap=1024):
    bsz, t, width = proj.shape
    chunk = min(t, C_CHUNK)
    all_rows = t == chunk
    if all_rows:
        c_out, v_rows = gmlp_call(proj.reshape(1, bsz * t, width), v_norm, w_s, b_s, chunk, True, tt_cap)
        return c_out.reshape(bsz, t, C_WIDTH), v_rows.reshape(bsz, t, C_WIDTH)
    return gmlp_call(proj, v_norm, w_s, b_s, chunk, False, tt_cap)


def gmlp_call(proj, v_norm, w_s, b_s, chunk, all_rows, tt_cap):
    bsz, t, _ = proj.shape
    tt = _row_tile(t, tt_cap)
    vr_rows = tt if all_rows else chunk
    ws = w_s[:, :chunk, :chunk]
    bs_t = b_s[:, :chunk].T
    kern = functools.partial(_gmlp_kernel, tt=tt, chunk=chunk, all_rows=all_rows)
    return pl.pallas_call(
        kern,
        grid=(bsz, t // tt),
        in_specs=[
            pl.BlockSpec((None, tt, C_WIDTH), lambda b, i: (b, i, 0)),
            pl.BlockSpec((None, tt, C_WIDTH), lambda b, i: (b, i, 1)),
            pl.BlockSpec((1, C_WIDTH), lambda b, i: (0, 0)),
            pl.BlockSpec((C_GROUPS, chunk, chunk), lambda b, i: (0, 0, 0)),
            pl.BlockSpec((chunk, C_GROUPS), lambda b, i: (0, 0)),
        ],
        out_specs=[
            pl.BlockSpec((None, tt, C_WIDTH), lambda b, i: (b, i, 0)),
            pl.BlockSpec((None, vr_rows, C_WIDTH), (lambda b, i: (b, i, 0)) if all_rows else (lambda b, i: (b, 0, 0))),
        ],
        out_shape=[
            jax.ShapeDtypeStruct((bsz, t, C_WIDTH), BF16),
            jax.ShapeDtypeStruct((bsz, t if all_rows else chunk, C_WIDTH), F32),
        ],
        compiler_params=_cparams(("parallel", "arbitrary")),
        name="gmlp",
    )(proj, proj, v_norm.reshape(1, C_WIDTH), ws, bs_t)


def _rope_lanes(x, cos_t, sin_lo, sin_hi):
    half = ROPE_DIM // 2
    return (x * cos_t + pltpu.roll(x, LANES - half, axis=1) * sin_lo
            + pltpu.roll(x, half, axis=1) * sin_hi)


def _mla_q_kernel(cq_ref, ckv_ref, kpe_ref, qn_ref, kvn_ref, wuq_ref, qg_ref,
                  qc_ref, qs_ref, kc_ref, ks1_ref, ks2_ref, *rest, with_kv):
    if with_kv:
        wk_ref, wvt_ref, kg_ref, q_ref, ckv_o_ref, kpe_o_ref, k_ref, vt_ref = rest
    else:
        q_ref, ckv_o_ref, kpe_o_ref = rest
    cqn = (_rms(cq_ref[...].astype(F32), Q_LORA) * qn_ref[...]).astype(BF16)
    width = D_HEADS * HEAD_PAD
    q_all = _dot(cqn, wuq_ref[...])
    qc, qs = qc_ref[...], qs_ref[...]
    scale = QK_DIM ** -0.5 * math.log2(math.e)
    for h in range(D_HEADS):
        sl = slice(h * HEAD_PAD, (h + 1) * HEAD_PAD)
        rot = slice(width + h * HEAD_PAD, width + (h + 1) * HEAD_PAD)
        qh = q_all[:, sl] * qc + q_all[:, rot] * qs
        qh = _rms(qh, QK_DIM) * (qg_ref[...] * scale)
        q_ref[:, sl] = qh.astype(q_ref.dtype)
    ckv = _rms(ckv_ref[...].astype(F32), KV_LORA) * kvn_ref[...]
    kpe = _rope_lanes(kpe_ref[...].astype(F32), kc_ref[...], ks1_ref[...], ks2_ref[...])
    ckv_o_ref[...] = ckv
    kpe_o_ref[...] = kpe[:, :ROPE_DIM]
    if with_kv:
        ckv_bf = ckv.astype(BF16)
        ckpe = jnp.concatenate([ckv_bf, kpe.astype(BF16)], axis=1)
        k_all = _dot(ckpe, wk_ref[...])
        for h in range(D_HEADS):
            sl = slice(h * HEAD_PAD, (h + 1) * HEAD_PAD)
            k_ref[:, sl] = (_rms(k_all[:, sl], QK_DIM) * kg_ref[...]).astype(k_ref.dtype)
        row = lax.broadcasted_iota(jnp.int32, (D_HEADS * HEAD_PAD, 1), 0)
        ones_row = jnp.where(_mod_pow2(row, HEAD_PAD) == V_ONE_ROW, 1.0, 0.0)
        for blk in range(vt_ref.shape[0]):
            rows = slice(blk * ATTN_TK, (blk + 1) * ATTN_TK)
            vt_ref[blk] = (_dot_nt(wvt_ref[...], ckv_bf[rows]) + ones_row).astype(vt_ref.dtype)


MLA_Q_ROWS = 1024


def mla_q(proj, tables, q_norm, kv_norm, wuq_p, q_gain_p, kv_weights=None, tm_cap=MLA_Q_ROWS):
    bsz, t, _ = proj.shape
    tm = _row_tile(t, tm_cap)
    width = D_HEADS * HEAD_PAD
    tab = pl.BlockSpec((tm, LANES), lambda b, i: (i, 0))
    row = lambda w: pl.BlockSpec((1, w), lambda b, i: (0, 0))
    full = lambda r, c: pl.BlockSpec((r, c), lambda b, i: (0, 0))
    with_kv = kv_weights is not None
    args = [proj, proj, proj, q_norm.reshape(1, Q_LORA), kv_norm.reshape(1, KV_LORA), wuq_p, q_gain_p, *tables]
    in_specs = [
        pl.BlockSpec((None, tm, Q_LORA), lambda b, i: (b, i, 2 * C_WIDTH // Q_LORA)),
        pl.BlockSpec((None, tm, KV_LORA), lambda b, i: (b, i, (2 * C_WIDTH + Q_LORA) // KV_LORA)),
        pl.BlockSpec((None, tm, LANES), lambda b, i: (b, i, (2 * C_WIDTH + Q_LORA + KV_LORA) // LANES)),
        row(Q_LORA), row(KV_LORA), full(Q_LORA, 2 * width), row(HEAD_PAD),
        tab, tab, tab, tab, tab,
    ]
    out_specs = [
        pl.BlockSpec((None, tm, width), lambda b, i: (b, i, 0)),
        pl.BlockSpec((None, tm, KV_LORA), lambda b, i: (b, i, 0)),
        pl.BlockSpec((None, tm, ROPE_DIM), lambda b, i: (b, i, 0)),
    ]
    out_shape = [
        jax.ShapeDtypeStruct((bsz, t, width), BF16),
        jax.ShapeDtypeStruct((bsz, t, KV_LORA), F32),
        jax.ShapeDtypeStruct((bsz, t, ROPE_DIM), F32),
    ]
    if with_kv:
        wk_full, wv_p, k_gain_p = kv_weights
        assert tm % ATTN_TK == 0
        args += [wk_full, wv_p.T, k_gain_p]
        in_specs += [full(*wk_full.shape), full(width, KV_LORA), row(HEAD_PAD)]
        out_specs += [pl.BlockSpec((None, tm, width), lambda b, i: (b, i, 0)),
                      pl.BlockSpec((None, tm // ATTN_TK, width, ATTN_TK), lambda b, i: (b, i, 0, 0))]
        out_shape += [jax.ShapeDtypeStruct((bsz, t, width), BF16),
                      jax.ShapeDtypeStruct((bsz, t // ATTN_TK, width, ATTN_TK), BF16)]
    return pl.pallas_call(
        functools.partial(_mla_q_kernel, with_kv=with_kv),
        grid=(bsz, t // tm),
        in_specs=in_specs,
        out_specs=out_specs,
        out_shape=out_shape,
        compiler_params=_cparams(("parallel", "parallel")),
        name="mla_q",
    )(*args)


def _for_blocks(lo, hi, block_fn):
    n_pairs = (hi - lo) // 2

    def pair(i, carry):
        block_fn(lo + 2 * i)
        block_fn(lo + 2 * i + 1)
        return carry

    lax.fori_loop(0, n_pairs, pair, 0)

    @pl.when(lo + 2 * n_pairs < hi)
    def _():
        block_fn(hi - 1)


def _attn_kernel(q_ref, k_ref, vt_ref, o_ref, sc_ref, mx_ref, acc_ref, *, tq, tk, s_len, q_pos0, diag_split):
    qi = pl.program_id(1)
    q_first = q_pos0 + qi * tq
    q_last = q_first + tq - 1
    vis_first = jnp.minimum(s_len, (q_first // CHUNK + 1) * CHUNK)
    vis_last = jnp.minimum(s_len, (q_last // CHUNK + 1) * CHUNK)
    n_full = vis_first // tk
    n_kv = (vis_last + tk - 1) // tk

    q_chunk = _div_pow2(q_first + lax.broadcasted_iota(jnp.int32, (tk, tq), 1), CHUNK)
    k_iota = lax.broadcasted_iota(jnp.int32, (tk, tq), 0)

    half = tk // 2
    q_chunk_lo = _div_pow2(q_first + lax.broadcasted_iota(jnp.int32, (half, tq), 1), CHUNK)
    q_chunk_hi = _div_pow2(q_first + half + lax.broadcasted_iota(jnp.int32, (half, tq - half), 1), CHUNK)
    k_iota_lo = lax.broadcasted_iota(jnp.int32, (half, tq), 0)
    k_iota_hi = lax.broadcasted_iota(jnp.int32, (half, tq - half), 0)

    for g0 in range(0, D_HEADS, ATTN_HEAD_GROUP):
        heads = [slice(h * HEAD_PAD, (h + 1) * HEAD_PAD) for h in range(g0, g0 + ATTN_HEAD_GROUP)]

        def track_max(h, s, cols=slice(None)):
            blk_max = jnp.max(s.reshape(s.shape[0] // SUBLANES, SUBLANES, s.shape[1]), axis=0)
            mx_ref[h, :, cols] = jnp.maximum(mx_ref[h, :, cols], blk_max)

        def score_block(j, masked, heads=heads):
            k0 = pl.multiple_of(j * tk, tk)
            visible = (_div_pow2(k0 + k_iota, CHUNK) <= q_chunk) if masked else None
            for h, sl in enumerate(heads):
                s = _dot_nt(k_ref[pl.ds(k0, tk), sl], q_ref[:, sl])
                if masked:
                    s = jnp.where(visible, s, NEG_BIG)
                sc_ref[h, j] = s
                track_max(h, s)

        def sum_block(j, heads=heads):
            for h, sl in enumerate(heads):
                p = jnp.exp2(sc_ref[h, j] - mx_ref[h][0:1, :])
                acc_ref[h] += _dot(vt_ref[j, sl, :], p.astype(BF16))

        def score_diag(j, heads=heads):
            k0 = pl.multiple_of(j * tk, tk)
            k1 = pl.multiple_of(k0 + half, half)
            vis0 = _div_pow2(k0 + k_iota_lo, CHUNK) <= q_chunk_lo
            vis1 = _div_pow2(k1 + k_iota_hi, CHUNK) <= q_chunk_hi
            for h, sl in enumerate(heads):
                s0 = jnp.where(vis0, _dot_nt(k_ref[pl.ds(k0, half), sl], q_ref[:, sl]), NEG_BIG)
                s1 = jnp.where(vis1, _dot_nt(k_ref[pl.ds(k1, half), sl], q_ref[half:, sl]), NEG_BIG)
                sc_ref[h, j, :half, :] = s0
                sc_ref[h, j, half:, half:] = s1
                track_max(h, s0)
                track_max(h, s1, slice(half, None))

        def sum_diag(j, heads=heads):
            for h, sl in enumerate(heads):
                m = mx_ref[h][0:1, :]
                p0 = jnp.exp2(sc_ref[h, j, :half, :] - m)
                p1 = jnp.exp2(sc_ref[h, j, half:, half:] - m[:, half:])
                acc_ref[h] += _dot(vt_ref[j, sl, :half], p0.astype(BF16))
                acc_ref[h, :, half:] += _dot(vt_ref[j, sl, half:], p1.astype(BF16))

        mx_ref[...] = jnp.full_like(mx_ref, NEG_BIG)
        _for_blocks(0, n_full, functools.partial(score_block, masked=False))
        if diag_split:
            score_diag(n_full)
        else:
            _for_blocks(n_full, n_kv, functools.partial(score_block, masked=True))
        for h in range(ATTN_HEAD_GROUP):
            m = jnp.max(mx_ref[h], axis=0, keepdims=True)
            mx_ref[h] = jnp.broadcast_to(m, (SUBLANES, tq))
        acc_ref[...] = jnp.zeros_like(acc_ref)
        if diag_split:
            _for_blocks(0, n_full, sum_block)
            sum_diag(n_full)
        else:
            _for_blocks(0, n_kv, sum_block)
        for hp in range(0, ATTN_HEAD_GROUP, 2):
            pair = [acc_ref[h][:V_DIM, :] / acc_ref[h][V_ONE_ROW:V_ONE_ROW + 1, :] for h in (hp, hp + 1)]
            lanes = slice((g0 + hp) * V_DIM, (g0 + hp + 2) * V_DIM)
            o_ref[:, lanes] = jnp.concatenate(pair, axis=0).T.astype(o_ref.dtype)


def attention(q, k, vt, q_pos0, tq_cap=512):
    bsz, tq_len, width = q.shape
    s_len = k.shape[1]
    n_blocks, _, tk = vt.shape[1:]
    assert n_blocks * tk == s_len and tk % LANES == 0
    tq = _row_tile(tq_len, tq_cap)
    diag_split = tq == tk and q_pos0 % tq == 0 and s_len >= q_pos0 + tq_len
    kern = functools.partial(_attn_kernel, tq=tq, tk=tk, s_len=s_len, q_pos0=q_pos0, diag_split=diag_split)
    return pl.pallas_call(
        kern,
        grid=(bsz, tq_len // tq),
        in_specs=[
            pl.BlockSpec((None, tq, width), lambda b, i: (b, i, 0)),
            pl.BlockSpec((None, s_len, width), lambda b, i: (b, 0, 0)),
            pl.BlockSpec((None, n_blocks, width, tk), lambda b, i: (b, 0, 0, 0)),
        ],
        out_specs=pl.BlockSpec((None, tq, D_HEADS * V_DIM), lambda b, i: (b, i, 0)),
        out_shape=jax.ShapeDtypeStruct((bsz, tq_len, D_HEADS * V_DIM), BF16),
        scratch_shapes=[pltpu.VMEM((ATTN_HEAD_GROUP, n_blocks, tk, tq), F32),
                        pltpu.VMEM((ATTN_HEAD_GROUP, SUBLANES, tq), F32),
                        pltpu.VMEM((ATTN_HEAD_GROUP, HEAD_PAD, tq), F32)],
        compiler_params=_cparams(("parallel", "arbitrary")),
        name="attention",
    )(q, k, vt)


def _mla_decode_kernel(q_ref, ckv_c_ref, kpe_c_ref, ckv_n_ref, kpe_n_ref, wk_ref, wkt_ref, wv_ref, kg_ref,
                       o_ref, *, tq, s_len, q_pos0):
    s_pad = ckv_c_ref.shape[0] + ckv_n_ref.shape[0]
    heads = [slice(h * HEAD_PAD, (h + 1) * HEAD_PAD) for h in range(D_HEADS)]
    ckv = jnp.concatenate([ckv_c_ref[...], ckv_n_ref[...]], axis=0).astype(BF16)
    kpe_raw = jnp.concatenate([kpe_c_ref[...], kpe_n_ref[...]], axis=0).astype(BF16)
    lane = lax.broadcasted_iota(jnp.int32, (ROPE_DIM, LANES), 1)
    src = lax.broadcasted_iota(jnp.int32, (ROPE_DIM, LANES), 0)
    place = jnp.where(lane == src + NOPE_DIM, 1.0, 0.0).astype(BF16)
    kpe = _dot(kpe_raw, place).astype(BF16)
    eye = (lax.broadcasted_iota(jnp.int32, (LANES, LANES), 0)
           == lax.broadcasted_iota(jnp.int32, (LANES, LANES), 1))
    kpe_t = _dot_nt(jnp.where(eye, 1.0, 0.0).astype(BF16), kpe)
    pe_ss = jnp.sum(kpe_t * kpe_t, axis=0, keepdims=True)
    kn_t = _dot_nt(wkt_ref[...], ckv)
    inv_rms, q_lat, q_g = [], [], []
    for h, sl in enumerate(heads):
        blk = kn_t[h * NOPE_DIM:(h + 1) * NOPE_DIM, :]
        ss = jnp.sum(blk * blk, axis=0, keepdims=True) + pe_ss
        inv_rms.append(jnp.broadcast_to(lax.rsqrt(ss * (1.0 / QK_DIM) + EPS), (tq, s_pad)))
        qg = (q_ref[:, sl].astype(F32) * kg_ref[...]).astype(BF16)
        q_g.append(qg)
        q_lat.append(_dot_nt(qg, wk_ref[:, sl]).astype(BF16))
    s = _dot_nt(jnp.concatenate(q_lat, axis=0), ckv) + _dot_nt(jnp.concatenate(q_g, axis=0), kpe)
    s = s * jnp.concatenate(inv_rms, axis=0)
    row = lax.broadcasted_iota(jnp.int32, s.shape, 0)
    col = lax.broadcasted_iota(jnp.int32, s.shape, 1)
    q_chunk = _div_pow2(q_pos0 + _mod_pow2(row, tq), CHUNK)
    visible = (_div_pow2(col, CHUNK) <= q_chunk) & (col < s_len)
    s = jnp.where(visible, s, NEG_BIG)
    p = jnp.exp2(s - jnp.max(s, axis=-1, keepdims=True))
    lat = _dot(p.astype(BF16), ckv) / jnp.sum(p, axis=-1, keepdims=True)
    lat = lat.astype(BF16)
    for hp in range(D_HEADS // 2):
        both = jnp.concatenate([lat[2 * hp * tq:(2 * hp + 1) * tq], lat[(2 * hp + 1) * tq:(2 * hp + 2) * tq]], axis=1)
        o_ref[:, hp * LANES:(hp + 1) * LANES] = _dot(both, wv_ref[hp]).astype(o_ref.dtype)


def mla_decode(q, cache_ckv, cache_kpe, ckv_new, kpe_new, wk_p, wv_pairs, k_gain_p, q_pos0):
    bsz, tq, width = q.shape
    n_cache, n_new = cache_ckv.shape[1], ckv_new.shape[1]
    assert n_cache % LANES == 0 and tq & (tq - 1) == 0
    s_len = n_cache + n_new
    new_pad = -(-n_new // LANES) * LANES
    ckv_new = jnp.pad(ckv_new, ((0, 0), (0, new_pad - n_new), (0, 0)))
    kpe_new = jnp.pad(kpe_new, ((0, 0), (0, new_pad - n_new), (0, 0)))
    wk_rows = wk_p.T.reshape(D_HEADS, HEAD_PAD, KV_LORA)[:, :NOPE_DIM].reshape(D_HEADS * NOPE_DIM, KV_LORA)
    full = lambda r, c: pl.BlockSpec((r, c), lambda b: (0, 0))
    per_b = lambda r, c: pl.BlockSpec((None, r, c), lambda b: (b, 0, 0))
    kern = functools.partial(_mla_decode_kernel, tq=tq, s_len=s_len, q_pos0=q_pos0)
    return pl.pallas_call(
        kern,
        grid=(bsz,),
        in_specs=[
            per_b(tq, width), per_b(n_cache, KV_LORA), per_b(n_cache, ROPE_DIM),
            per_b(new_pad, KV_LORA), per_b(new_pad, ROPE_DIM),
            full(KV_LORA, width), full(D_HEADS * NOPE_DIM, KV_LORA),
            pl.BlockSpec((D_HEADS // 2, 2 * KV_LORA, LANES), lambda b: (0, 0, 0)), full(1, HEAD_PAD),
        ],
        out_specs=per_b(tq, D_HEADS * V_DIM),
        out_shape=jax.ShapeDtypeStruct((bsz, tq, D_HEADS * V_DIM), BF16),
        compiler_params=_cparams(("parallel",)),
        name="mla_decode",
    )(q, cache_ckv, cache_kpe, ckv_new, kpe_new, wk_p, wk_rows, wv_pairs, k_gain_p)


def rope_tables(pos):
    half = ROPE_DIM // 2
    inv = ROPE_BASE ** (-jnp.arange(half, dtype=F32) / half)
    ang = pos.astype(F32)[:, None] * inv[None, :]
    cos, sin = jnp.cos(ang), jnp.sin(ang)
    n = pos.shape[0]
    z = lambda w: jnp.zeros((n, w), F32)
    o = lambda w: jnp.ones((n, w), F32)
    tail = LANES - NOPE_DIM - ROPE_DIM
    q_cos = jnp.concatenate([o(NOPE_DIM), cos, cos, z(tail)], axis=1)
    q_sin = jnp.concatenate([z(NOPE_DIM), sin, sin, z(tail)], axis=1)
    k_cos = jnp.concatenate([cos, cos, z(LANES - ROPE_DIM)], axis=1)
    k_s1 = jnp.concatenate([-sin, z(LANES - half)], axis=1)
    k_s2 = jnp.concatenate([z(half), sin, z(LANES - ROPE_DIM)], axis=1)
    return q_cos, q_sin, k_cos, k_s1, k_s2


def pack_mla_weights(w_uq, w_ukv, q_gain, k_gain):
    pad_q = HEAD_PAD - QK_DIM
    half = ROPE_DIM // 2
    wq = w_uq.reshape(Q_LORA, D_HEADS, QK_DIM)
    wuq_p = jnp.pad(wq, ((0, 0), (0, 0), (0, pad_q))).reshape(Q_LORA, D_HEADS * HEAD_PAD)
    wq_rot = jnp.concatenate([jnp.zeros_like(wq[:, :, :NOPE_DIM]), -wq[:, :, NOPE_DIM + half:],
                              wq[:, :, NOPE_DIM:NOPE_DIM + half], jnp.zeros_like(wq[:, :, :pad_q])], axis=2)
    wuq_p = jnp.concatenate([wuq_p, wq_rot.reshape(Q_LORA, D_HEADS * HEAD_PAD)], axis=1).astype(BF16)
    wkv = w_ukv.reshape(KV_LORA, D_HEADS, NOPE_DIM + V_DIM)
    wk_p = jnp.pad(wkv[:, :, :NOPE_DIM], ((0, 0), (0, 0), (0, HEAD_PAD - NOPE_DIM)))
    wk_p = wk_p.reshape(KV_LORA, D_HEADS * HEAD_PAD).astype(BF16)
    wv_p = jnp.pad(wkv[:, :, NOPE_DIM:], ((0, 0), (0, 0), (0, HEAD_PAD - V_DIM)))
    wv_p = wv_p.reshape(KV_LORA, D_HEADS * HEAD_PAD).astype(BF16)
    place = jnp.pad(jnp.eye(ROPE_DIM, dtype=F32), ((0, LANES - ROPE_DIM), (NOPE_DIM, HEAD_PAD - QK_DIM)))
    wk_full = jnp.concatenate([wk_p, jnp.tile(place, (1, D_HEADS)).astype(BF16)], axis=0)
    q_gain_p = jnp.pad(q_gain, (0, pad_q)).reshape(1, HEAD_PAD)
    k_gain_p = jnp.pad(k_gain, (0, pad_q)).reshape(1, HEAD_PAD)
    wv = wkv[:, :, NOPE_DIM:].reshape(KV_LORA, D_HEADS // 2, 2, V_DIM)
    zero = jnp.zeros_like(wv[:, :, 0])
    wv_pairs = jnp.concatenate([jnp.concatenate([wv[:, :, 0], zero], axis=2),
                                jnp.concatenate([zero, wv[:, :, 1]], axis=2)], axis=0)
    wv_pairs = jnp.transpose(wv_pairs, (1, 0, 2)).astype(BF16)
    return wuq_p, wk_p, wk_full, wv_p, wv_pairs, q_gain_p, k_gain_p


def _mix_even(proj, tail, w, lb, past, tmaj):
    bsz, t, _ = proj.shape
    n = bsz * t
    s0 = None if past is None else past[0][0]
    a_out, s_t = hgrn2(proj, lb, w['hgrn_out_norm'][0], s0)
    x0 = jnp.zeros((bsz, S5_STATE_W), F32) if past is None else s5_pack_state(past[1][0], past[2][0])
    if tmaj is not None:
        u_tb = tail.reshape(t, bsz, B_WIDTH)
    else:
        u_tb = jnp.transpose(proj[:, :, AB_IN - B_WIDTH:], (1, 0, 2))
    b_tb, x_t = s5_mix(u_tb, w['s5_wb'], w['s5_wc'], w['s5_lam'], w['s5_d'], w['s5_w_glu'], w['s5_b_glu'], x0)
    if tmaj is not None:
        b_out = b_tb.reshape(t, bsz * B_WIDTH)
    else:
        b_out = jnp.transpose(b_tb, (1, 0, 2)).reshape(n, B_WIDTH)
    s5re, s5im = s5_unpack_state(x_t)
    return a_out.reshape(n, A_WIDTH), b_out, dict(hgrn=s_t, s5re=s5re, s5im=s5im)


def _mix_odd(proj, w, pos0, past):
    bsz, t, _ = proj.shape
    n = bsz * t
    c_out, v_rows = gmlp(proj, w['gmlp_v_norm'], w['gmlp_w_s'], w['gmlp_b_s'])
    tables = rope_tables(pos0 + jnp.arange(t, dtype=jnp.int32))
    q_args = (w['mla_q_norm'], w['mla_kv_norm'], w['mla_wuq_p'], w['mla_q_gain_p'])
    if past is None:
        q, ckv, kpe, k, vt = mla_q(proj, tables, *q_args,
                                   kv_weights=(w['mla_wk_full'], w['mla_wv_p'], w['mla_k_gain_p']))
        d_out = attention(q, k, vt, pos0)
    else:
        flat = mla_q(proj.reshape(1, n, -1), [jnp.tile(tb, (bsz, 1)) for tb in tables], *q_args)
        q, ckv, kpe = [a.reshape(bsz, t, -1) for a in flat]
        d_out = mla_decode(q, past[3][0], past[4][0], ckv, kpe, w['mla_wk_p'], w['mla_wv_pairs'],
                           w['mla_k_gain_p'], pos0)
    return c_out.reshape(n, C_WIDTH), d_out.reshape(n, D_HEADS * V_DIM), dict(gv=v_rows, ckv=ckv, kpe=kpe)


def _trunks(streams, w):
    lb_all = jnp.cumsum(jax.nn.softmax(w['hgrn_lb_logits'], axis=0), axis=0)
    ffn_w = lambda name, l: (w[name + '_norm'][l], w[name + '_w_gate'], w[name + '_w_up'], w[name + '_w_down'], l)
    shapes = [x.shape[:2] for x, _, _ in streams]
    tmajs = [(bsz, t) if t % FFN_ROWS == 0 else None for bsz, t in shapes]
    xs = [x.reshape(-1, D_MODEL) for x, _, _ in streams]
    outs = [{} for _ in streams]
    for l in range(2):
        even = l % 2 == 0
        w_in = w['ab_w_in'] if even else w['cd_w_in_p']
        res = ffn([dict(x=x, time_major=tm if even else None) for x, tm in zip(xs, tmajs)],
                  *ffn_w('ffn1', l), post=(w['mix_norm'][l], w_in))
        groups = []
        for r, (bsz, t), tm, (_, pos0, past), o in zip(res, shapes, tmajs, streams, outs):
            proj = r[1].reshape(bsz, t, -1)
            if even:
                a1, a2, new = _mix_even(proj, r[2] if tm is not None else None, w, lb_all[l], past, tm)
            else:
                a1, a2, new = _mix_odd(proj, w, pos0, past)
            o.update(new)
            groups.append(dict(x=r[0], a1=a1, a2=a2, time_major=tm if even else None))
        pre_w = (w['ab_w_out'], A_WIDTH) if even else (w['cd_w_out'], C_WIDTH)
        xs = [r[0] for r in ffn(groups, *ffn_w('ffn2', l), pre_w=pre_w)]
    return [x.reshape(bsz, t, D_MODEL) for x, (bsz, t) in zip(xs, shapes)], outs


def kernel(x_prompt, x_sample, state_hgrn, state_s5_re, state_s5_im, cache_mla_ckv, cache_mla_kpe, ffn1_norm, ffn1_w_gate, ffn1_w_up, ffn1_w_down, mix_norm, ffn2_norm, ffn2_w_gate, ffn2_w_up, ffn2_w_down, ab_w_in, hgrn_lb_logits, hgrn_out_norm, s5_lambda_re, s5_lambda_im, s5_log_dt, s5_b_re, s5_b_im, s5_c_re, s5_c_im, s5_d, s5_w_glu, s5_b_glu, ab_w_out, cd_w_in, gmlp_v_norm, gmlp_w_s, gmlp_b_s, mla_q_norm, mla_w_uq, mla_kv_norm, mla_w_ukv, mla_q_gain, mla_k_gain, cd_w_out):
    bf = lambda a: a.astype(BF16)
    wb, wc, lam_rows = s5_prepare(s5_lambda_re[0], s5_lambda_im[0], s5_log_dt[0],
                                  s5_b_re[0], s5_b_im[0], s5_c_re[0], s5_c_im[0])
    wuq_p, wk_p, wk_full, wv_p, wv_pairs, q_gain_p, k_gain_p = pack_mla_weights(
        mla_w_uq[0], mla_w_ukv[0], mla_q_gain[0], mla_k_gain[0])
    w = dict(
        ffn1_norm=ffn1_norm, ffn1_w_gate=ffn1_w_gate, ffn1_w_up=ffn1_w_up, ffn1_w_down=ffn1_w_down,
        ffn2_norm=ffn2_norm, ffn2_w_gate=ffn2_w_gate, ffn2_w_up=ffn2_w_up, ffn2_w_down=ffn2_w_down,
        mix_norm=mix_norm, ab_w_in=ab_w_in, hgrn_lb_logits=hgrn_lb_logits, hgrn_out_norm=hgrn_out_norm,
        s5_wb=wb, s5_wc=wc, s5_lam=lam_rows, s5_d=s5_d[0].reshape(B_WIDTH), s5_w_glu=bf(s5_w_glu[0]),
        s5_b_glu=s5_b_glu[0], ab_w_out=ab_w_out, cd_w_out=cd_w_out,
        cd_w_in_p=jnp.pad(cd_w_in, ((0, 0), (0, 0), (0, CD_IN_PAD - CD_IN))),
        gmlp_v_norm=gmlp_v_norm[0], gmlp_w_s=gmlp_w_s[0], gmlp_b_s=gmlp_b_s[0],
        mla_q_norm=mla_q_norm[0], mla_kv_norm=mla_kv_norm[0], mla_wuq_p=wuq_p, mla_q_gain_p=q_gain_p,
        mla_wk_p=wk_p, mla_wk_full=wk_full, mla_wv_p=wv_p, mla_wv_pairs=wv_pairs, mla_k_gain_p=k_gain_p,
    )
    past_len = cache_mla_ckv.shape[2]
    (y_p, y_s), (o_p, o_s) = _trunks(
        [(x_prompt, 0, None),
         (x_sample, past_len, (state_hgrn, state_s5_re, state_s5_im, cache_mla_ckv, cache_mla_kpe))], w)
    e = lambda a: a[None]
    return (y_p, y_s, e(o_p['hgrn']), e(o_s['hgrn']), e(o_p['s5re']), e(o_p['s5im']),
            e(o_s['s5re']), e(o_s['s5im']), e(o_p['gv']), e(o_s['gv']),
            e(o_p['ckv']), e(o_p['kpe']), e(o_s['ckv']), e(o_s['kpe']))
```
